```python
import jax, jax.numpy as jnp
from jax import lax
import numpy as np

D_MODEL = 2048
BATCH = 8
SEQ = 8192
DEPTH = 4

CHUNK = 64
Q_BLOCK = 128
SB_HEADS = 8
SB_HEAD_DIM = 128
SB_WIDTH = SB_HEADS * SB_HEAD_DIM
SGU_GROUPS = 8
SGU_GROUP_DIM = 128
SGU_WIDTH = SGU_GROUPS * SGU_GROUP_DIM
SGU_LEN = 128
D_FF = 4 * D_MODEL
IN_COLS = 3 * SB_WIDTH + 2 * SGU_WIDTH + 2 * D_MODEL
EPS = 1e-6

kernel_name = "hybrid_stickbreak_sgu_block"


def rms_norm(x, g):
    xf = x.astype(jnp.float32)
    y = xf * lax.rsqrt(jnp.mean(xf * xf, axis=-1, keepdims=True) + EPS)
    return (y * g.astype(jnp.float32)).astype(x.dtype)


def layer_norm(x, g, b):
    xf = x.astype(jnp.float32)
    mu = jnp.mean(xf, axis=-1, keepdims=True)
    xc = xf - mu
    y = xc * lax.rsqrt(jnp.mean(xc * xc, axis=-1, keepdims=True) + EPS)
    return (y * g.astype(jnp.float32) + b.astype(jnp.float32)).astype(x.dtype)


def stick_breaking_attention(q, k, v):
    seq = q.shape[2]
    scale = SB_HEAD_DIM ** -0.5
    outs = []
    for blk in range(seq // Q_BLOCK):
        q0 = blk * Q_BLOCK
        kend = q0 + Q_BLOCK
        qb = q[:, :, q0:kend].astype(jnp.float32)
        kb = k[:, :, :kend].astype(jnp.float32)
        vb = v[:, :, :kend]
        z = jnp.einsum('bhtd,bhsd->bhts', qb, kb) * scale
        t_idx = q0 + jnp.arange(Q_BLOCK)[:, None]
        s_idx = jnp.arange(kend)[None, :]
        past = s_idx < t_idx
        log_keep = jnp.where(past, jax.nn.log_sigmoid(-z), 0.0)
        tail = lax.cumsum(log_keep, axis=3, reverse=True) - log_keep
        log_a = jax.nn.log_sigmoid(z) + tail
        a = jnp.where(past, jnp.exp(log_a), 0.0)
        outs.append(jnp.einsum('bhts,bhsd->bhtd', a.astype(v.dtype), vb))
    return jnp.concatenate(outs, axis=2)


def spatial_gating(u, v, ln_g, ln_b, w_s, b_s):
    bsz, seq, _ = v.shape
    v = layer_norm(v, ln_g, ln_b)
    vc = v.reshape(bsz, seq // SGU_LEN, SGU_LEN, SGU_GROUPS, SGU_GROUP_DIM)
    pos = jnp.arange(SGU_LEN)
    mask = (pos[None, :] // CHUNK) <= (pos[:, None] // CHUNK)
    w = jnp.where(mask[None], w_s, jnp.zeros_like(w_s))
    mixed = jnp.einsum('gij,bcjgd->bcigd', w, vc) + b_s.T[:, :, None]
    return u * mixed.reshape(bsz, seq, SGU_WIDTH)


def _fwd_setup_inputs(seed: int = 0) -> dict:
    key = jax.random.key(seed)
    ks = jax.random.split(key, 16)
    f32 = jnp.float32
    nrm = lambda k, shape, s: jax.random.normal(k, shape, f32) * s
    return {
        "x": nrm(ks[0], (BATCH, SEQ, D_MODEL), 1.0),
        "g_mix": 1.0 + nrm(ks[1], (DEPTH, D_MODEL), 0.05),
        "w_in": nrm(ks[2], (DEPTH, D_MODEL, IN_COLS), D_MODEL ** -0.5),
        "g_q": 1.0 + nrm(ks[3], (DEPTH, SB_HEADS, SB_HEAD_DIM), 0.05),
        "g_k": 1.0 + nrm(ks[4], (DEPTH, SB_HEADS, SB_HEAD_DIM), 0.05),
        "sgu_ln_g": 1.0 + nrm(ks[5], (DEPTH, SGU_WIDTH), 0.05),
        "sgu_ln_b": nrm(ks[6], (DEPTH, SGU_WIDTH), 0.02),
        "w_spatial": nrm(ks[7], (DEPTH, SGU_GROUPS, SGU_LEN, SGU_LEN), SGU_LEN ** -0.5),
        "b_spatial": 1.0 + nrm(ks[8], (DEPTH, SGU_GROUPS, SGU_LEN), 0.05),
        "w_oa": nrm(ks[9], (DEPTH, SB_WIDTH, D_MODEL), SB_WIDTH ** -0.5),
        "w_ob": nrm(ks[10], (DEPTH, SGU_WIDTH, D_MODEL), SGU_WIDTH ** -0.5),
        "w_out": nrm(ks[11], (DEPTH, D_MODEL, D_MODEL), D_MODEL ** -0.5),
        "g_ff": 1.0 + nrm(ks[12], (DEPTH, D_MODEL), 0.05),
        "w_ff1": nrm(ks[13], (DEPTH, D_MODEL, D_FF), D_MODEL ** -0.5),
        "w_ff2": nrm(ks[14], (DEPTH, D_FF, D_MODEL), D_FF ** -0.5),
    }


def _fwd_reference(x, g_mix, w_in, g_q, g_k, sgu_ln_g, sgu_ln_b, w_spatial, b_spatial,
              w_oa, w_ob, w_out, g_ff, w_ff1, w_ff2):
    bsz, seq, _ = x.shape
    splits = [SB_WIDTH, 2 * SB_WIDTH, 3 * SB_WIDTH,
              3 * SB_WIDTH + SGU_WIDTH, 3 * SB_WIDTH + 2 * SGU_WIDTH,
              3 * SB_WIDTH + 2 * SGU_WIDTH + D_MODEL]
    for l in range(DEPTH):
        h = rms_norm(x, g_mix[l])
        proj = h @ w_in[l]
        q, k, v_sb, u, v_sg, gate_a, gate_b = jnp.split(proj, splits, axis=-1)

        q = rms_norm(q.reshape(bsz, seq, SB_HEADS, SB_HEAD_DIM), g_q[l])
        k = rms_norm(k.reshape(bsz, seq, SB_HEADS, SB_HEAD_DIM), g_k[l])
        v_sb = v_sb.reshape(bsz, seq, SB_HEADS, SB_HEAD_DIM)
        o = stick_breaking_attention(q.transpose(0, 2, 1, 3), k.transpose(0, 2, 1, 3),
                                     v_sb.transpose(0, 2, 1, 3))
        y_a = o.transpose(0, 2, 1, 3).reshape(bsz, seq, SB_WIDTH) @ w_oa[l]

        u = jax.nn.gelu(u, approximate=False)
        v_sg = jax.nn.gelu(v_sg, approximate=False)
        s = spatial_gating(u, v_sg, sgu_ln_g[l], sgu_ln_b[l], w_spatial[l], b_spatial[l])
        y_b = s @ w_ob[l]

        merged = jax.nn.sigmoid(gate_a) * y_a + jax.nn.sigmoid(gate_b) * y_b
        x = x + merged @ w_out[l]

        h2 = rms_norm(x, g_ff[l])
        x = x + jnp.square(jax.nn.relu(h2 @ w_ff1[l])) @ w_ff2[l]
    return x


import jax as _jax
import jax.numpy as _jnp

TWIN_FORMAT = 'train_step'
FWD_PARAMS = ['x', 'g_mix', 'w_in', 'g_q', 'g_k', 'sgu_ln_g', 'sgu_ln_b', 'w_spatial', 'b_spatial', 'w_oa', 'w_ob', 'w_out', 'g_ff', 'w_ff1', 'w_ff2']
TWIN_WEIGHTS = ['g_mix', 'w_in', 'g_q', 'g_k', 'sgu_ln_g', 'sgu_ln_b', 'w_spatial', 'b_spatial', 'w_oa', 'w_ob', 'w_out', 'g_ff', 'w_ff1', 'w_ff2']
TWIN_DIFF_INPUT = 'x'
TWIN_INPUTS = ['x', 'g_mix', 'w_in', 'g_q', 'g_k', 'sgu_ln_g', 'sgu_ln_b', 'w_spatial', 'b_spatial', 'w_oa', 'w_ob', 'w_out', 'g_ff', 'w_ff1', 'w_ff2', 'loss_target', 'm_g_mix', 'm_w_in', 'm_g_q', 'm_g_k', 'm_sgu_ln_g', 'm_sgu_ln_b', 'm_w_spatial', 'm_b_spatial', 'm_w_oa', 'm_w_ob', 'm_w_out', 'm_g_ff', 'm_w_ff1', 'm_w_ff2', 'v_g_mix', 'v_w_in', 'v_g_q', 'v_g_k', 'v_sgu_ln_g', 'v_sgu_ln_b', 'v_w_spatial', 'v_b_spatial', 'v_w_oa', 'v_w_ob', 'v_w_out', 'v_g_ff', 'v_w_ff1', 'v_w_ff2']
TWIN_OUTPUTS = ['loss', 'grad_x', 'grad_g_mix', 'grad_w_in', 'grad_g_q', 'grad_g_k', 'grad_sgu_ln_g', 'grad_sgu_ln_b', 'grad_w_spatial', 'grad_b_spatial', 'grad_w_oa', 'grad_w_ob', 'grad_w_out', 'grad_g_ff', 'grad_w_ff1', 'grad_w_ff2', 'delta_g_mix', 'delta_w_in', 'delta_g_q', 'delta_g_k', 'delta_sgu_ln_g', 'delta_sgu_ln_b', 'delta_w_spatial', 'delta_b_spatial', 'delta_w_oa', 'delta_w_ob', 'delta_w_out', 'delta_g_ff', 'delta_w_ff1', 'delta_w_ff2', 'new_m_g_mix', 'new_m_w_in', 'new_m_g_q', 'new_m_g_k', 'new_m_sgu_ln_g', 'new_m_sgu_ln_b', 'new_m_w_spatial', 'new_m_b_spatial', 'new_m_w_oa', 'new_m_w_ob', 'new_m_w_out', 'new_m_g_ff', 'new_m_w_ff1', 'new_m_w_ff2', 'new_v_g_mix', 'new_v_w_in', 'new_v_g_q', 'new_v_g_k', 'new_v_sgu_ln_g', 'new_v_sgu_ln_b', 'new_v_w_spatial', 'new_v_b_spatial', 'new_v_w_oa', 'new_v_w_ob', 'new_v_w_out', 'new_v_g_ff', 'new_v_w_ff1', 'new_v_w_ff2']
TWIN_LEAF_KINDS = {'loss': 'loss', 'grad_x': 'grad_x', 'grad_g_mix': 'grad_w', 'grad_w_in': 'grad_w', 'grad_g_q': 'grad_w', 'grad_g_k': 'grad_w', 'grad_sgu_ln_g': 'grad_w', 'grad_sgu_ln_b': 'grad_w', 'grad_w_spatial': 'grad_w', 'grad_b_spatial': 'grad_w', 'grad_w_oa': 'grad_w', 'grad_w_ob': 'grad_w', 'grad_w_out': 'grad_w', 'grad_g_ff': 'grad_w', 'grad_w_ff1': 'grad_w', 'grad_w_ff2': 'grad_w', 'delta_g_mix': 'delta_w', 'delta_w_in': 'delta_w', 'delta_g_q': 'delta_w', 'delta_g_k': 'delta_w', 'delta_sgu_ln_g': 'delta_w', 'delta_sgu_ln_b': 'delta_w', 'delta_w_spatial': 'delta_w', 'delta_b_spatial': 'delta_w', 'delta_w_oa': 'delta_w', 'delta_w_ob': 'delta_w', 'delta_w_out': 'delta_w', 'delta_g_ff': 'delta_w', 'delta_w_ff1': 'delta_w', 'delta_w_ff2': 'delta_w', 'new_m_g_mix': 'new_m', 'new_m_w_in': 'new_m', 'new_m_g_q': 'new_m', 'new_m_g_k': 'new_m', 'new_m_sgu_ln_g': 'new_m', 'new_m_sgu_ln_b': 'new_m', 'new_m_w_spatial': 'new_m', 'new_m_b_spatial': 'new_m', 'new_m_w_oa': 'new_m', 'new_m_w_ob': 'new_m', 'new_m_w_out': 'new_m', 'new_m_g_ff': 'new_m', 'new_m_w_ff1': 'new_m', 'new_m_w_ff2': 'new_m', 'new_v_g_mix': 'new_v', 'new_v_w_in': 'new_v', 'new_v_g_q': 'new_v', 'new_v_g_k': 'new_v', 'new_v_sgu_ln_g': 'new_v', 'new_v_sgu_ln_b': 'new_v', 'new_v_w_spatial': 'new_v', 'new_v_b_spatial': 'new_v', 'new_v_w_oa': 'new_v', 'new_v_w_ob': 'new_v', 'new_v_w_out': 'new_v', 'new_v_g_ff': 'new_v', 'new_v_w_ff1': 'new_v', 'new_v_w_ff2': 'new_v'}


def _forward(args):
    return _fwd_reference(*[args[k] for k in FWD_PARAMS])


def _output_shape():
    def fwd():
        inp = _fwd_setup_inputs(0)
        return _fwd_reference(*[inp[k] for k in FWD_PARAMS])
    out = _jax.eval_shape(fwd)
    return out.shape, out.dtype

N_MICROBATCH = 1
ADAM_LR = 0.001
ADAM_B1 = 0.9
ADAM_B2 = 0.999
ADAM_EPS = 1e-08
ADAM_WD = 0.01
ADAM_STEP = 10
PER_EXAMPLE_BATCH_AXIS = {'x': 0, 'loss_target': 0}
SHARED_INPUTS = []
_WEIGHT_DTYPES = {'g_mix': _jnp.float32, 'w_in': _jnp.float32, 'g_q': _jnp.float32, 'g_k': _jnp.float32, 'sgu_ln_g': _jnp.float32, 'sgu_ln_b': _jnp.float32, 'w_spatial': _jnp.float32, 'b_spatial': _jnp.float32, 'w_oa': _jnp.float32, 'w_ob': _jnp.float32, 'w_out': _jnp.float32, 'g_ff': _jnp.float32, 'w_ff1': _jnp.float32, 'w_ff2': _jnp.float32}
MOMENT_SCALE = {'g_mix': 1.768614e+01, 'w_in': 4.852395e+00, 'g_q': 1.964893e+00, 'g_k': 1.977280e+00, 'sgu_ln_g': 7.174519e+00, 'sgu_ln_b': 2.224890e+00, 'w_spatial': 6.596470e-01, 'b_spatial': 8.118783e+00, 'w_oa': 8.527707e+00, 'w_ob': 1.002678e+01, 'w_out': 1.315648e+01, 'g_ff': 1.007373e+02, 'w_ff1': 8.806613e+00, 'w_ff2': 3.024230e+01}


def _to_microbatches(a, axis):
    t = _jnp.moveaxis(a, axis, 0)
    t = t.reshape((N_MICROBATCH, t.shape[0] // N_MICROBATCH) + t.shape[1:])
    return _jnp.moveaxis(t, 1, axis + 1)


def setup_inputs(seed: int = 0) -> dict:
    inp = _fwd_setup_inputs(seed)
    key = _jax.random.fold_in(_jax.random.key(seed), 7919)
    shape, _ = _output_shape()
    out = dict(inp)
    out["loss_target"] = _jax.random.normal(_jax.random.fold_in(key, 0), shape, _jnp.float32)
    for i, name in enumerate(TWIN_WEIGHTS):
        w = inp[name].astype(_jnp.float32)
        if MOMENT_SCALE is None:
            s = _jnp.sqrt(_jnp.mean(_jnp.square(w)) + 1e-30)
        else:
            s = MOMENT_SCALE[name]
        km, kv = _jax.random.split(_jax.random.fold_in(key, i + 1))
        out[name] = w
        out["m_" + name] = s * _jax.random.normal(km, w.shape, _jnp.float32)
        out["v_" + name] = (s * s) * _jax.random.uniform(kv, w.shape, _jnp.float32, 0.5, 1.5)
    if N_MICROBATCH > 1:
        for name, axis in PER_EXAMPLE_BATCH_AXIS.items():
            out[name] = _to_microbatches(out[name], axis)
    return {'x': out['x'], 'g_mix': out['g_mix'], 'w_in': out['w_in'], 'g_q': out['g_q'], 'g_k': out['g_k'], 'sgu_ln_g': out['sgu_ln_g'], 'sgu_ln_b': out['sgu_ln_b'], 'w_spatial': out['w_spatial'], 'b_spatial': out['b_spatial'], 'w_oa': out['w_oa'], 'w_ob': out['w_ob'], 'w_out': out['w_out'], 'g_ff': out['g_ff'], 'w_ff1': out['w_ff1'], 'w_ff2': out['w_ff2'], 'loss_target': out['loss_target'], 'm_g_mix': out['m_g_mix'], 'm_w_in': out['m_w_in'], 'm_g_q': out['m_g_q'], 'm_g_k': out['m_g_k'], 'm_sgu_ln_g': out['m_sgu_ln_g'], 'm_sgu_ln_b': out['m_sgu_ln_b'], 'm_w_spatial': out['m_w_spatial'], 'm_b_spatial': out['m_b_spatial'], 'm_w_oa': out['m_w_oa'], 'm_w_ob': out['m_w_ob'], 'm_w_out': out['m_w_out'], 'm_g_ff': out['m_g_ff'], 'm_w_ff1': out['m_w_ff1'], 'm_w_ff2': out['m_w_ff2'], 'v_g_mix': out['v_g_mix'], 'v_w_in': out['v_w_in'], 'v_g_q': out['v_g_q'], 'v_g_k': out['v_g_k'], 'v_sgu_ln_g': out['v_sgu_ln_g'], 'v_sgu_ln_b': out['v_sgu_ln_b'], 'v_w_spatial': out['v_w_spatial'], 'v_b_spatial': out['v_b_spatial'], 'v_w_oa': out['v_w_oa'], 'v_w_ob': out['v_w_ob'], 'v_w_out': out['v_w_out'], 'v_g_ff': out['v_g_ff'], 'v_w_ff1': out['v_w_ff1'], 'v_w_ff2': out['v_w_ff2']}


def _loss(weights, diff, rest, loss_target):
    with _jax.named_scope("forward"):
        args = {**rest, TWIN_DIFF_INPUT: diff, **{k: w.astype(_WEIGHT_DTYPES[k]) for k, w in weights.items()}}
        y = _forward(args)
    with _jax.named_scope("loss_head"):
        err = _jnp.square(y.astype(_jnp.float32) - loss_target)
        return 0.5 * _jnp.sum(_jnp.mean(err, axis=-1)) if err.ndim else 0.5 * err


def _adamw(w, g, m, v):
    m = ADAM_B1 * m + (1.0 - ADAM_B1) * g
    v = ADAM_B2 * v + (1.0 - ADAM_B2) * _jnp.square(g)
    m_hat = m / (1.0 - ADAM_B1 ** ADAM_STEP)
    v_hat = v / (1.0 - ADAM_B2 ** ADAM_STEP)
    delta = -ADAM_LR * (m_hat / (_jnp.sqrt(v_hat) + ADAM_EPS) + ADAM_WD * w)
    return delta, m, v


def reference(x, g_mix, w_in, g_q, g_k, sgu_ln_g, sgu_ln_b, w_spatial, b_spatial, w_oa, w_ob, w_out, g_ff, w_ff1, w_ff2, loss_target, m_g_mix, m_w_in, m_g_q, m_g_k, m_sgu_ln_g, m_sgu_ln_b, m_w_spatial, m_b_spatial, m_w_oa, m_w_ob, m_w_out, m_g_ff, m_w_ff1, m_w_ff2, v_g_mix, v_w_in, v_g_q, v_g_k, v_sgu_ln_g, v_sgu_ln_b, v_w_spatial, v_b_spatial, v_w_oa, v_w_ob, v_w_out, v_g_ff, v_w_ff1, v_w_ff2):
    given = dict(x=x, g_mix=g_mix, w_in=w_in, g_q=g_q, g_k=g_k, sgu_ln_g=sgu_ln_g, sgu_ln_b=sgu_ln_b, w_spatial=w_spatial, b_spatial=b_spatial, w_oa=w_oa, w_ob=w_ob, w_out=w_out, g_ff=g_ff, w_ff1=w_ff1, w_ff2=w_ff2, loss_target=loss_target, m_g_mix=m_g_mix, m_w_in=m_w_in, m_g_q=m_g_q, m_g_k=m_g_k, m_sgu_ln_g=m_sgu_ln_g, m_sgu_ln_b=m_sgu_ln_b, m_w_spatial=m_w_spatial, m_b_spatial=m_b_spatial, m_w_oa=m_w_oa, m_w_ob=m_w_ob, m_w_out=m_w_out, m_g_ff=m_g_ff, m_w_ff1=m_w_ff1, m_w_ff2=m_w_ff2, v_g_mix=v_g_mix, v_w_in=v_w_in, v_g_q=v_g_q, v_g_k=v_g_k, v_sgu_ln_g=v_sgu_ln_g, v_sgu_ln_b=v_sgu_ln_b, v_w_spatial=v_w_spatial, v_b_spatial=v_b_spatial, v_w_oa=v_w_oa, v_w_ob=v_w_ob, v_w_out=v_w_out, v_g_ff=v_g_ff, v_w_ff1=v_w_ff1, v_w_ff2=v_w_ff2)
    weights = {n: given[n] for n in TWIN_WEIGHTS}
    shared = {n: given[n] for n in SHARED_INPUTS}
    per_example = {n: given[n] for n in ['x']}
    grad_fn = _jax.value_and_grad(_loss, argnums=(0, 1))

    def one_microbatch(ex, loss_target):
        ex = dict(ex)
        diff = ex.pop(TWIN_DIFF_INPUT)
        return grad_fn(weights, diff, {**shared, **ex}, loss_target)

    if N_MICROBATCH == 1:
        loss, (grad_w, grad_x) = one_microbatch(per_example, given["loss_target"])
    else:
        def body(carry, xs):
            loss_sum, grad_sum = carry
            l_k, (gw_k, gx_k) = one_microbatch(xs[0], xs[1])
            with _jax.named_scope("update"):
                return (loss_sum + l_k, _jax.tree.map(_jnp.add, grad_sum, gw_k)), gx_k

        init = (_jnp.zeros((), _jnp.float32), _jax.tree.map(_jnp.zeros_like, weights))
        (loss, grad_w), grad_x = _jax.lax.scan(body, init, (per_example, given["loss_target"]))
    with _jax.named_scope("update"):
        delta_w, new_m, new_v = {}, {}, {}
        for n in TWIN_WEIGHTS:
            delta_w[n], new_m[n], new_v[n] = _adamw(weights[n], grad_w[n], given["m_" + n], given["v_" + n])
    return (loss, grad_x, *[grad_w[n] for n in TWIN_WEIGHTS], *[delta_w[n] for n in TWIN_WEIGHTS],
            *[new_m[n] for n in TWIN_WEIGHTS], *[new_v[n] for n in TWIN_WEIGHTS])
```

```python
import functools

import jax
import jax.numpy as jnp
from jax import lax
from jax.experimental import pallas as pl
from jax.experimental.pallas import tpu as pltpu

F32 = jnp.float32
BF16 = jnp.bfloat16
MESH = pl.DeviceIdType.MESH

EPS = 1e-6
HEADS = 8
HEAD_DIM = 128
SB_WIDTH = HEADS * HEAD_DIM
GROUPS = 8
GROUP_DIM = 128
SGU_WIDTH = GROUPS * GROUP_DIM
SGU_LEN = 128
CHUNK = 64
ATT_BLOCK = 128
LOG_ZERO = -104.0

ADAM_LR = 0.001
ADAM_B1 = 0.9
ADAM_B2 = 0.999
ADAM_EPS = 1e-08
ADAM_WD = 0.01
ADAM_STEP = 10

N_CHIPS = 4
N_DEV = 8
V7X_VMEM_LIMIT = 48 * 1024 * 1024
LANE = 128


def _params(sem):
    return pltpu.CompilerParams(dimension_semantics=sem, vmem_limit_bytes=V7X_VMEM_LIMIT)


def _pick(dim, pref):
    if dim <= pref:
        return dim
    for t in range(pref - pref % LANE, 0, -LANE):
        if dim % t == 0:
            return t
    raise ValueError(f"no tile for {dim}")


_DIMS = {
    "nn": (((1,), (0,)), ((), ())),
    "nt": (((1,), (1,)), ((), ())),
    "tn": (((0,), (0,)), ((), ())),
}


def _matmul(a, b, mode, out_dtype, name, *, layer=None, res=None, into=None, tm=1024, tn=1024, tk=2048):
    assert a.dtype == BF16 and b.dtype == BF16
    bshape = b.shape[1:] if layer is not None else b.shape
    if mode == "nn":
        (m, k), (k2, n) = a.shape, bshape
    elif mode == "nt":
        (m, k), (n, k2) = a.shape, bshape
    else:
        (k, m), (k2, n) = a.shape, bshape
    assert k == k2
    tm, tn, tk = _pick(m, tm), _pick(n, tn), _pick(k, tk)
    nk = k // tk
    dims = _DIMS[mode]

    a_spec = pl.BlockSpec((tk, tm), lambda i, j, kk: (kk, i)) if mode == "tn" else pl.BlockSpec((tm, tk), lambda i, j, kk: (i, kk))
    if mode == "nt":
        bblk, bidx = (tn, tk), (lambda i, j, kk: (j, kk))
    else:
        bblk, bidx = (tk, tn), (lambda i, j, kk: (kk, j))
    if layer is not None:
        b_spec = pl.BlockSpec((None,) + bblk, lambda i, j, kk: (layer,) + bidx(i, j, kk))
    else:
        b_spec = pl.BlockSpec(bblk, bidx)
    in_specs = [a_spec, b_spec]
    operands = [a, b]
    if res is not None:
        in_specs.append(pl.BlockSpec((tm, tn), lambda i, j, kk: (i, j)))
        operands.append(res)
    aliases = {}
    if into is not None:
        buf, slab = into
        in_specs.append(pl.BlockSpec(memory_space=pl.ANY))
        operands.append(buf)
        aliases = {len(operands) - 1: 0}
        out_shape = jax.ShapeDtypeStruct(buf.shape, buf.dtype)
        out_spec = pl.BlockSpec((None, tm, tn), lambda i, j, kk: (slab, i, j))
        out_dtype = buf.dtype
    else:
        out_shape = jax.ShapeDtypeStruct((m, n), out_dtype)
        out_spec = pl.BlockSpec((tm, tn), lambda i, j, kk: (i, j))

    def body(*refs):
        a_ref, b_ref = refs[0], refs[1]
        pos = 2
        r_ref = None
        if res is not None:
            r_ref = refs[pos]
            pos += 1
        if into is not None:
            pos += 1
        o_ref = refs[pos]
        p = lax.dot_general(a_ref[...], b_ref[...], dims, preferred_element_type=F32)

        def finish(total):
            if r_ref is not None:
                total = total + r_ref[...]
            o_ref[...] = total.astype(out_dtype)

        if nk == 1:
            finish(p)
        else:
            acc_ref = refs[pos + 1]
            kk = pl.program_id(2)

            @pl.when(kk == 0)
            def _():
                acc_ref[...] = p

            @pl.when(kk > 0)
            def _():
                acc_ref[...] += p

            @pl.when(kk == nk - 1)
            def _():
                finish(acc_ref[...])

    return pl.pallas_call(
        body,
        name=name,
        grid=(m // tm, n // tn, nk),
        in_specs=in_specs,
        out_specs=out_spec,
        out_shape=out_shape,
        scratch_shapes=[pltpu.VMEM((tm, tn), F32)] if nk > 1 else [],
        input_output_aliases=aliases,
        compiler_params=_params(("parallel", "parallel", "arbitrary")),
    )(*operands)


def _rowwise(fn, rows, consts, outs, reds, name, *, tr=256, ncol=1, n_rows=None):
    if n_rows is None:
        n_rows = rows[0][0].shape[0]
    tr = min(tr, n_rows)
    assert n_rows % tr == 0
    in_specs, operands = [], []
    for arr, width, coloff, rowoff in rows:
        in_specs.append(pl.BlockSpec((tr, width), functools.partial(lambda j, i, c, r: (i + r, j + c), c=coloff, r=rowoff)))
        operands.append(arr)
    for arr in consts:
        if arr.ndim == 3:
            in_specs.append(pl.BlockSpec(arr.shape, lambda j, i: (0, 0, 0)))
        else:
            in_specs.append(pl.BlockSpec(arr.shape, lambda j, i: (0, 0)))
        operands.append(arr)
    out_specs, out_shape = [], []
    for width, dtype in outs:
        out_specs.append(pl.BlockSpec((tr, width), lambda j, i: (i, j)))
        out_shape.append(jax.ShapeDtypeStruct((n_rows, ncol * width), dtype))
    for shape in reds:
        if len(shape) == 3:
            out_specs.append(pl.BlockSpec(shape, lambda j, i: (0, 0, 0)))
            out_shape.append(jax.ShapeDtypeStruct(shape, F32))
        else:
            out_specs.append(pl.BlockSpec(shape, lambda j, i: (0, j)))
            out_shape.append(jax.ShapeDtypeStruct((shape[0], ncol * shape[1]), F32))
    n_in, n_out, n_red = len(operands), len(outs), len(reds)

    def body(*refs):
        vals = fn(*[r[...] for r in refs[:n_in]])
        if not isinstance(vals, (tuple, list)):
            vals = (vals,)
        assert len(vals) == n_out + n_red
        for o_ref, v in zip(refs[n_in:n_in + n_out], vals[:n_out]):
            o_ref[...] = v.astype(o_ref.dtype)
        if n_red:
            first = pl.program_id(1) == 0
            for o_ref, v in zip(refs[n_in + n_out:], vals[n_out:]):
                @pl.when(first)
                def _(o_ref=o_ref, v=v):
                    o_ref[...] = v

                @pl.when(jnp.logical_not(first))
                def _(o_ref=o_ref, v=v):
                    o_ref[...] += v

    res = pl.pallas_call(
        body,
        name=name,
        grid=(ncol, n_rows // tr),
        in_specs=in_specs,
        out_specs=out_specs,
        out_shape=out_shape,
        compiler_params=_params(("parallel", "arbitrary")),
    )(*operands)
    return res


def _row(arr, width=None, coloff=0, rowoff=0):
    return (arr, arr.shape[1] if width is None else width, coloff, rowoff)


def _rms(x, g):
    return x * lax.rsqrt(jnp.mean(x * x, axis=-1, keepdims=True) + EPS) * g


def _erf(x):
    x = jnp.clip(x, -4.0, 4.0)
    x2 = x * x
    alpha = x2 * -2.72614225801306e-10 + 2.77068142495902e-08
    alpha = alpha * x2 - 2.10102402082508e-06
    alpha = alpha * x2 - 5.69250639462346e-05
    alpha = alpha * x2 - 7.34990630326855e-04
    alpha = alpha * x2 - 2.95459980854025e-03
    alpha = alpha * x2 - 1.60960333262415e-02
    beta = x2 * -1.45660718464996e-05 - 2.13374055278905e-04
    beta = beta * x2 - 1.68282697438203e-03
    beta = beta * x2 - 7.37332916720468e-03
    beta = beta * x2 - 1.42647390514189e-02
    return x * alpha / beta


def _gelu(x):
    return 0.5 * x * (1.0 + _erf(x * (2.0 ** -0.5)))


def _gelu_grad(x):
    cdf = 0.5 * (1.0 + _erf(x * (2.0 ** -0.5)))
    pdf = jnp.exp(-0.5 * x * x) * (2.0 * jnp.pi) ** -0.5
    return cdf + x * pdf


def _layer_norm(x, g, b):
    mu = jnp.mean(x, axis=-1, keepdims=True)
    xc = x - mu
    return xc * lax.rsqrt(jnp.mean(xc * xc, axis=-1, keepdims=True) + EPS) * g + b


def _per_head(fn, x, g):
    parts = []
    for h in range(HEADS):
        sl = slice(h * HEAD_DIM, (h + 1) * HEAD_DIM)
        parts.append(fn(x[:, sl], g[:, sl]))
    return parts


def _sgu_mask():
    i = lax.broadcasted_iota(jnp.int32, (SGU_LEN, SGU_LEN), 0)
    j = lax.broadcasted_iota(jnp.int32, (SGU_LEN, SGU_LEN), 1)
    return (j // CHUNK) <= (i // CHUNK)


def _sgu_mix(vln_bf, w_bf, bias):
    rows = vln_bf.shape[0]
    out_rows = []
    for c in range(rows // SGU_LEN):
        cols = []
        for g in range(GROUPS):
            blk = vln_bf[c * SGU_LEN:(c + 1) * SGU_LEN, g * GROUP_DIM:(g + 1) * GROUP_DIM]
            mixed = jnp.dot(w_bf[g], blk, preferred_element_type=F32) + bias[g]
            cols.append(mixed)
        out_rows.append(jnp.concatenate(cols, axis=1))
    return out_rows[0] if len(out_rows) == 1 else jnp.concatenate(out_rows, axis=0)


def _split_dot(x, m_bf):
    hi = x.astype(BF16)
    lo = (x - hi.astype(F32)).astype(BF16)
    return jnp.dot(hi, m_bf, preferred_element_type=F32) + jnp.dot(lo, m_bf, preferred_element_type=F32)


def _att_block(q, k_ref, j, run_keep, diag):
    blk = ATT_BLOCK
    kb = k_ref[pl.ds(pl.multiple_of(j * blk, blk), blk), :]
    z = lax.dot_general(q, kb, _DIMS["nt"], preferred_element_type=F32) * (HEAD_DIM ** -0.5)
    e = jnp.exp(-jnp.abs(z))
    softplus = jnp.maximum(z, 0.0) + jnp.log(1.0 + e)
    row = lax.broadcasted_iota(jnp.int32, (blk, blk), 0)
    col = lax.broadcasted_iota(jnp.int32, (blk, blk), 1)
    later = (row > col).astype(BF16)
    log_keep = -softplus
    if diag:
        past = col < row
        log_keep = jnp.where(past, log_keep, 0.0)
    tail = _split_dot(log_keep, later) + run_keep
    a = jnp.exp((z - softplus) + tail)
    if diag:
        a = jnp.where(past, a, 0.0)
    inv = 1.0 / (1.0 + e)
    sig = jnp.where(z >= 0.0, inv, e * inv)
    return a, sig, jnp.sum(log_keep, axis=1, keepdims=True)


def _attention_fwd(qn, kn, vb):
    t = qn.shape[0]
    blk = ATT_BLOCK
    nq = t // blk

    def body(q_ref, k_ref, v_ref, o_ref):
        i = pl.program_id(1)
        q = q_ref[...]

        def tile(j, run_keep, acc, diag):
            a, _, keep_sum = _att_block(q, k_ref, j, run_keep, diag)
            vblk = v_ref[pl.ds(pl.multiple_of(j * blk, blk), blk), :]
            acc = acc + jnp.dot(a.astype(BF16), vblk, preferred_element_type=F32)
            return run_keep + keep_sum, acc

        run_keep, acc = tile(i, jnp.zeros((blk, 1), F32), jnp.zeros((blk, HEAD_DIM), F32), True)

        def cond(c):
            j, run_keep, _ = c
            return jnp.logical_and(j >= 0, jnp.max(run_keep) >= LOG_ZERO)

        def step(c):
            j, run_keep, acc = c
            run_keep, acc = tile(j, run_keep, acc, False)
            return j - 1, run_keep, acc

        _, _, acc = lax.while_loop(cond, step, (i - 1, run_keep, acc))
        o_ref[...] = acc.astype(o_ref.dtype)

    return pl.pallas_call(
        body,
        name="attention_fwd",
        grid=(HEADS, nq),
        in_specs=[
            pl.BlockSpec((blk, HEAD_DIM), lambda h, i: (i, h)),
            pl.BlockSpec((t, HEAD_DIM), lambda h, i: (0, h)),
            pl.BlockSpec((t, HEAD_DIM), lambda h, i: (0, h)),
        ],
        out_specs=pl.BlockSpec((blk, HEAD_DIM), lambda h, i: (i, h)),
        out_shape=jax.ShapeDtypeStruct((t, SB_WIDTH), BF16),
        compiler_params=_params(("parallel", "arbitrary")),
    )(qn, kn, vb)


def _attention_bwd(qn, kn, vb, do):
    t = qn.shape[0]
    blk = ATT_BLOCK
    nq = t // blk

    def body(q_ref, k_ref, v_ref, do_ref, dq_ref, dk_ref, dv_ref):
        i = pl.program_id(1)

        @pl.when(i == 0)
        def _():
            dk_ref[...] = jnp.zeros_like(dk_ref)
            dv_ref[...] = jnp.zeros_like(dv_ref)

        q = q_ref[...]
        dob = do_ref[...]
        row = lax.broadcasted_iota(jnp.int32, (blk, blk), 0)
        col = lax.broadcasted_iota(jnp.int32, (blk, blk), 1)
        from_here = (row >= col).astype(BF16)

        def grad_log_a(j, run_keep, diag):
            a, sig, keep_sum = _att_block(q, k_ref, j, run_keep, diag)
            vblk = v_ref[pl.ds(pl.multiple_of(j * blk, blk), blk), :]
            dp = lax.dot_general(dob, vblk, _DIMS["nt"], preferred_element_type=F32)
            return a, sig, keep_sum, a * dp

        def cond(c):
            return jnp.logical_and(c[0] >= 0, jnp.max(c[1]) >= LOG_ZERO)

        _, _, keep_sum, dla = grad_log_a(i, jnp.zeros((blk, 1), F32), True)
        total = jnp.sum(dla, axis=1, keepdims=True)

        def step_a(c):
            j, run_keep, total = c
            _, _, keep_sum, dla = grad_log_a(j, run_keep, False)
            return j - 1, run_keep + keep_sum, total + jnp.sum(dla, axis=1, keepdims=True)

        _, _, total = lax.while_loop(cond, step_a, (i - 1, keep_sum, total))

        def tile(j, run_keep, run_dla, dq, diag):
            a, sig, keep_sum, dla = grad_log_a(j, run_keep, diag)
            later_sum = _split_dot(dla, from_here) + run_dla
            dz = dla * (1.0 - sig) - sig * (total - later_sum)
            if diag:
                dz = jnp.where(col < row, dz, 0.0)
            dzb = (dz * (HEAD_DIM ** -0.5)).astype(BF16)
            rows = pl.ds(pl.multiple_of(j * blk, blk), blk)
            dq = dq + jnp.dot(dzb, k_ref[rows, :], preferred_element_type=F32)
            dk_ref[rows, :] += lax.dot_general(dzb, q, _DIMS["tn"], preferred_element_type=F32)
            dv_ref[rows, :] += lax.dot_general(a.astype(BF16), dob, _DIMS["tn"], preferred_element_type=F32)
            return run_keep + keep_sum, run_dla + jnp.sum(dla, axis=1, keepdims=True), dq

        zero = jnp.zeros((blk, 1), F32)
        run_keep, run_dla, dq = tile(i, zero, zero, jnp.zeros((blk, HEAD_DIM), F32), True)

        def step_b(c):
            j, run_keep, run_dla, dq = c
            run_keep, run_dla, dq = tile(j, run_keep, run_dla, dq, False)
            return j - 1, run_keep, run_dla, dq

        _, _, _, dq = lax.while_loop(cond, step_b, (i - 1, run_keep, run_dla, dq))
        dq_ref[...] = dq

    blk_spec = pl.BlockSpec((blk, HEAD_DIM), lambda h, i: (i, h))
    head_spec = pl.BlockSpec((t, HEAD_DIM), lambda h, i: (0, h))
    full = jax.ShapeDtypeStruct((t, SB_WIDTH), F32)
    return pl.pallas_call(
        body,
        name="attention_bwd",
        grid=(HEADS, nq),
        in_specs=[blk_spec, head_spec, head_spec, blk_spec],
        out_specs=[blk_spec, head_spec, head_spec],
        out_shape=[full, full, full],
        compiler_params=_params(("parallel", "arbitrary")),
    )(qn, kn, vb, do)


def _place():
    return lax.axis_index("x"), lax.axis_index("y"), lax.axis_index("c")


def _other_chips(x, y):
    return [(1 - x, y), (x, 1 - y), (1 - x, 1 - y)]


def _shard_view(ref, axis, index):
    size = ref.shape[axis] // N_CHIPS
    start = pl.multiple_of(index * size, size)
    if axis == 1:
        return ref.at[:, pl.ds(start, size), :]
    return ref.at[:, :, pl.ds(start, size)]


HBM_SPEC = pl.BlockSpec(memory_space=pl.ANY)


def _gather_weights(shards, axes):
    n = len(shards)

    def body(*refs):
        ins, outs = refs[:n], refs[n:2 * n]
        send_sems, recv_sems, local_sems = refs[2 * n:]
        x, y, c = _place()
        me = 2 * x + y
        copies = []
        for t in range(n):
            local = pltpu.make_async_copy(ins[t], _shard_view(outs[t], axes[t], me), local_sems.at[t])
            local.start()
            copies.append(local)
        sends, recvs = [], []
        for t in range(n):
            for k, (px, py) in enumerate(_other_chips(x, y)):
                sem = t * 3 + k
                send = pltpu.make_async_remote_copy(
                    src_ref=ins[t], dst_ref=_shard_view(outs[t], axes[t], me),
                    send_sem=send_sems.at[sem], recv_sem=recv_sems.at[sem],
                    device_id=(px, py, c), device_id_type=MESH)
                send.start()
                sends.append(send)
                recvs.append(pltpu.make_async_remote_copy(
                    src_ref=ins[t], dst_ref=_shard_view(outs[t], axes[t], 2 * px + py),
                    send_sem=send_sems.at[sem], recv_sem=recv_sems.at[sem],
                    device_id=(px, py, c), device_id_type=MESH))
        for cp in copies:
            cp.wait()
        for cp in sends:
            cp.wait_send()
        for cp in recvs:
            cp.wait_recv()

    out_shape = []
    for s, ax in zip(shards, axes):
        shape = list(s.shape)
        shape[ax] *= N_CHIPS
        out_shape.append(jax.ShapeDtypeStruct(tuple(shape), s.dtype))
    return pl.pallas_call(
        body,
        name="gather_weights",
        in_specs=[HBM_SPEC] * n,
        out_specs=[HBM_SPEC] * n,
        out_shape=out_shape,
        scratch_shapes=[pltpu.SemaphoreType.DMA((3 * n,)), pltpu.SemaphoreType.DMA((3 * n,)), pltpu.SemaphoreType.DMA((n,))],
    )(*shards)


def _scatter_grads(fulls, axes):
    n = len(fulls)

    def body(*refs):
        ins, outs = refs[:n], refs[n:2 * n]
        send_sems, recv_sems, local_sems = refs[2 * n:]
        x, y, c = _place()
        me = 2 * x + y
        copies, sends, recvs = [], [], []
        for t in range(n):
            local = pltpu.make_async_copy(_shard_view(ins[t], axes[t], me), outs[t].at[3], local_sems.at[t])
            local.start()
            copies.append(local)
        for t in range(n):
            for k, (px, py) in enumerate(_other_chips(x, y)):
                sem = t * 3 + k
                send = pltpu.make_async_remote_copy(
                    src_ref=_shard_view(ins[t], axes[t], 2 * px + py), dst_ref=outs[t].at[k],
                    send_sem=send_sems.at[sem], recv_sem=recv_sems.at[sem],
                    device_id=(px, py, c), device_id_type=MESH)
                send.start()
                sends.append(send)
                recvs.append(send)
        for cp in copies:
            cp.wait()
        for cp in sends:
            cp.wait_send()
        for cp in recvs:
            cp.wait_recv()

    out_shape = []
    for s, ax in zip(fulls, axes):
        shape = list(s.shape)
        shape[ax] //= N_CHIPS
        out_shape.append(jax.ShapeDtypeStruct((4,) + tuple(shape), s.dtype))
    return pl.pallas_call(
        body,
        name="scatter_grads",
        in_specs=[HBM_SPEC] * n,
        out_specs=[HBM_SPEC] * n,
        out_shape=out_shape,
        scratch_shapes=[pltpu.SemaphoreType.DMA((3 * n,)), pltpu.SemaphoreType.DMA((3 * n,)), pltpu.SemaphoreType.DMA((n,))],
    )(*fulls)


def _swap_with_sibling(arrs):
    n = len(arrs)

    def body(*refs):
        ins, outs = refs[:n], refs[n:2 * n]
        send_sems, recv_sems = refs[2 * n:]
        x, y, c = _place()
        copies = []
        for t in range(n):
            cp = pltpu.make_async_remote_copy(
                src_ref=ins[t], dst_ref=outs[t], send_sem=send_sems.at[t], recv_sem=recv_sems.at[t],
                device_id=(x, y, 1 - c), device_id_type=MESH)
            cp.start()
            copies.append(cp)
        for cp in copies:
            cp.wait_send()
        for cp in copies:
            cp.wait_recv()

    return pl.pallas_call(
        body,
        name="swap_with_sibling",
        in_specs=[HBM_SPEC] * n,
        out_specs=[HBM_SPEC] * n,
        out_shape=[jax.ShapeDtypeStruct(a.shape, a.dtype) for a in arrs],
        scratch_shapes=[pltpu.SemaphoreType.DMA((n,)), pltpu.SemaphoreType.DMA((n,))],
    )(*arrs)


def _sum_over_devices(v):
    rows, width = v.shape

    def body(v_ref, o_ref, buf, send_sems, recv_sems):
        x, y, c = _place()
        me = 4 * x + 2 * y + c
        buf[me] = v_ref[...]
        sends = []
        for d in range(1, N_DEV):
            px, py, pc = x ^ (d >> 2), y ^ ((d >> 1) & 1), c ^ (d & 1)
            cp = pltpu.make_async_remote_copy(
                src_ref=v_ref, dst_ref=buf.at[me], send_sem=send_sems.at[d - 1], recv_sem=recv_sems.at[d - 1],
                device_id=(px, py, pc), device_id_type=MESH)
            cp.start()
            sends.append((cp, 4 * px + 2 * py + pc))
        for cp, _ in sends:
            cp.wait_send()
        for d, (cp, peer) in enumerate(sends):
            pltpu.make_async_remote_copy(
                src_ref=v_ref, dst_ref=buf.at[peer], send_sem=send_sems.at[d], recv_sem=recv_sems.at[d],
                device_id=(x, y, c), device_id_type=MESH).wait_recv()
        total = buf[0]
        for d in range(1, N_DEV):
            total = total + buf[d]
        o_ref[...] = total

    vmem = pl.BlockSpec(memory_space=pltpu.VMEM)
    return pl.pallas_call(
        body,
        name="sum_over_devices",
        in_specs=[vmem],
        out_specs=vmem,
        out_shape=jax.ShapeDtypeStruct((rows, width), F32),
        scratch_shapes=[pltpu.VMEM((N_DEV, rows, width), F32), pltpu.SemaphoreType.DMA((N_DEV - 1,)), pltpu.SemaphoreType.DMA((N_DEV - 1,))],
        compiler_params=pltpu.CompilerParams(vmem_limit_bytes=V7X_VMEM_LIMIT),
    )(v)


def _adamw_math(w, g, m, v):
    m = ADAM_B1 * m + (1.0 - ADAM_B1) * g
    v = ADAM_B2 * v + (1.0 - ADAM_B2) * (g * g)
    m_hat = m / (1.0 - ADAM_B1 ** ADAM_STEP)
    v_hat = v / (1.0 - ADAM_B2 ** ADAM_STEP)
    delta = -ADAM_LR * (m_hat / (jnp.sqrt(v_hat) + ADAM_EPS) + ADAM_WD * w)
    return delta, m, v


def _adamw_sharded(part_a, part_b, w, m, v, name):
    width = _pick(w.shape[1], 1152)
    ncol = w.shape[1] // width

    def fn(a, b, w, m, v):
        g = a + b
        delta, m, v = _adamw_math(w, g, m, v)
        return g, delta, m, v

    return _rowwise(fn, [_row(t, width) for t in (part_a, part_b, w, m, v)], [], [(width, F32)] * 4, [], name, ncol=ncol)


def _adamw_small(g, w, m, v):
    def fn(g, w, m, v):
        return _adamw_math(w, g, m, v)

    return _rowwise(fn, [_row(t) for t in (g, w, m, v)], [], [(g.shape[1], F32)] * 3, [], "adamw_small", tr=g.shape[0])


def _flat(a):
    return a.reshape(-1, a.shape[-1])


def kernel(x, g_mix, w_in, g_q, g_k, sgu_ln_g, sgu_ln_b, w_spatial, b_spatial, w_oa, w_ob, w_out, g_ff, w_ff1, w_ff2, loss_target, m_g_mix, m_w_in, m_g_q, m_g_k, m_sgu_ln_g, m_sgu_ln_b, m_w_spatial, m_b_spatial, m_w_oa, m_w_ob, m_w_out, m_g_ff, m_w_ff1, m_w_ff2, v_g_mix, v_w_in, v_g_q, v_g_k, v_sgu_ln_g, v_sgu_ln_b, v_w_spatial, v_b_spatial, v_w_oa, v_w_ob, v_w_out, v_g_ff, v_w_ff1, v_w_ff2):
    depth = g_mix.shape[0]
    seq, d_model = x.shape[1], x.shape[2]
    d_ff = w_ff1.shape[2] * N_CHIPS
    in_cols = w_in.shape[2] * N_CHIPS
    col_gate_a = 3 * SB_WIDTH + 2 * SGU_WIDTH
    assert in_cols == col_gate_a + 2 * d_model

    big = [w_in, w_oa, w_ob, w_out, w_ff1, w_ff2]
    big_m = [m_w_in, m_w_oa, m_w_ob, m_w_out, m_w_ff1, m_w_ff2]
    big_v = [v_w_in, v_w_oa, v_w_ob, v_w_out, v_w_ff1, v_w_ff2]
    axes = [2, 2, 2, 1, 2, 1]
    W_IN, W_OA, W_OB, W_OUT, W_FF1, W_FF2 = range(6)
    full = _gather_weights([w.astype(BF16) for w in big], axes)

    gate_w = _pick(d_model, 1024)
    gate_n = d_model // gate_w
    ff_w = _pick(d_ff, 1024)
    ff_n = d_ff // ff_w
    bias_col = b_spatial[..., None]

    def rms_fwd(xin, g, name):
        return _rowwise(lambda xb, gb: _rms(xb, gb), [_row(xin)], [g], [(d_model, BF16)], [], name)[0]

    saved = []
    cur = x.reshape(seq, d_model)
    for l in range(depth):
        hb = rms_fwd(cur, g_mix[l:l + 1], "rms_mix")
        proj = _matmul(hb, full[W_IN], "nn", F32, "proj", layer=l)

        def qk_fn(q, k, v, gq, gk):
            qn = jnp.concatenate(_per_head(_rms, q, gq), axis=1)
            kn = jnp.concatenate(_per_head(_rms, k, gk), axis=1)
            return qn, kn, v

        gq, gk = g_q[l].reshape(1, SB_WIDTH), g_k[l].reshape(1, SB_WIDTH)
        qn, kn, vb = _rowwise(qk_fn, [_row(proj, SB_WIDTH, 0), _row(proj, SB_WIDTH, 1), _row(proj, SB_WIDTH, 2)],
                              [gq, gk], [(SB_WIDTH, BF16)] * 3, [], "qk_norm")
        o = _attention_fwd(qn, kn, vb)
        ya = _matmul(o, full[W_OA], "nn", F32, "proj_a", layer=l)

        def sgu_fn(u_pre, v_pre, ln_g, ln_b, w_s, b_t):
            w_bf = jnp.where(_sgu_mask()[None], w_s, 0.0).astype(BF16)
            vln = _layer_norm(_gelu(v_pre), ln_g, ln_b)
            return _gelu(u_pre) * _sgu_mix(vln.astype(BF16), w_bf, b_t)

        sgu_consts = [sgu_ln_g[l:l + 1], sgu_ln_b[l:l + 1], w_spatial[l], bias_col[l]]
        s = _rowwise(sgu_fn, [_row(proj, SGU_WIDTH, 3), _row(proj, SGU_WIDTH, 4)], sgu_consts, [(SGU_WIDTH, BF16)], [], "sgu_fwd")[0]
        yb = _matmul(s, full[W_OB], "nn", F32, "proj_b", layer=l)

        def merge_fn(ga, gb, a, b):
            return jax.nn.sigmoid(ga) * a + jax.nn.sigmoid(gb) * b

        gate_rows = [_row(proj, gate_w, col_gate_a // gate_w), _row(proj, gate_w, (col_gate_a + d_model) // gate_w)]
        merged = _rowwise(merge_fn, gate_rows + [_row(ya, gate_w), _row(yb, gate_w)], [], [(gate_w, BF16)], [], "merge", ncol=gate_n)[0]
        x1 = _matmul(merged, full[W_OUT], "nn", F32, "proj_out", layer=l, res=cur)
        h2 = rms_fwd(x1, g_ff[l:l + 1], "rms_ff")
        a1 = _matmul(h2, full[W_FF1], "nn", F32, "ff1", layer=l)
        r = _rowwise(lambda a: jnp.square(jnp.maximum(a, 0.0)), [_row(a1, ff_w)], [], [(ff_w, BF16)], [], "relu_sq", ncol=ff_n)[0]
        x2 = _matmul(r, full[W_FF2], "nn", F32, "ff2", layer=l, res=x1)
        saved.append(dict(x=cur, hb=hb, proj=proj, qn=qn, kn=kn, vb=vb, o=o, ya=ya, yb=yb, s=s, merged=merged,
                          x1=x1, h2=h2, a1=a1, r=r, gq=gq, gk=gk, sgu_consts=sgu_consts, gate_rows=gate_rows))
        cur = x2

    def loss_fn(y, target):
        err = y - target
        per_row = jnp.sum(err * err, axis=-1, keepdims=True) * (1.0 / d_model)
        part = 0.5 * jnp.sum(per_row, axis=0, keepdims=True)
        dy = err * (1.0 / d_model)
        return dy, dy, jnp.broadcast_to(part, (8, LANE))

    dx, dxb, loss_part = _rowwise(loss_fn, [_row(cur), _row(loss_target.reshape(seq, d_model))], [],
                                  [(d_model, F32), (d_model, BF16)], [(8, LANE)], "loss")
    loss = lax.psum(loss_part[0, 0], ("x", "y", "c"))

    grads_big = [jnp.zeros((depth,) + s.shape[1:], BF16) for s in full]
    small = {n: [None] * depth for n in ("g_mix", "g_q", "g_k", "ln_g", "ln_b", "w_s", "b_s", "g_ff")}

    def rms_bwd(dh, xin, dres, g, name):
        def fn(dh, xin, dres, g):
            _, vjp = jax.vjp(_rms, xin, g)
            dxin, dg = vjp(dh)
            total = dres + dxin
            return total, total, dg

        return _rowwise(fn, [_row(dh), _row(xin), _row(dres)], [g], [(d_model, F32), (d_model, BF16)], [(1, d_model)], name)

    for l in reversed(range(depth)):
        sv = saved[l]
        dr = _matmul(dxb, full[W_FF2], "nt", F32, "ff2_dx", layer=l)
        grads_big[W_FF2] = _matmul(sv["r"], dxb, "tn", BF16, "ff2_dw", into=(grads_big[W_FF2], l))
        da1 = _rowwise(lambda d, a: d * (2.0 * jnp.maximum(a, 0.0)), [_row(dr, ff_w), _row(sv["a1"], ff_w)], [],
                       [(ff_w, BF16)], [], "relu_sq_bwd", ncol=ff_n)[0]
        dh2 = _matmul(da1, full[W_FF1], "nt", F32, "ff1_dx", layer=l)
        grads_big[W_FF1] = _matmul(sv["h2"], da1, "tn", BF16, "ff1_dw", into=(grads_big[W_FF1], l))
        dx1, dx1b, small["g_ff"][l] = rms_bwd(dh2, sv["x1"], dx, g_ff[l:l + 1], "rms_ff_bwd")

        dmerged = _matmul(dx1b, full[W_OUT], "nt", F32, "out_dx", layer=l)
        grads_big[W_OUT] = _matmul(sv["merged"], dx1b, "tn", BF16, "out_dw", into=(grads_big[W_OUT], l))

        def merge_bwd_fn(dm, ga, gb, a, b):
            sa, sb = jax.nn.sigmoid(ga), jax.nn.sigmoid(gb)
            return dm * a * sa * (1.0 - sa), dm * b * sb * (1.0 - sb), dm * sa, dm * sb

        dga, dgb, dya, dyb = _rowwise(merge_bwd_fn, [_row(dmerged, gate_w)] + sv["gate_rows"] + [_row(sv["ya"], gate_w), _row(sv["yb"], gate_w)],
                                      [], [(gate_w, BF16)] * 4, [], "merge_bwd", ncol=gate_n)

        ds = _matmul(dyb, full[W_OB], "nt", F32, "ob_dx", layer=l)
        grads_big[W_OB] = _matmul(sv["s"], dyb, "tn", BF16, "ob_dw", into=(grads_big[W_OB], l))

        def sgu_bwd_fn(ds, u_pre, v_pre, ln_g, ln_b, w_s, b_t):
            mask = _sgu_mask()
            w_bf = jnp.where(mask[None], w_s, 0.0).astype(BF16)
            vln, ln_vjp = jax.vjp(lambda vg, g, b: _layer_norm(vg, g, b), _gelu(v_pre), ln_g, ln_b)
            vln_bf = vln.astype(BF16)
            mixed = _sgu_mix(vln_bf, w_bf, b_t)
            du_pre = ds * mixed * _gelu_grad(u_pre)
            dmixed = ds * _gelu(u_pre)
            dm_bf = dmixed.astype(BF16)
            dw = [jnp.zeros((SGU_LEN, SGU_LEN), F32) for _ in range(GROUPS)]
            db = [jnp.zeros((SGU_LEN, 1), F32) for _ in range(GROUPS)]
            dvln_rows = []
            for c in range(ds.shape[0] // SGU_LEN):
                rows = slice(c * SGU_LEN, (c + 1) * SGU_LEN)
                cols = []
                for g in range(GROUPS):
                    sl = slice(g * GROUP_DIM, (g + 1) * GROUP_DIM)
                    cols.append(lax.dot_general(w_bf[g], dm_bf[rows, sl], _DIMS["tn"], preferred_element_type=F32))
                    dw[g] = dw[g] + lax.dot_general(dm_bf[rows, sl], vln_bf[rows, sl], _DIMS["nt"], preferred_element_type=F32)
                    db[g] = db[g] + jnp.sum(dmixed[rows, sl], axis=1, keepdims=True)
                dvln_rows.append(jnp.concatenate(cols, axis=1))
            dvln = dvln_rows[0] if len(dvln_rows) == 1 else jnp.concatenate(dvln_rows, axis=0)
            dvg, dln_g, dln_b = ln_vjp(dvln)
            dv_pre = dvg * _gelu_grad(v_pre)
            dw_s = jnp.stack([jnp.where(mask, d, 0.0) for d in dw])
            return du_pre, dv_pre, dln_g, dln_b, dw_s, jnp.stack(db)

        du, dvs, small["ln_g"][l], small["ln_b"][l], small["w_s"][l], db_col = _rowwise(
            sgu_bwd_fn, [_row(ds), _row(sv["proj"], SGU_WIDTH, 3), _row(sv["proj"], SGU_WIDTH, 4)], sv["sgu_consts"],
            [(SGU_WIDTH, BF16)] * 2, [(1, SGU_WIDTH), (1, SGU_WIDTH), (GROUPS, SGU_LEN, SGU_LEN), (GROUPS, SGU_LEN, 1)], "sgu_bwd", tr=128)
        small["b_s"][l] = db_col[..., 0]

        do = _matmul(dya, full[W_OA], "nt", BF16, "oa_dx", layer=l)
        grads_big[W_OA] = _matmul(sv["o"], dya, "tn", BF16, "oa_dw", into=(grads_big[W_OA], l))
        dqn, dkn, dv = _attention_bwd(sv["qn"], sv["kn"], sv["vb"], do)

        def qk_bwd_fn(dqn, dkn, dv, q, k, gq, gk):
            outs = []
            for d, xin, g in ((dqn, q, gq), (dkn, k, gk)):
                dxs, dgs = [], []
                for h in range(HEADS):
                    sl = slice(h * HEAD_DIM, (h + 1) * HEAD_DIM)
                    _, vjp = jax.vjp(_rms, xin[:, sl], g[:, sl])
                    dxh, dgh = vjp(d[:, sl])
                    dxs.append(dxh)
                    dgs.append(dgh)
                outs.append((jnp.concatenate(dxs, axis=1), jnp.concatenate(dgs, axis=1)))
            return outs[0][0], outs[1][0], dv, outs[0][1], outs[1][1]

        dq, dk, dvb, small["g_q"][l], small["g_k"][l] = _rowwise(
            qk_bwd_fn, [_row(dqn), _row(dkn), _row(dv), _row(sv["proj"], SB_WIDTH, 0), _row(sv["proj"], SB_WIDTH, 1)],
            [sv["gq"], sv["gk"]], [(SB_WIDTH, BF16)] * 3, [(1, SB_WIDTH), (1, SB_WIDTH)], "qk_norm_bwd")

        dproj = jnp.concatenate([dq, dk, dvb, du, dvs, dga, dgb], axis=1)
        dh = _matmul(dproj, full[W_IN], "nt", F32, "in_dx", layer=l)
        grads_big[W_IN] = _matmul(sv["hb"], dproj, "tn", BF16, "in_dw", into=(grads_big[W_IN], l))
        dx, dxb, small["g_mix"][l] = rms_bwd(dh, sv["x"], dx1, g_mix[l:l + 1], "rms_mix_bwd")

    landed = _scatter_grads(grads_big, axes)
    partial = []
    for t in range(len(big)):
        flat = landed[t].reshape(-1, landed[t].shape[-1])
        shard_rows = flat.shape[0] // 4
        width = _pick(flat.shape[1], 1152)
        tr = min(256, shard_rows)
        rows = [_row(flat, width, 0, s * (shard_rows // tr)) for s in range(4)]
        partial.append(_rowwise(lambda a, b, c, d: ((a.astype(F32) + b.astype(F32)) + c.astype(F32)) + d.astype(F32),
                                rows, [], [(width, F32)], [], "sum_chips", tr=tr, ncol=flat.shape[1] // width, n_rows=shard_rows)[0])
    from_sibling = _swap_with_sibling(partial)
    big_out = []
    for t in range(len(big)):
        res = _adamw_sharded(partial[t], from_sibling[t], _flat(big[t]), _flat(big_m[t]), _flat(big_v[t]), "adamw")
        big_out.append([r.reshape(big[t].shape) for r in res])

    names = ["g_mix", "g_q", "g_k", "ln_g", "ln_b", "w_s", "b_s", "g_ff"]
    small_w = [g_mix, g_q, g_k, sgu_ln_g, sgu_ln_b, w_spatial, b_spatial, g_ff]
    small_m = [m_g_mix, m_g_q, m_g_k, m_sgu_ln_g, m_sgu_ln_b, m_w_spatial, m_b_spatial, m_g_ff]
    small_v = [v_g_mix, v_g_q, v_g_k, v_sgu_ln_g, v_sgu_ln_b, v_w_spatial, v_b_spatial, v_g_ff]

    def pack(parts):
        return jnp.concatenate([p.reshape(-1, LANE) for p in parts], axis=0)

    local_small = pack([jnp.stack(small[n]) for n in names])
    g_small = _sum_over_devices(local_small)
    d_small, m_small, v_small = _adamw_small(g_small, pack(small_w), pack(small_m), pack(small_v))

    def unpack(packed):
        outs, row = [], 0
        for w in small_w:
            n = w.size // LANE
            outs.append(packed[row:row + n].reshape(w.shape))
            row += n
        return outs

    small_out = [unpack(p) for p in (g_small, d_small, m_small, v_small)]

    order = [("s", 0), ("b", W_IN), ("s", 1), ("s", 2), ("s", 3), ("s", 4), ("s", 5), ("s", 6),
             ("b", W_OA), ("b", W_OB), ("b", W_OUT), ("s", 7), ("b", W_FF1), ("b", W_FF2)]
    result = [loss, dx.reshape(x.shape)]
    for kind in range(4):
        for which, idx in order:
            result.append(small_out[kind][idx] if which == "s" else big_out[idx][kind])
    return tuple(result)
```

```python
import functools

import jax
import jax.numpy as jnp
from jax import lax
from jax.experimental import pallas as pl
from jax.experimental.pallas import tpu as pltpu

F32 = jnp.float32
BF16 = jnp.bfloat16
MESH = pl.DeviceIdType.MESH

EPS = 1e-6
HEADS = 8
HEAD_DIM = 128
SB_WIDTH = HEADS * HEAD_DIM
GROUPS = 8
GROUP_DIM = 128
SGU_WIDTH = GROUPS * GROUP_DIM
SGU_LEN = 128
CHUNK = 64
ATT_BLOCK = 128
LOG_ZERO = -104.0

ADAM_LR = 0.001
ADAM_B1 = 0.9
ADAM_B2 = 0.999
ADAM_EPS = 1e-08
ADAM_WD = 0.01
ADAM_STEP = 10

N_CHIPS = 4
N_DEV = 8
V7X_VMEM_LIMIT = 48 * 1024 * 1024
LANE = 128


def _params(sem):
    return pltpu.CompilerParams(dimension_semantics=sem, vmem_limit_bytes=V7X_VMEM_LIMIT)


def _pick(dim, pref):
    if dim <= pref:
        return dim
    for t in range(pref - pref % LANE, 0, -LANE):
        if dim % t == 0:
            return t
    raise ValueError(f"no tile for {dim}")


HBM_SPEC = pl.BlockSpec(memory_space=pl.ANY)


def _place():
    return lax.axis_index("x"), lax.axis_index("y"), lax.axis_index("c")


def _other_chips(x, y):
    return [(1 - x, y), (x, 1 - y), (1 - x, 1 - y)]


def _shard_view(ref, axis, index):
    size = ref.shape[axis] // N_CHIPS
    start = pl.multiple_of(index * size, size)
    if axis == 0:
        return ref.at[pl.ds(start, size), :]
    return ref.at[:, pl.ds(start, size)]


def _gather_job(shards, layer, axis):
    return dict(kind="gather", src=shards, layer=layer, axis=axis - 1)


def _scatter_job(grad, landed, layer, axis):
    return dict(kind="scatter", src=grad, buf=landed, layer=layer, axis=axis - 1)


def _jobs_io(jobs):
    ins, outs, alias = [], [], {}
    for job in jobs:
        if job["kind"] == "gather":
            shape = list(job["src"].shape[1:])
            shape[job["axis"]] *= N_CHIPS
            ins.append(job["src"])
            outs.append(jax.ShapeDtypeStruct(tuple(shape), job["src"].dtype))
        else:
            ins += [job["src"], job["buf"]]
            alias[len(ins) - 1] = len(outs)
            outs.append(jax.ShapeDtypeStruct(job["buf"].shape, job["buf"].dtype))
    return ins, outs, alias


def _jobs_sems(jobs):
    n = len(jobs)
    return [pltpu.SemaphoreType.DMA((3 * n,)), pltpu.SemaphoreType.DMA((3 * n,)), pltpu.SemaphoreType.DMA((n,))]


def _jobs_copies(jobs, in_refs, out_refs, sems):
    send_sems, recv_sems, local_sems = sems
    x, y, c = _place()
    me = 2 * x + y
    triples, ip = [], 0
    for n, (job, out) in enumerate(zip(jobs, out_refs)):
        axis, layer = job["axis"], job["layer"]
        src = in_refs[ip]
        ip += 1 if job["kind"] == "gather" else 2
        sends, recvs = [], []
        if job["kind"] == "gather":
            mine = src.at[layer]
            local = pltpu.make_async_copy(mine, _shard_view(out, axis, me), local_sems.at[n])
            for k, (px, py) in enumerate(_other_chips(x, y)):
                def copy(dst_index):
                    return pltpu.make_async_remote_copy(
                        src_ref=mine, dst_ref=_shard_view(out, axis, dst_index),
                        send_sem=send_sems.at[3 * n + k], recv_sem=recv_sems.at[3 * n + k],
                        device_id=(px, py, c), device_id_type=MESH)
                sends.append(copy(me))
                recvs.append(copy(2 * px + py))
        else:
            local = pltpu.make_async_copy(_shard_view(src, axis, me), out.at[3, layer], local_sems.at[n])
            for k, (px, py) in enumerate(_other_chips(x, y)):
                cp = pltpu.make_async_remote_copy(
                    src_ref=_shard_view(src, axis, 2 * px + py), dst_ref=out.at[k, layer],
                    send_sem=send_sems.at[3 * n + k], recv_sem=recv_sems.at[3 * n + k],
                    device_id=(px, py, c), device_id_type=MESH)
                sends.append(cp)
                recvs.append(cp)
        triples.append((local, sends, recvs))
    return triples


def _jobs_start(jobs, in_refs, out_refs, sems):
    for local, sends, _ in _jobs_copies(jobs, in_refs, out_refs, sems):
        local.start()
        for cp in sends:
            cp.start()


def _jobs_wait(jobs, in_refs, out_refs, sems):
    triples = _jobs_copies(jobs, in_refs, out_refs, sems)
    for local, sends, _ in triples:
        local.wait()
        for cp in sends:
            cp.wait_send()
    for _, _, recvs in triples:
        for cp in recvs:
            cp.wait_recv()


def _run_jobs(jobs, name):
    ins, outs, alias = _jobs_io(jobs)

    def body(*refs):
        in_refs, out_refs, sems = refs[:len(ins)], refs[len(ins):len(ins) + len(outs)], refs[len(ins) + len(outs):]
        _jobs_start(jobs, in_refs, out_refs, sems)
        _jobs_wait(jobs, in_refs, out_refs, sems)

    return pl.pallas_call(
        body, name=name, in_specs=[HBM_SPEC] * len(ins), out_specs=[HBM_SPEC] * len(outs), out_shape=outs,
        scratch_shapes=_jobs_sems(jobs), input_output_aliases=alias,
    )(*ins)


_DIMS = {
    "nn": (((1,), (0,)), ((), ())),
    "nt": (((1,), (1,)), ((), ())),
    "tn": (((0,), (0,)), ((), ())),
}


def _matmul(a, b, mode, out_dtype, name, *, extras=(), post=None, jobs=(), tm=1024, tn=1024, tk=2048):
    assert a.dtype == BF16 and b.dtype == BF16
    if mode == "nn":
        (m, k), (k2, n) = a.shape, b.shape
    elif mode == "nt":
        (m, k), (n, k2) = a.shape, b.shape
    else:
        (k, m), (k2, n) = a.shape, b.shape
    assert k == k2
    tm, tn, tk = _pick(m, tm), _pick(n, tn), _pick(k, tk)
    nk = k // tk
    grid = (m // tm, n // tn, nk)
    dims = _DIMS[mode]
    out_dtypes = out_dtype if isinstance(out_dtype, (tuple, list)) else (out_dtype,)
    n_main, n_out = 2 + len(extras), len(out_dtypes)

    a_spec = pl.BlockSpec((tk, tm), lambda i, j, kk: (kk, i)) if mode == "tn" else pl.BlockSpec((tm, tk), lambda i, j, kk: (i, kk))
    b_spec = pl.BlockSpec((tn, tk), lambda i, j, kk: (j, kk)) if mode == "nt" else pl.BlockSpec((tk, tn), lambda i, j, kk: (kk, j))
    tile_spec = pl.BlockSpec((tm, tn), lambda i, j, kk: (i, j))
    job_ins, job_outs, job_alias = _jobs_io(jobs)
    operands = [a, b, *extras, *job_ins]
    in_specs = [a_spec, b_spec] + [tile_spec] * len(extras) + [HBM_SPEC] * len(job_ins)
    out_shape = [jax.ShapeDtypeStruct((m, n), dt) for dt in out_dtypes] + job_outs
    out_specs = [tile_spec] * n_out + [HBM_SPEC] * len(job_outs)
    aliases = {n_main + i: n_out + o for i, o in job_alias.items()}
    scratch = ([pltpu.VMEM((tm, tn), F32)] if nk > 1 else []) + (_jobs_sems(jobs) if jobs else [])

    def body(*refs):
        a_ref, b_ref = refs[0], refs[1]
        extra_refs = refs[2:n_main]
        job_in_refs = refs[n_main:n_main + len(job_ins)]
        outs = refs[n_main + len(job_ins):]
        o_refs, job_out_refs = outs[:n_out], outs[n_out:n_out + len(job_outs)]
        rest = outs[n_out + len(job_outs):]
        acc_ref = rest[0] if nk > 1 else None
        sems = rest[1:] if nk > 1 else rest
        step = (pl.program_id(0) * grid[1] + pl.program_id(1)) * grid[2] + pl.program_id(2)
        if jobs:
            pl.when(step == 0)(lambda: _jobs_start(jobs, job_in_refs, job_out_refs, sems))

        p = lax.dot_general(a_ref[...], b_ref[...], dims, preferred_element_type=F32)

        def finish(total):
            vals = post(total, *[r[...] for r in extra_refs]) if post is not None else total
            vals = vals if isinstance(vals, (tuple, list)) else (vals,)
            for o_ref, v in zip(o_refs, vals, strict=True):
                o_ref[...] = v.astype(o_ref.dtype)

        if nk == 1:
            finish(p)
        else:
            kk = pl.program_id(2)

            @pl.when(kk == 0)
            def _():
                acc_ref[...] = p

            @pl.when(kk > 0)
            def _():
                acc_ref[...] += p

            @pl.when(kk == nk - 1)
            def _():
                finish(acc_ref[...])

        if jobs:
            pl.when(step == grid[0] * grid[1] * grid[2] - 1)(lambda: _jobs_wait(jobs, job_in_refs, job_out_refs, sems))

    res = pl.pallas_call(
        body,
        name=name,
        grid=grid,
        in_specs=in_specs,
        out_specs=out_specs,
        out_shape=out_shape,
        scratch_shapes=scratch,
        input_output_aliases=aliases,
        compiler_params=_params(("arbitrary", "arbitrary", "arbitrary") if jobs else ("parallel", "parallel", "arbitrary")),
    )(*operands)
    return res[0] if len(res) == 1 else res


def _rowwise(fn, rows, consts, outs, reds, name, *, tr=256, ncol=1, n_rows=None):
    if n_rows is None:
        n_rows = rows[0][0].shape[0]
    tr = min(tr, n_rows)
    assert n_rows % tr == 0
    in_specs, operands = [], []
    for arr, width, coloff, rowoff in rows:
        in_specs.append(pl.BlockSpec((tr, width), functools.partial(lambda j, i, c, r: (i + r, j + c), c=coloff, r=rowoff)))
        operands.append(arr)
    for arr in consts:
        if arr.ndim == 3:
            in_specs.append(pl.BlockSpec(arr.shape, lambda j, i: (0, 0, 0)))
        else:
            in_specs.append(pl.BlockSpec(arr.shape, lambda j, i: (0, 0)))
        operands.append(arr)
    out_specs, out_shape = [], []
    for width, dtype in outs:
        out_specs.append(pl.BlockSpec((tr, width), lambda j, i: (i, j)))
        out_shape.append(jax.ShapeDtypeStruct((n_rows, ncol * width), dtype))
    for shape in reds:
        if len(shape) == 3:
            out_specs.append(pl.BlockSpec(shape, lambda j, i: (0, 0, 0)))
            out_shape.append(jax.ShapeDtypeStruct(shape, F32))
        else:
            out_specs.append(pl.BlockSpec(shape, lambda j, i: (0, j)))
            out_shape.append(jax.ShapeDtypeStruct((shape[0], ncol * shape[1]), F32))
    n_in, n_out, n_red = len(operands), len(outs), len(reds)

    def body(*refs):
        vals = fn(*[r[...] for r in refs[:n_in]])
        if not isinstance(vals, (tuple, list)):
            vals = (vals,)
        assert len(vals) == n_out + n_red
        for o_ref, v in zip(refs[n_in:n_in + n_out], vals[:n_out]):
            o_ref[...] = v.astype(o_ref.dtype)
        if n_red:
            first = pl.program_id(1) == 0
            for o_ref, v in zip(refs[n_in + n_out:], vals[n_out:]):
                @pl.when(first)
                def _(o_ref=o_ref, v=v):
                    o_ref[...] = v

                @pl.when(jnp.logical_not(first))
                def _(o_ref=o_ref, v=v):
                    o_ref[...] += v

    res = pl.pallas_call(
        body,
        name=name,
        grid=(ncol, n_rows // tr),
        in_specs=in_specs,
        out_specs=out_specs,
        out_shape=out_shape,
        compiler_params=_params(("parallel", "arbitrary")),
    )(*operands)
    return res


def _row(arr, width=None, coloff=0, rowoff=0):
    return (arr, arr.shape[1] if width is None else width, coloff, rowoff)


def _rms(x, g):
    return x * lax.rsqrt(jnp.mean(x * x, axis=-1, keepdims=True) + EPS) * g


def _erf(x):
    x = jnp.clip(x, -4.0, 4.0)
    x2 = x * x
    alpha = x2 * -2.72614225801306e-10 + 2.77068142495902e-08
    alpha = alpha * x2 - 2.10102402082508e-06
    alpha = alpha * x2 - 5.69250639462346e-05
    alpha = alpha * x2 - 7.34990630326855e-04
    alpha = alpha * x2 - 2.95459980854025e-03
    alpha = alpha * x2 - 1.60960333262415e-02
    beta = x2 * -1.45660718464996e-05 - 2.13374055278905e-04
    beta = beta * x2 - 1.68282697438203e-03
    beta = beta * x2 - 7.37332916720468e-03
    beta = beta * x2 - 1.42647390514189e-02
    return x * alpha / beta


def _gelu(x):
    return 0.5 * x * (1.0 + _erf(x * (2.0 ** -0.5)))


def _gelu_grad(x):
    cdf = 0.5 * (1.0 + _erf(x * (2.0 ** -0.5)))
    pdf = jnp.exp(-0.5 * x * x) * (2.0 * jnp.pi) ** -0.5
    return cdf + x * pdf


def _layer_norm(x, g, b):
    mu = jnp.mean(x, axis=-1, keepdims=True)
    xc = x - mu
    return xc * lax.rsqrt(jnp.mean(xc * xc, axis=-1, keepdims=True) + EPS) * g + b


def _per_head(fn, x, g):
    parts = []
    for h in range(HEADS):
        sl = slice(h * HEAD_DIM, (h + 1) * HEAD_DIM)
        parts.append(fn(x[:, sl], g[:, sl]))
    return parts


def _sgu_mask():
    i = lax.broadcasted_iota(jnp.int32, (SGU_LEN, SGU_LEN), 0)
    j = lax.broadcasted_iota(jnp.int32, (SGU_LEN, SGU_LEN), 1)
    return (j // CHUNK) <= (i // CHUNK)


def _sgu_mix(vln_bf, w_bf, bias):
    rows = vln_bf.shape[0]
    out_rows = []
    for c in range(rows // SGU_LEN):
        cols = []
        for g in range(GROUPS):
            blk = vln_bf[c * SGU_LEN:(c + 1) * SGU_LEN, g * GROUP_DIM:(g + 1) * GROUP_DIM]
            mixed = jnp.dot(w_bf[g], blk, preferred_element_type=F32) + bias[g]
            cols.append(mixed)
        out_rows.append(jnp.concatenate(cols, axis=1))
    return out_rows[0] if len(out_rows) == 1 else jnp.concatenate(out_rows, axis=0)


def _split_dot(x, m_bf):
    hi = x.astype(BF16)
    lo = (x - hi.astype(F32)).astype(BF16)
    return jnp.dot(hi, m_bf, preferred_element_type=F32) + jnp.dot(lo, m_bf, preferred_element_type=F32)


def _att_block(q, kb, run_keep, diag):
    blk = ATT_BLOCK
    z = lax.dot_general(q, kb, _DIMS["nt"], preferred_element_type=F32) * (HEAD_DIM ** -0.5)
    e = jnp.exp(-jnp.abs(z))
    softplus = jnp.maximum(z, 0.0) + jnp.log(1.0 + e)
    row = lax.broadcasted_iota(jnp.int32, (blk, blk), 0)
    col = lax.broadcasted_iota(jnp.int32, (blk, blk), 1)
    later = (row > col).astype(BF16)
    log_keep = -softplus
    if diag:
        past = col < row
        log_keep = jnp.where(past, log_keep, 0.0)
    tail = _split_dot(log_keep, later) + run_keep
    a = jnp.exp((z - softplus) + tail)
    if diag:
        a = jnp.where(past, a, 0.0)
    inv = 1.0 / (1.0 + e)
    sig = jnp.where(z >= 0.0, inv, e * inv)
    return a, sig, jnp.sum(log_keep, axis=1, keepdims=True)


def _any_alive(keeps):
    return jnp.max(functools.reduce(jnp.maximum, keeps)) >= LOG_ZERO


def _attention_fwd(qn, kn, vb, *, heads_per_step=4, jobs=()):
    t = qn.shape[0]
    blk = ATT_BLOCK
    nq = t // blk
    hp = heads_per_step
    groups, width = HEADS // hp, hp * HEAD_DIM
    heads = [slice(h * HEAD_DIM, (h + 1) * HEAD_DIM) for h in range(hp)]
    job_ins, job_outs, job_alias = _jobs_io(jobs)
    n_ji, n_jo = len(job_ins), len(job_outs)

    def body(q_ref, k_ref, v_ref, *rest):
        job_in_refs, o_ref = rest[:n_ji], rest[n_ji]
        job_out_refs, sems = rest[n_ji + 1:n_ji + 1 + n_jo], rest[n_ji + 1 + n_jo:]
        i = pl.program_id(1)
        step = pl.program_id(0) * nq + i
        if jobs:
            pl.when(step == 0)(lambda: _jobs_start(jobs, job_in_refs, job_out_refs, sems))
        qs = [q_ref[:, sl] for sl in heads]

        def tile(j, keeps, accs, diag):
            rows = pl.ds(pl.multiple_of(j * blk, blk), blk)
            new_keeps, new_accs = [], []
            for h, sl in enumerate(heads):
                a, _, keep_sum = _att_block(qs[h], k_ref[rows, sl], keeps[h], diag)
                new_accs.append(accs[h] + jnp.dot(a.astype(BF16), v_ref[rows, sl], preferred_element_type=F32))
                new_keeps.append(keeps[h] + keep_sum)
            return tuple(new_keeps), tuple(new_accs)

        zero = tuple(jnp.zeros((blk, 1), F32) for _ in heads)
        keeps, accs = tile(i, zero, tuple(jnp.zeros((blk, HEAD_DIM), F32) for _ in heads), True)

        def cond(c):
            return jnp.logical_and(c[0] >= 0, _any_alive(c[1]))

        def step_fn(c):
            j, keeps, accs = c
            keeps, accs = tile(j, keeps, accs, False)
            return j - 1, keeps, accs

        _, _, accs = lax.while_loop(cond, step_fn, (i - 1, keeps, accs))
        for h, sl in enumerate(heads):
            o_ref[:, sl] = accs[h].astype(o_ref.dtype)
        if jobs:
            pl.when(step == groups * nq - 1)(lambda: _jobs_wait(jobs, job_in_refs, job_out_refs, sems))

    blk_spec = pl.BlockSpec((blk, width), lambda g, i: (i, g))
    head_spec = pl.BlockSpec((t, width), lambda g, i: (0, g))
    res = pl.pallas_call(
        body,
        name="attention_fwd",
        grid=(groups, nq),
        in_specs=[blk_spec, head_spec, head_spec] + [HBM_SPEC] * n_ji,
        out_specs=[blk_spec] + [HBM_SPEC] * n_jo,
        out_shape=[jax.ShapeDtypeStruct((t, SB_WIDTH), BF16)] + job_outs,
        scratch_shapes=_jobs_sems(jobs) if jobs else [],
        input_output_aliases={3 + i: 1 + o for i, o in job_alias.items()},
        compiler_params=_params(("arbitrary", "arbitrary")),
    )(qn, kn, vb, *job_ins)
    return res[0] if len(res) == 1 else res


def _attention_bwd(qn, kn, vb, do, *, heads_per_step=2, jobs=()):
    t = qn.shape[0]
    blk = ATT_BLOCK
    nq = t // blk
    hp = heads_per_step
    groups, width = HEADS // hp, hp * HEAD_DIM
    heads = [slice(h * HEAD_DIM, (h + 1) * HEAD_DIM) for h in range(hp)]
    job_ins, job_outs, job_alias = _jobs_io(jobs)
    n_ji, n_jo = len(job_ins), len(job_outs)

    def body(q_ref, k_ref, v_ref, do_ref, *rest):
        job_in_refs = rest[:n_ji]
        dq_ref, dk_ref, dv_ref = rest[n_ji:n_ji + 3]
        job_out_refs, sems = rest[n_ji + 3:n_ji + 3 + n_jo], rest[n_ji + 3 + n_jo:]
        i = pl.program_id(1)
        step = pl.program_id(0) * nq + i
        if jobs:
            pl.when(step == 0)(lambda: _jobs_start(jobs, job_in_refs, job_out_refs, sems))

        @pl.when(i == 0)
        def _():
            dk_ref[...] = jnp.zeros_like(dk_ref)
            dv_ref[...] = jnp.zeros_like(dv_ref)

        qs = [q_ref[:, sl] for sl in heads]
        dos = [do_ref[:, sl] for sl in heads]
        row = lax.broadcasted_iota(jnp.int32, (blk, blk), 0)
        col = lax.broadcasted_iota(jnp.int32, (blk, blk), 1)
        from_here = (row >= col).astype(BF16)

        def grad_log_a(h, rows, run_keep, diag):
            a, sig, keep_sum = _att_block(qs[h], k_ref[rows, heads[h]], run_keep, diag)
            dp = lax.dot_general(dos[h], v_ref[rows, heads[h]], _DIMS["nt"], preferred_element_type=F32)
            return a, sig, keep_sum, a * dp

        def cond(c):
            return jnp.logical_and(c[0] >= 0, _any_alive(c[1]))

        def tile_a(j, keeps, totals, diag):
            rows = pl.ds(pl.multiple_of(j * blk, blk), blk)
            new_keeps, new_totals = [], []
            for h in range(hp):
                _, _, keep_sum, dla = grad_log_a(h, rows, keeps[h], diag)
                new_keeps.append(keeps[h] + keep_sum)
                new_totals.append(totals[h] + jnp.sum(dla, axis=1, keepdims=True))
            return tuple(new_keeps), tuple(new_totals)

        zero = tuple(jnp.zeros((blk, 1), F32) for _ in heads)
        keeps, totals = tile_a(i, zero, zero, True)

        def step_a(c):
            j, keeps, totals = c
            keeps, totals = tile_a(j, keeps, totals, False)
            return j - 1, keeps, totals

        _, _, totals = lax.while_loop(cond, step_a, (i - 1, keeps, totals))

        def tile_b(j, keeps, run_dlas, dqs, diag):
            rows = pl.ds(pl.multiple_of(j * blk, blk), blk)
            new_keeps, new_dlas, new_dqs = [], [], []
            for h, sl in enumerate(heads):
                a, sig, keep_sum, dla = grad_log_a(h, rows, keeps[h], diag)
                later_sum = _split_dot(dla, from_here) + run_dlas[h]
                dz = dla * (1.0 - sig) - sig * (totals[h] - later_sum)
                if diag:
                    dz = jnp.where(col < row, dz, 0.0)
                dzb = (dz * (HEAD_DIM ** -0.5)).astype(BF16)
                new_dqs.append(dqs[h] + jnp.dot(dzb, k_ref[rows, sl], preferred_element_type=F32))
                dk_ref[rows, sl] += lax.dot_general(dzb, qs[h], _DIMS["tn"], preferred_element_type=F32)
                dv_ref[rows, sl] += lax.dot_general(a.astype(BF16), dos[h], _DIMS["tn"], preferred_element_type=F32)
                new_keeps.append(keeps[h] + keep_sum)
                new_dlas.append(run_dlas[h] + jnp.sum(dla, axis=1, keepdims=True))
            return tuple(new_keeps), tuple(new_dlas), tuple(new_dqs)

        keeps, run_dlas, dqs = tile_b(i, zero, zero, tuple(jnp.zeros((blk, HEAD_DIM), F32) for _ in heads), True)

        def step_b(c):
            j, keeps, run_dlas, dqs = c
            keeps, run_dlas, dqs = tile_b(j, keeps, run_dlas, dqs, False)
            return j - 1, keeps, run_dlas, dqs

        _, _, _, dqs = lax.while_loop(cond, step_b, (i - 1, keeps, run_dlas, dqs))
        for h, sl in enumerate(heads):
            dq_ref[:, sl] = dqs[h]
        if jobs:
            pl.when(step == groups * nq - 1)(lambda: _jobs_wait(jobs, job_in_refs, job_out_refs, sems))

    blk_spec = pl.BlockSpec((blk, width), lambda g, i: (i, g))
    head_spec = pl.BlockSpec((t, width), lambda g, i: (0, g), pipeline_mode=pl.Buffered(1))
    full = jax.ShapeDtypeStruct((t, SB_WIDTH), F32)
    return pl.pallas_call(
        body,
        name="attention_bwd",
        grid=(groups, nq),
        in_specs=[blk_spec, head_spec, head_spec, blk_spec] + [HBM_SPEC] * n_ji,
        out_specs=[blk_spec, head_spec, head_spec] + [HBM_SPEC] * n_jo,
        out_shape=[full, full, full] + job_outs,
        scratch_shapes=_jobs_sems(jobs) if jobs else [],
        input_output_aliases={4 + i: 3 + o for i, o in job_alias.items()},
        compiler_params=_params(("arbitrary", "arbitrary")),
    )(qn, kn, vb, do, *job_ins)


def _swap_with_sibling(arrs):
    n = len(arrs)

    def body(*refs):
        ins, outs = refs[:n], refs[n:2 * n]
        send_sems, recv_sems = refs[2 * n:]
        x, y, c = _place()
        copies = []
        for t in range(n):
            cp = pltpu.make_async_remote_copy(
                src_ref=ins[t], dst_ref=outs[t], send_sem=send_sems.at[t], recv_sem=recv_sems.at[t],
                device_id=(x, y, 1 - c), device_id_type=MESH)
            cp.start()
            copies.append(cp)
        for cp in copies:
            cp.wait_send()
        for cp in copies:
            cp.wait_recv()

    return pl.pallas_call(
        body,
        name="swap_with_sibling",
        in_specs=[HBM_SPEC] * n,
        out_specs=[HBM_SPEC] * n,
        out_shape=[jax.ShapeDtypeStruct(a.shape, a.dtype) for a in arrs],
        scratch_shapes=[pltpu.SemaphoreType.DMA((n,)), pltpu.SemaphoreType.DMA((n,))],
    )(*arrs)


def _sum_over_devices(v):
    rows, width = v.shape

    def body(v_ref, o_ref, buf, send_sems, recv_sems):
        x, y, c = _place()
        me = 4 * x + 2 * y + c
        buf[me] = v_ref[...]
        sends = []
        for d in range(1, N_DEV):
            px, py, pc = x ^ (d >> 2), y ^ ((d >> 1) & 1), c ^ (d & 1)
            cp = pltpu.make_async_remote_copy(
                src_ref=v_ref, dst_ref=buf.at[me], send_sem=send_sems.at[d - 1], recv_sem=recv_sems.at[d - 1],
                device_id=(px, py, pc), device_id_type=MESH)
            cp.start()
            sends.append((cp, 4 * px + 2 * py + pc))
        for cp, _ in sends:
            cp.wait_send()
        for d, (cp, peer) in enumerate(sends):
            pltpu.make_async_remote_copy(
                src_ref=v_ref, dst_ref=buf.at[peer], send_sem=send_sems.at[d], recv_sem=recv_sems.at[d],
                device_id=(x, y, c), device_id_type=MESH).wait_recv()
        total = buf[0]
        for d in range(1, N_DEV):
            total = total + buf[d]
        o_ref[...] = total

    vmem = pl.BlockSpec(memory_space=pltpu.VMEM)
    return pl.pallas_call(
        body,
        name="sum_over_devices",
        in_specs=[vmem],
        out_specs=vmem,
        out_shape=jax.ShapeDtypeStruct((rows, width), F32),
        scratch_shapes=[pltpu.VMEM((N_DEV, rows, width), F32), pltpu.SemaphoreType.DMA((N_DEV - 1,)), pltpu.SemaphoreType.DMA((N_DEV - 1,))],
        compiler_params=pltpu.CompilerParams(vmem_limit_bytes=V7X_VMEM_LIMIT),
    )(v)


def _adamw_math(w, g, m, v):
    m = ADAM_B1 * m + (1.0 - ADAM_B1) * g
    v = ADAM_B2 * v + (1.0 - ADAM_B2) * (g * g)
    m_hat = m / (1.0 - ADAM_B1 ** ADAM_STEP)
    v_hat = v / (1.0 - ADAM_B2 ** ADAM_STEP)
    delta = -ADAM_LR * (m_hat / (jnp.sqrt(v_hat) + ADAM_EPS) + ADAM_WD * w)
    return delta, m, v


def _adamw_sharded(part_a, part_b, w, m, v, name):
    width = _pick(w.shape[1], 1152)
    ncol = w.shape[1] // width

    def fn(a, b, w, m, v):
        g = a + b
        delta, m, v = _adamw_math(w, g, m, v)
        return g, delta, m, v

    return _rowwise(fn, [_row(t, width) for t in (part_a, part_b, w, m, v)], [], [(width, F32)] * 4, [], name, ncol=ncol)


def _adamw_small(g, w, m, v):
    def fn(g, w, m, v):
        return _adamw_math(w, g, m, v)

    return _rowwise(fn, [_row(t) for t in (g, w, m, v)], [], [(g.shape[1], F32)] * 3, [], "adamw_small", tr=g.shape[0])


def _flat(a):
    return a.reshape(-1, a.shape[-1])


def kernel(x, g_mix, w_in, g_q, g_k, sgu_ln_g, sgu_ln_b, w_spatial, b_spatial, w_oa, w_ob, w_out, g_ff, w_ff1, w_ff2, loss_target, m_g_mix, m_w_in, m_g_q, m_g_k, m_sgu_ln_g, m_sgu_ln_b, m_w_spatial, m_b_spatial, m_w_oa, m_w_ob, m_w_out, m_g_ff, m_w_ff1, m_w_ff2, v_g_mix, v_w_in, v_g_q, v_g_k, v_sgu_ln_g, v_sgu_ln_b, v_w_spatial, v_b_spatial, v_w_oa, v_w_ob, v_w_out, v_g_ff, v_w_ff1, v_w_ff2):
    depth = g_mix.shape[0]
    seq, d_model = x.shape[1], x.shape[2]
    d_ff = w_ff1.shape[2] * N_CHIPS
    in_cols = w_in.shape[2] * N_CHIPS
    col_gate_a = 3 * SB_WIDTH + 2 * SGU_WIDTH
    assert in_cols == col_gate_a + 2 * d_model

    big = [w_in, w_oa, w_ob, w_out, w_ff1, w_ff2]
    big_m = [m_w_in, m_w_oa, m_w_ob, m_w_out, m_w_ff1, m_w_ff2]
    big_v = [v_w_in, v_w_oa, v_w_ob, v_w_out, v_w_ff1, v_w_ff2]
    axes = [2, 2, 2, 1, 2, 1]
    W_IN, W_OA, W_OB, W_OUT, W_FF1, W_FF2 = range(6)
    shards = [w.astype(BF16) for w in big]
    full = [[None] * depth for _ in big]

    def gather(t, l):
        return _gather_job(shards[t], l, axes[t])

    def add_residual(total, res):
        return total + res

    (full[W_IN][0],) = _run_jobs([gather(W_IN, 0)], "gather_first")

    gate_w = _pick(d_model, 1024)
    gate_n = d_model // gate_w
    bias_col = b_spatial[..., None]

    def rms_fwd(xin, g, name):
        return _rowwise(lambda xb, gb: _rms(xb, gb), [_row(xin)], [g], [(d_model, BF16)], [], name)[0]

    saved = []
    cur = x.reshape(seq, d_model)
    for l in range(depth):
        hb = rms_fwd(cur, g_mix[l:l + 1], "rms_mix")
        proj, full[W_FF1][l] = _matmul(hb, full[W_IN][l], "nn", F32, "proj", jobs=[gather(W_FF1, l)])

        def qk_fn(q, k, v, gq, gk):
            qn = jnp.concatenate(_per_head(_rms, q, gq), axis=1)
            kn = jnp.concatenate(_per_head(_rms, k, gk), axis=1)
            return qn, kn, v

        gq, gk = g_q[l].reshape(1, SB_WIDTH), g_k[l].reshape(1, SB_WIDTH)
        qn, kn, vb = _rowwise(qk_fn, [_row(proj, SB_WIDTH, 0), _row(proj, SB_WIDTH, 1), _row(proj, SB_WIDTH, 2)],
                              [gq, gk], [(SB_WIDTH, BF16)] * 3, [], "qk_norm")
        o, full[W_OA][l], full[W_OB][l], full[W_OUT][l] = _attention_fwd(
            qn, kn, vb, jobs=[gather(W_OA, l), gather(W_OB, l), gather(W_OUT, l)])
        ya = _matmul(o, full[W_OA][l], "nn", F32, "proj_a")

        def sgu_fn(u_pre, v_pre, ln_g, ln_b, w_s, b_t):
            w_bf = jnp.where(_sgu_mask()[None], w_s, 0.0).astype(BF16)
            vln = _layer_norm(_gelu(v_pre), ln_g, ln_b)
            return _gelu(u_pre) * _sgu_mix(vln.astype(BF16), w_bf, b_t)

        sgu_consts = [sgu_ln_g[l:l + 1], sgu_ln_b[l:l + 1], w_spatial[l], bias_col[l]]
        s = _rowwise(sgu_fn, [_row(proj, SGU_WIDTH, 3), _row(proj, SGU_WIDTH, 4)], sgu_consts, [(SGU_WIDTH, BF16)], [], "sgu_fwd")[0]
        yb = _matmul(s, full[W_OB][l], "nn", F32, "proj_b")

        def merge_fn(ga, gb, a, b):
            return jax.nn.sigmoid(ga) * a + jax.nn.sigmoid(gb) * b

        gate_rows = [_row(proj, gate_w, col_gate_a // gate_w), _row(proj, gate_w, (col_gate_a + d_model) // gate_w)]
        merged = _rowwise(merge_fn, gate_rows + [_row(ya, gate_w), _row(yb, gate_w)], [], [(gate_w, BF16)], [], "merge", ncol=gate_n)[0]
        x1 = _matmul(merged, full[W_OUT][l], "nn", F32, "proj_out", extras=[cur], post=add_residual)
        h2 = rms_fwd(x1, g_ff[l:l + 1], "rms_ff")
        a1, r, full[W_FF2][l] = _matmul(h2, full[W_FF1][l], "nn", (F32, BF16), "ff1", jobs=[gather(W_FF2, l)],
                                        post=lambda total: (total, jnp.square(jnp.maximum(total, 0.0))))
        if l + 1 < depth:
            x2, full[W_IN][l + 1] = _matmul(r, full[W_FF2][l], "nn", F32, "ff2", extras=[x1], post=add_residual,
                                            jobs=[gather(W_IN, l + 1)])
        else:
            x2 = _matmul(r, full[W_FF2][l], "nn", F32, "ff2", extras=[x1], post=add_residual)
        saved.append(dict(x=cur, hb=hb, proj=proj, qn=qn, kn=kn, vb=vb, o=o, ya=ya, yb=yb, s=s, merged=merged,
                          x1=x1, h2=h2, a1=a1, r=r, gq=gq, gk=gk, sgu_consts=sgu_consts, gate_rows=gate_rows))
        cur = x2

    def loss_fn(y, target):
        err = y - target
        per_row = jnp.sum(err * err, axis=-1, keepdims=True) * (1.0 / d_model)
        part = 0.5 * jnp.sum(per_row, axis=0, keepdims=True)
        dy = err * (1.0 / d_model)
        return dy, dy, jnp.broadcast_to(part, (8, LANE))

    dx, dxb, loss_part = _rowwise(loss_fn, [_row(cur), _row(loss_target.reshape(seq, d_model))], [],
                                  [(d_model, F32), (d_model, BF16)], [(8, LANE)], "loss")
    loss = lax.psum(loss_part[0, 0], ("x", "y", "c"))

    landed = [jnp.zeros((4,) + s.shape, BF16) for s in shards]

    def scatter(t, grad, l):
        return _scatter_job(grad, landed[t], l, axes[t])

    grad_in_above = None
    small = {n: [None] * depth for n in ("g_mix", "g_q", "g_k", "ln_g", "ln_b", "w_s", "b_s", "g_ff")}

    def rms_bwd(dh, xin, dres, g, name):
        def fn(dh, xin, dres, g):
            _, vjp = jax.vjp(_rms, xin, g)
            dxin, dg = vjp(dh)
            total = dres + dxin
            return total, total, dg

        return _rowwise(fn, [_row(dh), _row(xin), _row(dres)], [g], [(d_model, F32), (d_model, BF16)], [(1, d_model)], name)

    for l in reversed(range(depth)):
        sv = saved[l]
        da1 = _matmul(dxb, full[W_FF2][l], "nt", BF16, "ff2_dx", extras=[sv["a1"]],
                      post=lambda total, a: total * (2.0 * jnp.maximum(a, 0.0)))
        grad_ff2 = _matmul(sv["r"], dxb, "tn", BF16, "ff2_dw")
        dh2 = _matmul(da1, full[W_FF1][l], "nt", F32, "ff1_dx")
        grad_ff1 = _matmul(sv["h2"], da1, "tn", BF16, "ff1_dw")
        dx1, dx1b, small["g_ff"][l] = rms_bwd(dh2, sv["x1"], dx, g_ff[l:l + 1], "rms_ff_bwd")

        dmerged = _matmul(dx1b, full[W_OUT][l], "nt", F32, "out_dx")
        grad_out = _matmul(sv["merged"], dx1b, "tn", BF16, "out_dw")

        def merge_bwd_fn(dm, ga, gb, a, b):
            sa, sb = jax.nn.sigmoid(ga), jax.nn.sigmoid(gb)
            return dm * a * sa * (1.0 - sa), dm * b * sb * (1.0 - sb), dm * sa, dm * sb

        dga, dgb, dya, dyb = _rowwise(merge_bwd_fn, [_row(dmerged, gate_w)] + sv["gate_rows"] + [_row(sv["ya"], gate_w), _row(sv["yb"], gate_w)],
                                      [], [(gate_w, BF16)] * 4, [], "merge_bwd", ncol=gate_n)

        ds = _matmul(dyb, full[W_OB][l], "nt", F32, "ob_dx")
        grad_ob = _matmul(sv["s"], dyb, "tn", BF16, "ob_dw")

        def sgu_bwd_fn(ds, u_pre, v_pre, ln_g, ln_b, w_s, b_t):
            mask = _sgu_mask()
            w_bf = jnp.where(mask[None], w_s, 0.0).astype(BF16)
            vln, ln_vjp = jax.vjp(lambda vg, g, b: _layer_norm(vg, g, b), _gelu(v_pre), ln_g, ln_b)
            vln_bf = vln.astype(BF16)
            mixed = _sgu_mix(vln_bf, w_bf, b_t)
            du_pre = ds * mixed * _gelu_grad(u_pre)
            dmixed = ds * _gelu(u_pre)
            dm_bf = dmixed.astype(BF16)
            dw = [jnp.zeros((SGU_LEN, SGU_LEN), F32) for _ in range(GROUPS)]
            db = [jnp.zeros((SGU_LEN, 1), F32) for _ in range(GROUPS)]
            dvln_rows = []
            for c in range(ds.shape[0] // SGU_LEN):
                rows = slice(c * SGU_LEN, (c + 1) * SGU_LEN)
                cols = []
                for g in range(GROUPS):
                    sl = slice(g * GROUP_DIM, (g + 1) * GROUP_DIM)
                    cols.append(lax.dot_general(w_bf[g], dm_bf[rows, sl], _DIMS["tn"], preferred_element_type=F32))
                    dw[g] = dw[g] + lax.dot_general(dm_bf[rows, sl], vln_bf[rows, sl], _DIMS["nt"], preferred_element_type=F32)
                    db[g] = db[g] + jnp.sum(dmixed[rows, sl], axis=1, keepdims=True)
                dvln_rows.append(jnp.concatenate(cols, axis=1))
            dvln = dvln_rows[0] if len(dvln_rows) == 1 else jnp.concatenate(dvln_rows, axis=0)
            dvg, dln_g, dln_b = ln_vjp(dvln)
            dv_pre = dvg * _gelu_grad(v_pre)
            dw_s = jnp.stack([jnp.where(mask, d, 0.0) for d in dw])
            return du_pre, dv_pre, dln_g, dln_b, dw_s, jnp.stack(db)

        du, dvs, small["ln_g"][l], small["ln_b"][l], small["w_s"][l], db_col = _rowwise(
            sgu_bwd_fn, [_row(ds), _row(sv["proj"], SGU_WIDTH, 3), _row(sv["proj"], SGU_WIDTH, 4)], sv["sgu_consts"],
            [(SGU_WIDTH, BF16)] * 2, [(1, SGU_WIDTH), (1, SGU_WIDTH), (GROUPS, SGU_LEN, SGU_LEN), (GROUPS, SGU_LEN, 1)], "sgu_bwd", tr=128)
        small["b_s"][l] = db_col[..., 0]

        do = _matmul(dya, full[W_OA][l], "nt", BF16, "oa_dx")
        grad_oa = _matmul(sv["o"], dya, "tn", BF16, "oa_dw")
        jobs = [scatter(W_FF2, grad_ff2, l)]
        if grad_in_above is not None:
            jobs.append(scatter(W_IN, grad_in_above, l + 1))
        dqn, dkn, dv, landed[W_FF2], *rest = _attention_bwd(sv["qn"], sv["kn"], sv["vb"], do, jobs=jobs)
        if rest:
            landed[W_IN] = rest[0]

        def qk_bwd_fn(dqn, dkn, dv, q, k, gq, gk):
            outs = []
            for d, xin, g in ((dqn, q, gq), (dkn, k, gk)):
                dxs, dgs = [], []
                for h in range(HEADS):
                    sl = slice(h * HEAD_DIM, (h + 1) * HEAD_DIM)
                    _, vjp = jax.vjp(_rms, xin[:, sl], g[:, sl])
                    dxh, dgh = vjp(d[:, sl])
                    dxs.append(dxh)
                    dgs.append(dgh)
                outs.append((jnp.concatenate(dxs, axis=1), jnp.concatenate(dgs, axis=1)))
            return outs[0][0], outs[1][0], dv, outs[0][1], outs[1][1]

        dq, dk, dvb, small["g_q"][l], small["g_k"][l] = _rowwise(
            qk_bwd_fn, [_row(dqn), _row(dkn), _row(dv), _row(sv["proj"], SB_WIDTH, 0), _row(sv["proj"], SB_WIDTH, 1)],
            [sv["gq"], sv["gk"]], [(SB_WIDTH, BF16)] * 3, [(1, SB_WIDTH), (1, SB_WIDTH)], "qk_norm_bwd")

        dproj = jnp.concatenate([dq, dk, dvb, du, dvs, dga, dgb], axis=1)
        dh, landed[W_FF1] = _matmul(dproj, full[W_IN][l], "nt", F32, "in_dx", jobs=[scatter(W_FF1, grad_ff1, l)])
        grad_in_above, landed[W_OUT], landed[W_OB], landed[W_OA] = _matmul(
            sv["hb"], dproj, "tn", BF16, "in_dw",
            jobs=[scatter(W_OUT, grad_out, l), scatter(W_OB, grad_ob, l), scatter(W_OA, grad_oa, l)])
        dx, dxb, small["g_mix"][l] = rms_bwd(dh, sv["x"], dx1, g_mix[l:l + 1], "rms_mix_bwd")

    (landed[W_IN],) = _run_jobs([scatter(W_IN, grad_in_above, 0)], "scatter_last")
    partial = []
    for t in range(len(big)):
        flat = landed[t].reshape(-1, landed[t].shape[-1])
        shard_rows = flat.shape[0] // 4
        width = _pick(flat.shape[1], 1152)
        tr = min(256, shard_rows)
        rows = [_row(flat, width, 0, s * (shard_rows // tr)) for s in range(4)]
        partial.append(_rowwise(lambda a, b, c, d: ((a.astype(F32) + b.astype(F32)) + c.astype(F32)) + d.astype(F32),
                                rows, [], [(width, F32)], [], "sum_chips", tr=tr, ncol=flat.shape[1] // width, n_rows=shard_rows)[0])
    from_sibling = _swap_with_sibling(partial)
    big_out = []
    for t in range(len(big)):
        res = _adamw_sharded(partial[t], from_sibling[t], _flat(big[t]), _flat(big_m[t]), _flat(big_v[t]), "adamw")
        big_out.append([r.reshape(big[t].shape) for r in res])

    names = ["g_mix", "g_q", "g_k", "ln_g", "ln_b", "w_s", "b_s", "g_ff"]
    small_w = [g_mix, g_q, g_k, sgu_ln_g, sgu_ln_b, w_spatial, b_spatial, g_ff]
    small_m = [m_g_mix, m_g_q, m_g_k, m_sgu_ln_g, m_sgu_ln_b, m_w_spatial, m_b_spatial, m_g_ff]
    small_v = [v_g_mix, v_g_q, v_g_k, v_sgu_ln_g, v_sgu_ln_b, v_w_spatial, v_b_spatial, v_g_ff]

    def pack(parts):
        return jnp.concatenate([p.reshape(-1, LANE) for p in parts], axis=0)

    local_small = pack([jnp.stack(small[n]) for n in names])
    g_small = _sum_over_devices(local_small)
    d_small, m_small, v_small = _adamw_small(g_small, pack(small_w), pack(small_m), pack(small_v))

    def unpack(packed):
        outs, row = [], 0
        for w in small_w:
            n = w.size // LANE
            outs.append(packed[row:row + n].reshape(w.shape))
            row += n
        return outs

    small_out = [unpack(p) for p in (g_small, d_small, m_small, v_small)]

    order = [("s", 0), ("b", W_IN), ("s", 1), ("s", 2), ("s", 3), ("s", 4), ("s", 5), ("s", 6),
             ("b", W_OA), ("b", W_OB), ("b", W_OUT), ("s", 7), ("b", W_FF1), ("b", W_FF2)]
    result = [loss, dx.reshape(x.shape)]
    for kind in range(4):
        for which, idx in order:
            result.append(small_out[kind][idx] if which == "s" else big_out[idx][kind])
    return tuple(result)
```

```python
import functools

import jax
import jax.numpy as jnp
from jax import lax
from jax.experimental import pallas as pl
from jax.experimental.pallas import tpu as pltpu

F32 = jnp.float32
BF16 = jnp.bfloat16
MESH = pl.DeviceIdType.MESH

EPS = 1e-6
HEADS = 8
HEAD_DIM = 128
SB_WIDTH = HEADS * HEAD_DIM
GROUPS = 8
GROUP_DIM = 128
SGU_WIDTH = GROUPS * GROUP_DIM
SGU_LEN = 128
CHUNK = 64
ATT_BLOCK = 128
LOG_ZERO = -104.0

ADAM_LR = 0.001
ADAM_B1 = 0.9
ADAM_B2 = 0.999
ADAM_EPS = 1e-08
ADAM_WD = 0.01
ADAM_STEP = 10

N_CHIPS = 4
N_DEV = 8
V7X_VMEM_LIMIT = 48 * 1024 * 1024
LANE = 128


def _params(sem):
    return pltpu.CompilerParams(dimension_semantics=sem, vmem_limit_bytes=V7X_VMEM_LIMIT)


def _pick(dim, pref):
    if dim <= pref:
        return dim
    for t in range(pref - pref % LANE, 0, -LANE):
        if dim % t == 0:
            return t
    raise ValueError(f"no tile for {dim}")


HBM_SPEC = pl.BlockSpec(memory_space=pl.ANY)


def _place():
    return lax.axis_index("x"), lax.axis_index("y"), lax.axis_index("c")


def _other_chips(x, y):
    return [(1 - x, y), (x, 1 - y), (1 - x, 1 - y)]


def _shard_view(ref, axis, index):
    size = ref.shape[axis] // N_CHIPS
    start = pl.multiple_of(index * size, size)
    if axis == 0:
        return ref.at[pl.ds(start, size), :]
    return ref.at[:, pl.ds(start, size)]


def _gather_job(shards, layer, axis):
    return dict(kind="gather", src=shards, layer=layer, axis=axis - 1)


def _scatter_job(grad, landed, layer, axis):
    return dict(kind="scatter", src=grad, buf=landed, layer=layer, axis=axis - 1)


def _jobs_io(jobs):
    ins, outs, alias = [], [], {}
    for job in jobs:
        if job["kind"] == "gather":
            shape = list(job["src"].shape[1:])
            shape[job["axis"]] *= N_CHIPS
            ins.append(job["src"])
            outs.append(jax.ShapeDtypeStruct(tuple(shape), job["src"].dtype))
        else:
            ins += [job["src"], job["buf"]]
            alias[len(ins) - 1] = len(outs)
            outs.append(jax.ShapeDtypeStruct(job["buf"].shape, job["buf"].dtype))
    return ins, outs, alias


def _jobs_sems(jobs):
    n = len(jobs)
    return [pltpu.SemaphoreType.DMA((3 * n,)), pltpu.SemaphoreType.DMA((3 * n,)), pltpu.SemaphoreType.DMA((n,))]


def _jobs_copies(jobs, in_refs, out_refs, sems):
    send_sems, recv_sems, local_sems = sems
    x, y, c = _place()
    me = 2 * x + y
    triples, ip = [], 0
    for n, (job, out) in enumerate(zip(jobs, out_refs)):
        axis, layer = job["axis"], job["layer"]
        src = in_refs[ip]
        ip += 1 if job["kind"] == "gather" else 2
        sends, recvs = [], []
        if job["kind"] == "gather":
            mine = src.at[layer]
            local = pltpu.make_async_copy(mine, _shard_view(out, axis, me), local_sems.at[n])
            for k, (px, py) in enumerate(_other_chips(x, y)):
                def copy(dst_index):
                    return pltpu.make_async_remote_copy(
                        src_ref=mine, dst_ref=_shard_view(out, axis, dst_index),
                        send_sem=send_sems.at[3 * n + k], recv_sem=recv_sems.at[3 * n + k],
                        device_id=(px, py, c), device_id_type=MESH)
                sends.append(copy(me))
                recvs.append(copy(2 * px + py))
        else:
            local = pltpu.make_async_copy(_shard_view(src, axis, me), out.at[3, layer], local_sems.at[n])
            for k, (px, py) in enumerate(_other_chips(x, y)):
                cp = pltpu.make_async_remote_copy(
                    src_ref=_shard_view(src, axis, 2 * px + py), dst_ref=out.at[k, layer],
                    send_sem=send_sems.at[3 * n + k], recv_sem=recv_sems.at[3 * n + k],
                    device_id=(px, py, c), device_id_type=MESH)
                sends.append(cp)
                recvs.append(cp)
        triples.append((local, sends, recvs))
    return triples


def _jobs_start(jobs, in_refs, out_refs, sems):
    for local, sends, _ in _jobs_copies(jobs, in_refs, out_refs, sems):
        local.start()
        for cp in sends:
            cp.start()


def _jobs_wait(jobs, in_refs, out_refs, sems):
    triples = _jobs_copies(jobs, in_refs, out_refs, sems)
    for local, sends, _ in triples:
        local.wait()
        for cp in sends:
            cp.wait_send()
    for _, _, recvs in triples:
        for cp in recvs:
            cp.wait_recv()


def _run_jobs(jobs, name):
    ins, outs, alias = _jobs_io(jobs)

    def body(*refs):
        in_refs, out_refs, sems = refs[:len(ins)], refs[len(ins):len(ins) + len(outs)], refs[len(ins) + len(outs):]
        _jobs_start(jobs, in_refs, out_refs, sems)
        _jobs_wait(jobs, in_refs, out_refs, sems)

    return pl.pallas_call(
        body, name=name, in_specs=[HBM_SPEC] * len(ins), out_specs=[HBM_SPEC] * len(outs), out_shape=outs,
        scratch_shapes=_jobs_sems(jobs), input_output_aliases=alias,
    )(*ins)


_DIMS = {
    "nn": (((1,), (0,)), ((), ())),
    "nt": (((1,), (1,)), ((), ())),
    "tn": (((0,), (0,)), ((), ())),
}


def _matmul(a, b, mode, out_dtype, name, *, extras=(), post=None, jobs=(), tm=1024, tn=1024, tk=2048):
    assert a.dtype == BF16 and b.dtype == BF16
    if mode == "nn":
        (m, k), (k2, n) = a.shape, b.shape
    elif mode == "nt":
        (m, k), (n, k2) = a.shape, b.shape
    else:
        (k, m), (k2, n) = a.shape, b.shape
    assert k == k2
    tm, tn, tk = _pick(m, tm), _pick(n, tn), _pick(k, tk)
    nk = k // tk
    grid = (m // tm, n // tn, nk)
    dims = _DIMS[mode]
    out_dtypes = out_dtype if isinstance(out_dtype, (tuple, list)) else (out_dtype,)
    n_main, n_out = 2 + len(extras), len(out_dtypes)

    a_spec = pl.BlockSpec((tk, tm), lambda i, j, kk: (kk, i)) if mode == "tn" else pl.BlockSpec((tm, tk), lambda i, j, kk: (i, kk))
    b_spec = pl.BlockSpec((tn, tk), lambda i, j, kk: (j, kk)) if mode == "nt" else pl.BlockSpec((tk, tn), lambda i, j, kk: (kk, j))
    tile_spec = pl.BlockSpec((tm, tn), lambda i, j, kk: (i, j))
    job_ins, job_outs, job_alias = _jobs_io(jobs)
    operands = [a, b, *extras, *job_ins]
    in_specs = [a_spec, b_spec] + [tile_spec] * len(extras) + [HBM_SPEC] * len(job_ins)
    out_shape = [jax.ShapeDtypeStruct((m, n), dt) for dt in out_dtypes] + job_outs
    out_specs = [tile_spec] * n_out + [HBM_SPEC] * len(job_outs)
    aliases = {n_main + i: n_out + o for i, o in job_alias.items()}
    scratch = ([pltpu.VMEM((tm, tn), F32)] if nk > 1 else []) + (_jobs_sems(jobs) if jobs else [])

    def body(*refs):
        a_ref, b_ref = refs[0], refs[1]
        extra_refs = refs[2:n_main]
        job_in_refs = refs[n_main:n_main + len(job_ins)]
        outs = refs[n_main + len(job_ins):]
        o_refs, job_out_refs = outs[:n_out], outs[n_out:n_out + len(job_outs)]
        rest = outs[n_out + len(job_outs):]
        acc_ref = rest[0] if nk > 1 else None
        sems = rest[1:] if nk > 1 else rest
        step = (pl.program_id(0) * grid[1] + pl.program_id(1)) * grid[2] + pl.program_id(2)
        if jobs:
            pl.when(step == 0)(lambda: _jobs_start(jobs, job_in_refs, job_out_refs, sems))

        p = lax.dot_general(a_ref[...], b_ref[...], dims, preferred_element_type=F32)

        def finish(total):
            vals = post(total, *[r[...] for r in extra_refs]) if post is not None else total
            vals = vals if isinstance(vals, (tuple, list)) else (vals,)
            for o_ref, v in zip(o_refs, vals, strict=True):
                o_ref[...] = v.astype(o_ref.dtype)

        if nk == 1:
            finish(p)
        else:
            kk = pl.program_id(2)

            @pl.when(kk == 0)
            def _():
                acc_ref[...] = p

            @pl.when(kk > 0)
            def _():
                acc_ref[...] += p

            @pl.when(kk == nk - 1)
            def _():
                finish(acc_ref[...])

        if jobs:
            pl.when(step == grid[0] * grid[1] * grid[2] - 1)(lambda: _jobs_wait(jobs, job_in_refs, job_out_refs, sems))

    res = pl.pallas_call(
        body,
        name=name,
        grid=grid,
        in_specs=in_specs,
        out_specs=out_specs,
        out_shape=out_shape,
        scratch_shapes=scratch,
        input_output_aliases=aliases,
        compiler_params=_params(("arbitrary", "arbitrary", "arbitrary") if jobs else ("parallel", "parallel", "arbitrary")),
    )(*operands)
    return res[0] if len(res) == 1 else res


def _rowwise(fn, rows, consts, outs, reds, name, *, tr=256, ncol=1, n_rows=None):
    if n_rows is None:
        n_rows = rows[0][0].shape[0]
    tr = min(tr, n_rows)
    assert n_rows % tr == 0
    in_specs, operands = [], []
    for arr, width, coloff, rowoff in rows:
        in_specs.append(pl.BlockSpec((tr, width), functools.partial(lambda j, i, c, r: (i + r, j + c), c=coloff, r=rowoff)))
        operands.append(arr)
    for arr in consts:
        if arr.ndim == 3:
            in_specs.append(pl.BlockSpec(arr.shape, lambda j, i: (0, 0, 0)))
        else:
            in_specs.append(pl.BlockSpec(arr.shape, lambda j, i: (0, 0)))
        operands.append(arr)
    out_specs, out_shape = [], []
    for width, dtype in outs:
        out_specs.append(pl.BlockSpec((tr, width), lambda j, i: (i, j)))
        out_shape.append(jax.ShapeDtypeStruct((n_rows, ncol * width), dtype))
    for shape in reds:
        if len(shape) == 3:
            out_specs.append(pl.BlockSpec(shape, lambda j, i: (0, 0, 0)))
            out_shape.append(jax.ShapeDtypeStruct(shape, F32))
        else:
            out_specs.append(pl.BlockSpec(shape, lambda j, i: (0, j)))
            out_shape.append(jax.ShapeDtypeStruct((shape[0], ncol * shape[1]), F32))
    n_in, n_out, n_red = len(operands), len(outs), len(reds)

    def body(*refs):
        vals = fn(*[r[...] for r in refs[:n_in]])
        if not isinstance(vals, (tuple, list)):
            vals = (vals,)
        assert len(vals) == n_out + n_red
        for o_ref, v in zip(refs[n_in:n_in + n_out], vals[:n_out]):
            o_ref[...] = v.astype(o_ref.dtype)
        if n_red:
            first = pl.program_id(1) == 0
            for o_ref, v in zip(refs[n_in + n_out:], vals[n_out:]):
                @pl.when(first)
                def _(o_ref=o_ref, v=v):
                    o_ref[...] = v

                @pl.when(jnp.logical_not(first))
                def _(o_ref=o_ref, v=v):
                    o_ref[...] += v

    res = pl.pallas_call(
        body,
        name=name,
        grid=(ncol, n_rows // tr),
        in_specs=in_specs,
        out_specs=out_specs,
        out_shape=out_shape,
        compiler_params=_params(("parallel", "arbitrary")),
    )(*operands)
    return res


def _row(arr, width=None, coloff=0, rowoff=0):
    return (arr, arr.shape[1] if width is None else width, coloff, rowoff)


def _rms(x, g):
    return x * lax.rsqrt(jnp.mean(x * x, axis=-1, keepdims=True) + EPS) * g


def _erf(x):
    x = jnp.clip(x, -4.0, 4.0)
    x2 = x * x
    alpha = x2 * -2.72614225801306e-10 + 2.77068142495902e-08
    alpha = alpha * x2 - 2.10102402082508e-06
    alpha = alpha * x2 - 5.69250639462346e-05
    alpha = alpha * x2 - 7.34990630326855e-04
    alpha = alpha * x2 - 2.95459980854025e-03
    alpha = alpha * x2 - 1.60960333262415e-02
    beta = x2 * -1.45660718464996e-05 - 2.13374055278905e-04
    beta = beta * x2 - 1.68282697438203e-03
    beta = beta * x2 - 7.37332916720468e-03
    beta = beta * x2 - 1.42647390514189e-02
    return x * alpha / beta


def _gelu(x):
    return 0.5 * x * (1.0 + _erf(x * (2.0 ** -0.5)))


def _gelu_grad(x):
    cdf = 0.5 * (1.0 + _erf(x * (2.0 ** -0.5)))
    pdf = jnp.exp(-0.5 * x * x) * (2.0 * jnp.pi) ** -0.5
    return cdf + x * pdf


def _layer_norm(x, g, b):
    mu = jnp.mean(x, axis=-1, keepdims=True)
    xc = x - mu
    return xc * lax.rsqrt(jnp.mean(xc * xc, axis=-1, keepdims=True) + EPS) * g + b


def _per_head(fn, x, g):
    parts = []
    for h in range(HEADS):
        sl = slice(h * HEAD_DIM, (h + 1) * HEAD_DIM)
        parts.append(fn(x[:, sl], g[:, sl]))
    return parts


def _sgu_mask():
    i = lax.broadcasted_iota(jnp.int32, (SGU_LEN, SGU_LEN), 0)
    j = lax.broadcasted_iota(jnp.int32, (SGU_LEN, SGU_LEN), 1)
    return (j // CHUNK) <= (i // CHUNK)


def _sgu_mix(vln_bf, w_bf, bias):
    rows = vln_bf.shape[0]
    out_rows = []
    for c in range(rows // SGU_LEN):
        cols = []
        for g in range(GROUPS):
            blk = vln_bf[c * SGU_LEN:(c + 1) * SGU_LEN, g * GROUP_DIM:(g + 1) * GROUP_DIM]
            mixed = jnp.dot(w_bf[g], blk, preferred_element_type=F32) + bias[g]
            cols.append(mixed)
        out_rows.append(jnp.concatenate(cols, axis=1))
    return out_rows[0] if len(out_rows) == 1 else jnp.concatenate(out_rows, axis=0)


def _split_dots(xs, m_bf):
    his = [x.astype(BF16) for x in xs]
    los = [(x - hi.astype(F32)).astype(BF16) for x, hi in zip(xs, his)]
    res = jnp.dot(jnp.concatenate(his + los, axis=0), m_bf, preferred_element_type=F32)
    n, rows = len(xs), xs[0].shape[0]
    return [res[h * rows:(h + 1) * rows] + res[(n + h) * rows:(n + h + 1) * rows] for h in range(n)]


def _att_tiles(qs, kbs, run_keeps, diag, want_sig=False):
    blk = ATT_BLOCK
    row = lax.broadcasted_iota(jnp.int32, (blk, blk), 0)
    col = lax.broadcasted_iota(jnp.int32, (blk, blk), 1)
    later = (row > col).astype(BF16)
    past = col < row
    zs = [lax.dot_general(q, kb, _DIMS["nt"], preferred_element_type=F32) * (HEAD_DIM ** -0.5) for q, kb in zip(qs, kbs)]
    es = [jnp.exp(-jnp.abs(z)) for z in zs]
    softplus = [jnp.maximum(z, 0.0) + jnp.log(1.0 + e) for z, e in zip(zs, es)]
    log_keeps = [jnp.where(past, -sp, 0.0) if diag else -sp for sp in softplus]
    tails = _split_dots(log_keeps, later)
    weights = [jnp.exp((z - sp) + (tail + keep)) for z, sp, tail, keep in zip(zs, softplus, tails, run_keeps)]
    if diag:
        weights = [jnp.where(past, a, 0.0) for a in weights]
    sigs = None
    if want_sig:
        invs = [1.0 / (1.0 + e) for e in es]
        sigs = [jnp.where(z >= 0.0, inv, e * inv) for z, e, inv in zip(zs, es, invs)]
    return weights, sigs, [jnp.sum(lk, axis=1, keepdims=True) for lk in log_keeps]


def _any_alive(keeps):
    return jnp.max(functools.reduce(jnp.maximum, keeps)) >= LOG_ZERO


def _attention_fwd(qn, kn, vb, *, heads_per_step=4, jobs=()):
    t = qn.shape[0]
    blk = ATT_BLOCK
    nq = t // blk
    hp = heads_per_step
    groups, width = HEADS // hp, hp * HEAD_DIM
    heads = [slice(h * HEAD_DIM, (h + 1) * HEAD_DIM) for h in range(hp)]
    job_ins, job_outs, job_alias = _jobs_io(jobs)
    n_ji, n_jo = len(job_ins), len(job_outs)

    def body(q_ref, k_ref, v_ref, *rest):
        job_in_refs, o_ref, o32_ref = rest[:n_ji], rest[n_ji], rest[n_ji + 1]
        job_out_refs, sems = rest[n_ji + 2:n_ji + 2 + n_jo], rest[n_ji + 2 + n_jo:]
        i = pl.program_id(1)
        step = pl.program_id(0) * nq + i
        if jobs:
            pl.when(step == 0)(lambda: _jobs_start(jobs, job_in_refs, job_out_refs, sems))
        qs = [q_ref[:, sl] for sl in heads]

        def tile(j, keeps, accs, diag):
            rows = pl.ds(pl.multiple_of(j * blk, blk), blk)
            weights, _, keep_sums = _att_tiles(qs, [k_ref[rows, sl] for sl in heads], keeps, diag)
            accs = [acc + jnp.dot(a.astype(BF16), v_ref[rows, sl], preferred_element_type=F32)
                    for acc, a, sl in zip(accs, weights, heads)]
            return tuple(k + s for k, s in zip(keeps, keep_sums)), tuple(accs)

        zero = tuple(jnp.zeros((blk, 1), F32) for _ in heads)
        keeps, accs = tile(i, zero, tuple(jnp.zeros((blk, HEAD_DIM), F32) for _ in heads), True)

        def cond(c):
            return jnp.logical_and(c[0] >= 0, _any_alive(c[1]))

        def step_fn(c):
            j, keeps, accs = c
            keeps, accs = tile(j, keeps, accs, False)
            return j - 1, keeps, accs

        _, _, accs = lax.while_loop(cond, step_fn, (i - 1, keeps, accs))
        for h, sl in enumerate(heads):
            o_ref[:, sl] = accs[h].astype(o_ref.dtype)
            o32_ref[:, sl] = accs[h]
        if jobs:
            pl.when(step == groups * nq - 1)(lambda: _jobs_wait(jobs, job_in_refs, job_out_refs, sems))

    blk_spec = pl.BlockSpec((blk, width), lambda g, i: (i, g))
    head_spec = pl.BlockSpec((t, width), lambda g, i: (0, g))
    return pl.pallas_call(
        body,
        name="attention_fwd",
        grid=(groups, nq),
        in_specs=[blk_spec, head_spec, head_spec] + [HBM_SPEC] * n_ji,
        out_specs=[blk_spec, blk_spec] + [HBM_SPEC] * n_jo,
        out_shape=[jax.ShapeDtypeStruct((t, SB_WIDTH), BF16), jax.ShapeDtypeStruct((t, SB_WIDTH), F32)] + job_outs,
        scratch_shapes=_jobs_sems(jobs) if jobs else [],
        input_output_aliases={3 + i: 2 + o for i, o in job_alias.items()},
        compiler_params=_params(("arbitrary", "arbitrary")),
    )(qn, kn, vb, *job_ins)


def _attention_bwd(qn, kn, vb, do, o32, *, heads_per_step=2, jobs=()):
    t = qn.shape[0]
    blk = ATT_BLOCK
    nq = t // blk
    hp = heads_per_step
    groups, width = HEADS // hp, hp * HEAD_DIM
    heads = [slice(h * HEAD_DIM, (h + 1) * HEAD_DIM) for h in range(hp)]
    job_ins, job_outs, job_alias = _jobs_io(jobs)
    n_ji, n_jo = len(job_ins), len(job_outs)

    def body(q_ref, k_ref, v_ref, do_ref, o32_ref, *rest):
        job_in_refs = rest[:n_ji]
        dq_ref, dk_ref, dv_ref = rest[n_ji:n_ji + 3]
        job_out_refs, sems = rest[n_ji + 3:n_ji + 3 + n_jo], rest[n_ji + 3 + n_jo:]
        i = pl.program_id(1)
        step = pl.program_id(0) * nq + i
        if jobs:
            pl.when(step == 0)(lambda: _jobs_start(jobs, job_in_refs, job_out_refs, sems))

        @pl.when(i == 0)
        def _():
            dk_ref[...] = jnp.zeros_like(dk_ref)
            dv_ref[...] = jnp.zeros_like(dv_ref)

        qs = [q_ref[:, sl] for sl in heads]
        dos = [do_ref[:, sl] for sl in heads]
        totals = [jnp.sum(d.astype(F32) * o32_ref[:, sl], axis=1, keepdims=True) for d, sl in zip(dos, heads)]
        row = lax.broadcasted_iota(jnp.int32, (blk, blk), 0)
        col = lax.broadcasted_iota(jnp.int32, (blk, blk), 1)
        from_here = (row >= col).astype(BF16)

        def tile(j, keeps, run_dlas, dqs, diag):
            rows = pl.ds(pl.multiple_of(j * blk, blk), blk)
            kbs = [k_ref[rows, sl] for sl in heads]
            weights, sigs, keep_sums = _att_tiles(qs, kbs, keeps, diag, want_sig=True)
            wbs = [a.astype(BF16) for a in weights]
            dps = [lax.dot_general(d, v_ref[rows, sl], _DIMS["nt"], preferred_element_type=F32) for d, sl in zip(dos, heads)]
            dlas = [wb.astype(F32) * dp for wb, dp in zip(wbs, dps)]
            later_sums = _split_dots(dlas, from_here)
            dzbs = []
            for dla, sig, total, later, run in zip(dlas, sigs, totals, later_sums, run_dlas):
                dz = dla * (1.0 - sig) - sig * (total - (later + run))
                if diag:
                    dz = jnp.where(col < row, dz, 0.0)
                dzbs.append((dz * (HEAD_DIM ** -0.5)).astype(BF16))
            dqs = [dq + jnp.dot(dzb, kb, preferred_element_type=F32) for dq, dzb, kb in zip(dqs, dzbs, kbs)]
            for sl, dzb, wb, q, d in zip(heads, dzbs, wbs, qs, dos):
                dk_ref[rows, sl] += lax.dot_general(dzb, q, _DIMS["tn"], preferred_element_type=F32)
                dv_ref[rows, sl] += lax.dot_general(wb, d, _DIMS["tn"], preferred_element_type=F32)
            keeps = tuple(k + s for k, s in zip(keeps, keep_sums))
            run_dlas = tuple(r + jnp.sum(dla, axis=1, keepdims=True) for r, dla in zip(run_dlas, dlas))
            return keeps, run_dlas, tuple(dqs)

        zero = tuple(jnp.zeros((blk, 1), F32) for _ in heads)
        keeps, run_dlas, dqs = tile(i, zero, zero, tuple(jnp.zeros((blk, HEAD_DIM), F32) for _ in heads), True)

        def cond(c):
            return jnp.logical_and(c[0] >= 0, _any_alive(c[1]))

        def step_fn(c):
            j, keeps, run_dlas, dqs = c
            keeps, run_dlas, dqs = tile(j, keeps, run_dlas, dqs, False)
            return j - 1, keeps, run_dlas, dqs

        _, _, _, dqs = lax.while_loop(cond, step_fn, (i - 1, keeps, run_dlas, dqs))
        for h, sl in enumerate(heads):
            dq_ref[:, sl] = dqs[h]
        if jobs:
            pl.when(step == groups * nq - 1)(lambda: _jobs_wait(jobs, job_in_refs, job_out_refs, sems))

    blk_spec = pl.BlockSpec((blk, width), lambda g, i: (i, g))
    head_spec = pl.BlockSpec((t, width), lambda g, i: (0, g), pipeline_mode=pl.Buffered(1))
    full = jax.ShapeDtypeStruct((t, SB_WIDTH), F32)
    return pl.pallas_call(
        body,
        name="attention_bwd",
        grid=(groups, nq),
        in_specs=[blk_spec, head_spec, head_spec, blk_spec, blk_spec] + [HBM_SPEC] * n_ji,
        out_specs=[blk_spec, head_spec, head_spec] + [HBM_SPEC] * n_jo,
        out_shape=[full, full, full] + job_outs,
        scratch_shapes=_jobs_sems(jobs) if jobs else [],
        input_output_aliases={5 + i: 3 + o for i, o in job_alias.items()},
        compiler_params=_params(("arbitrary", "arbitrary")),
    )(qn, kn, vb, do, o32, *job_ins)


def _swap_with_sibling(arrs):
    n = len(arrs)

    def body(*refs):
        ins, outs = refs[:n], refs[n:2 * n]
        send_sems, recv_sems = refs[2 * n:]
        x, y, c = _place()
        copies = []
        for t in range(n):
            cp = pltpu.make_async_remote_copy(
                src_ref=ins[t], dst_ref=outs[t], send_sem=send_sems.at[t], recv_sem=recv_sems.at[t],
                device_id=(x, y, 1 - c), device_id_type=MESH)
            cp.start()
            copies.append(cp)
        for cp in copies:
            cp.wait_send()
        for cp in copies:
            cp.wait_recv()

    return pl.pallas_call(
        body,
        name="swap_with_sibling",
        in_specs=[HBM_SPEC] * n,
        out_specs=[HBM_SPEC] * n,
        out_shape=[jax.ShapeDtypeStruct(a.shape, a.dtype) for a in arrs],
        scratch_shapes=[pltpu.SemaphoreType.DMA((n,)), pltpu.SemaphoreType.DMA((n,))],
    )(*arrs)


def _sum_over_devices(v):
    rows, width = v.shape

    def body(v_ref, o_ref, buf, send_sems, recv_sems):
        x, y, c = _place()
        me = 4 * x + 2 * y + c
        buf[me] = v_ref[...]
        sends = []
        for d in range(1, N_DEV):
            px, py, pc = x ^ (d >> 2), y ^ ((d >> 1) & 1), c ^ (d & 1)
            cp = pltpu.make_async_remote_copy(
                src_ref=v_ref, dst_ref=buf.at[me], send_sem=send_sems.at[d - 1], recv_sem=recv_sems.at[d - 1],
                device_id=(px, py, pc), device_id_type=MESH)
            cp.start()
            sends.append((cp, 4 * px + 2 * py + pc))
        for cp, _ in sends:
            cp.wait_send()
        for d, (cp, peer) in enumerate(sends):
            pltpu.make_async_remote_copy(
                src_ref=v_ref, dst_ref=buf.at[peer], send_sem=send_sems.at[d], recv_sem=recv_sems.at[d],
                device_id=(x, y, c), device_id_type=MESH).wait_recv()
        total = buf[0]
        for d in range(1, N_DEV):
            total = total + buf[d]
        o_ref[...] = total

    vmem = pl.BlockSpec(memory_space=pltpu.VMEM)
    return pl.pallas_call(
        body,
        name="sum_over_devices",
        in_specs=[vmem],
        out_specs=vmem,
        out_shape=jax.ShapeDtypeStruct((rows, width), F32),
        scratch_shapes=[pltpu.VMEM((N_DEV, rows, width), F32), pltpu.SemaphoreType.DMA((N_DEV - 1,)), pltpu.SemaphoreType.DMA((N_DEV - 1,))],
        compiler_params=pltpu.CompilerParams(vmem_limit_bytes=V7X_VMEM_LIMIT),
    )(v)


def _adamw_math(w, g, m, v):
    m = ADAM_B1 * m + (1.0 - ADAM_B1) * g
    v = ADAM_B2 * v + (1.0 - ADAM_B2) * (g * g)
    m_hat = m / (1.0 - ADAM_B1 ** ADAM_STEP)
    v_hat = v / (1.0 - ADAM_B2 ** ADAM_STEP)
    delta = -ADAM_LR * (m_hat / (jnp.sqrt(v_hat) + ADAM_EPS) + ADAM_WD * w)
    return delta, m, v


def _adamw_sharded(part_a, part_b, w, m, v, name):
    width = _pick(w.shape[1], 1152)
    ncol = w.shape[1] // width

    def fn(a, b, w, m, v):
        g = a + b
        delta, m, v = _adamw_math(w, g, m, v)
        return g, delta, m, v

    return _rowwise(fn, [_row(t, width) for t in (part_a, part_b, w, m, v)], [], [(width, F32)] * 4, [], name, ncol=ncol)


def _adamw_small(g, w, m, v):
    def fn(g, w, m, v):
        return _adamw_math(w, g, m, v)

    return _rowwise(fn, [_row(t) for t in (g, w, m, v)], [], [(g.shape[1], F32)] * 3, [], "adamw_small", tr=g.shape[0])


def _flat(a):
    return a.reshape(-1, a.shape[-1])


def kernel(x, g_mix, w_in, g_q, g_k, sgu_ln_g, sgu_ln_b, w_spatial, b_spatial, w_oa, w_ob, w_out, g_ff, w_ff1, w_ff2, loss_target, m_g_mix, m_w_in, m_g_q, m_g_k, m_sgu_ln_g, m_sgu_ln_b, m_w_spatial, m_b_spatial, m_w_oa, m_w_ob, m_w_out, m_g_ff, m_w_ff1, m_w_ff2, v_g_mix, v_w_in, v_g_q, v_g_k, v_sgu_ln_g, v_sgu_ln_b, v_w_spatial, v_b_spatial, v_w_oa, v_w_ob, v_w_out, v_g_ff, v_w_ff1, v_w_ff2):
    depth = g_mix.shape[0]
    seq, d_model = x.shape[1], x.shape[2]
    d_ff = w_ff1.shape[2] * N_CHIPS
    in_cols = w_in.shape[2] * N_CHIPS
    col_gate_a = 3 * SB_WIDTH + 2 * SGU_WIDTH
    assert in_cols == col_gate_a + 2 * d_model

    big = [w_in, w_oa, w_ob, w_out, w_ff1, w_ff2]
    big_m = [m_w_in, m_w_oa, m_w_ob, m_w_out, m_w_ff1, m_w_ff2]
    big_v = [v_w_in, v_w_oa, v_w_ob, v_w_out, v_w_ff1, v_w_ff2]
    axes = [2, 2, 2, 1, 2, 1]
    W_IN, W_OA, W_OB, W_OUT, W_FF1, W_FF2 = range(6)
    shards = [w.astype(BF16) for w in big]
    full = [[None] * depth for _ in big]

    def gather(t, l):
        return _gather_job(shards[t], l, axes[t])

    def add_residual(total, res):
        return total + res

    (full[W_IN][0],) = _run_jobs([gather(W_IN, 0)], "gather_first")

    gate_w = _pick(d_model, 1024)
    gate_n = d_model // gate_w
    bias_col = b_spatial[..., None]

    def rms_fwd(xin, g, name):
        return _rowwise(lambda xb, gb: _rms(xb, gb), [_row(xin)], [g], [(d_model, BF16)], [], name)[0]

    saved = []
    cur = x.reshape(seq, d_model)
    for l in range(depth):
        hb = rms_fwd(cur, g_mix[l:l + 1], "rms_mix")
        proj, full[W_FF1][l] = _matmul(hb, full[W_IN][l], "nn", F32, "proj", jobs=[gather(W_FF1, l)])

        def qk_fn(q, k, v, gq, gk):
            qn = jnp.concatenate(_per_head(_rms, q, gq), axis=1)
            kn = jnp.concatenate(_per_head(_rms, k, gk), axis=1)
            return qn, kn, v

        gq, gk = g_q[l].reshape(1, SB_WIDTH), g_k[l].reshape(1, SB_WIDTH)
        qn, kn, vb = _rowwise(qk_fn, [_row(proj, SB_WIDTH, 0), _row(proj, SB_WIDTH, 1), _row(proj, SB_WIDTH, 2)],
                              [gq, gk], [(SB_WIDTH, BF16)] * 3, [], "qk_norm")
        o, o32, full[W_OA][l], full[W_OB][l], full[W_OUT][l] = _attention_fwd(
            qn, kn, vb, jobs=[gather(W_OA, l), gather(W_OB, l), gather(W_OUT, l)])
        ya = _matmul(o, full[W_OA][l], "nn", F32, "proj_a")

        def sgu_fn(u_pre, v_pre, ln_g, ln_b, w_s, b_t):
            w_bf = jnp.where(_sgu_mask()[None], w_s, 0.0).astype(BF16)
            vln = _layer_norm(_gelu(v_pre), ln_g, ln_b)
            return _gelu(u_pre) * _sgu_mix(vln.astype(BF16), w_bf, b_t)

        sgu_consts = [sgu_ln_g[l:l + 1], sgu_ln_b[l:l + 1], w_spatial[l], bias_col[l]]
        s = _rowwise(sgu_fn, [_row(proj, SGU_WIDTH, 3), _row(proj, SGU_WIDTH, 4)], sgu_consts, [(SGU_WIDTH, BF16)], [], "sgu_fwd")[0]
        yb = _matmul(s, full[W_OB][l], "nn", F32, "proj_b")

        def merge_fn(ga, gb, a, b):
            return jax.nn.sigmoid(ga) * a + jax.nn.sigmoid(gb) * b

        gate_rows = [_row(proj, gate_w, col_gate_a // gate_w), _row(proj, gate_w, (col_gate_a + d_model) // gate_w)]
        merged = _rowwise(merge_fn, gate_rows + [_row(ya, gate_w), _row(yb, gate_w)], [], [(gate_w, BF16)], [], "merge", ncol=gate_n)[0]
        x1 = _matmul(merged, full[W_OUT][l], "nn", F32, "proj_out", extras=[cur], post=add_residual)
        h2 = rms_fwd(x1, g_ff[l:l + 1], "rms_ff")
        a1, r, full[W_FF2][l] = _matmul(h2, full[W_FF1][l], "nn", (F32, BF16), "ff1", jobs=[gather(W_FF2, l)],
                                        post=lambda total: (total, jnp.square(jnp.maximum(total, 0.0))))
        if l + 1 < depth:
            x2, full[W_IN][l + 1] = _matmul(r, full[W_FF2][l], "nn", F32, "ff2", extras=[x1], post=add_residual,
                                            jobs=[gather(W_IN, l + 1)])
        else:
            x2 = _matmul(r, full[W_FF2][l], "nn", F32, "ff2", extras=[x1], post=add_residual)
        saved.append(dict(x=cur, hb=hb, proj=proj, qn=qn, kn=kn, vb=vb, o=o, o32=o32, ya=ya, yb=yb, s=s, merged=merged,
                          x1=x1, h2=h2, a1=a1, r=r, gq=gq, gk=gk, sgu_consts=sgu_consts, gate_rows=gate_rows))
        cur = x2

    def loss_fn(y, target):
        err = y - target
        per_row = jnp.sum(err * err, axis=-1, keepdims=True) * (1.0 / d_model)
        part = 0.5 * jnp.sum(per_row, axis=0, keepdims=True)
        dy = err * (1.0 / d_model)
        return dy, dy, jnp.broadcast_to(part, (8, LANE))

    dx, dxb, loss_part = _rowwise(loss_fn, [_row(cur), _row(loss_target.reshape(seq, d_model))], [],
                                  [(d_model, F32), (d_model, BF16)], [(8, LANE)], "loss")
    loss = lax.psum(loss_part[0, 0], ("x", "y", "c"))

    landed = [jnp.zeros((4,) + s.shape, BF16) for s in shards]

    def scatter(t, grad, l):
        return _scatter_job(grad, landed[t], l, axes[t])

    small = {n: [None] * depth for n in ("g_mix", "g_q", "g_k", "ln_g", "ln_b", "w_s", "b_s", "g_ff")}

    def rms_bwd(dh, xin, dres, g, name):
        def fn(dh, xin, dres, g):
            _, vjp = jax.vjp(_rms, xin, g)
            dxin, dg = vjp(dh)
            total = dres + dxin
            return total, total, dg

        return _rowwise(fn, [_row(dh), _row(xin), _row(dres)], [g], [(d_model, F32), (d_model, BF16)], [(1, d_model)], name)

    for l in reversed(range(depth)):
        sv = saved[l]
        da1 = _matmul(dxb, full[W_FF2][l], "nt", BF16, "ff2_dx", extras=[sv["a1"]],
                      post=lambda total, a: total * (2.0 * jnp.maximum(a, 0.0)))
        grad_ff2 = _matmul(sv["r"], dxb, "tn", BF16, "ff2_dw")
        dh2 = _matmul(da1, full[W_FF1][l], "nt", F32, "ff1_dx")
        grad_ff1 = _matmul(sv["h2"], da1, "tn", BF16, "ff1_dw")
        dx1, dx1b, small["g_ff"][l] = rms_bwd(dh2, sv["x1"], dx, g_ff[l:l + 1], "rms_ff_bwd")

        dmerged = _matmul(dx1b, full[W_OUT][l], "nt", F32, "out_dx")
        grad_out = _matmul(sv["merged"], dx1b, "tn", BF16, "out_dw")

        def merge_bwd_fn(dm, ga, gb, a, b):
            sa, sb = jax.nn.sigmoid(ga), jax.nn.sigmoid(gb)
            return dm * a * sa * (1.0 - sa), dm * b * sb * (1.0 - sb), dm * sa, dm * sb

        dga, dgb, dya, dyb = _rowwise(merge_bwd_fn, [_row(dmerged, gate_w)] + sv["gate_rows"] + [_row(sv["ya"], gate_w), _row(sv["yb"], gate_w)],
                                      [], [(gate_w, BF16)] * 4, [], "merge_bwd", ncol=gate_n)

        ds = _matmul(dyb, full[W_OB][l], "nt", F32, "ob_dx")
        grad_ob = _matmul(sv["s"], dyb, "tn", BF16, "ob_dw")

        def sgu_bwd_fn(ds, u_pre, v_pre, ln_g, ln_b, w_s, b_t):
            mask = _sgu_mask()
            w_bf = jnp.where(mask[None], w_s, 0.0).astype(BF16)
            vln, ln_vjp = jax.vjp(lambda vg, g, b: _layer_norm(vg, g, b), _gelu(v_pre), ln_g, ln_b)
            vln_bf = vln.astype(BF16)
            mixed = _sgu_mix(vln_bf, w_bf, b_t)
            du_pre = ds * mixed * _gelu_grad(u_pre)
            dmixed = ds * _gelu(u_pre)
            dm_bf = dmixed.astype(BF16)
            dw = [jnp.zeros((SGU_LEN, SGU_LEN), F32) for _ in range(GROUPS)]
            db = [jnp.zeros((SGU_LEN, 1), F32) for _ in range(GROUPS)]
            dvln_rows = []
            for c in range(ds.shape[0] // SGU_LEN):
                rows = slice(c * SGU_LEN, (c + 1) * SGU_LEN)
                cols = []
                for g in range(GROUPS):
                    sl = slice(g * GROUP_DIM, (g + 1) * GROUP_DIM)
                    cols.append(lax.dot_general(w_bf[g], dm_bf[rows, sl], _DIMS["tn"], preferred_element_type=F32))
                    dw[g] = dw[g] + lax.dot_general(dm_bf[rows, sl], vln_bf[rows, sl], _DIMS["nt"], preferred_element_type=F32)
                    db[g] = db[g] + jnp.sum(dmixed[rows, sl], axis=1, keepdims=True)
                dvln_rows.append(jnp.concatenate(cols, axis=1))
            dvln = dvln_rows[0] if len(dvln_rows) == 1 else jnp.concatenate(dvln_rows, axis=0)
            dvg, dln_g, dln_b = ln_vjp(dvln)
            dv_pre = dvg * _gelu_grad(v_pre)
            dw_s = jnp.stack([jnp.where(mask, d, 0.0) for d in dw])
            return du_pre, dv_pre, dln_g, dln_b, dw_s, jnp.stack(db)

        du, dvs, small["ln_g"][l], small["ln_b"][l], small["w_s"][l], db_col = _rowwise(
            sgu_bwd_fn, [_row(ds), _row(sv["proj"], SGU_WIDTH, 3), _row(sv["proj"], SGU_WIDTH, 4)], sv["sgu_consts"],
            [(SGU_WIDTH, BF16)] * 2, [(1, SGU_WIDTH), (1, SGU_WIDTH), (GROUPS, SGU_LEN, SGU_LEN), (GROUPS, SGU_LEN, 1)], "sgu_bwd", tr=128)
        small["b_s"][l] = db_col[..., 0]

        do = _matmul(dya, full[W_OA][l], "nt", BF16, "oa_dx")
        grad_oa = _matmul(sv["o"], dya, "tn", BF16, "oa_dw")
        dqn, dkn, dv, landed[W_FF2], landed[W_OUT], landed[W_OB], landed[W_OA] = _attention_bwd(
            sv["qn"], sv["kn"], sv["vb"], do, sv["o32"],
            jobs=[scatter(W_FF2, grad_ff2, l), scatter(W_OUT, grad_out, l), scatter(W_OB, grad_ob, l), scatter(W_OA, grad_oa, l)])

        def qk_bwd_fn(dqn, dkn, dv, q, k, gq, gk):
            outs = []
            for d, xin, g in ((dqn, q, gq), (dkn, k, gk)):
                dxs, dgs = [], []
                for h in range(HEADS):
                    sl = slice(h * HEAD_DIM, (h + 1) * HEAD_DIM)
                    _, vjp = jax.vjp(_rms, xin[:, sl], g[:, sl])
                    dxh, dgh = vjp(d[:, sl])
                    dxs.append(dxh)
                    dgs.append(dgh)
                outs.append((jnp.concatenate(dxs, axis=1), jnp.concatenate(dgs, axis=1)))
            return outs[0][0], outs[1][0], dv, outs[0][1], outs[1][1]

        dq, dk, dvb, small["g_q"][l], small["g_k"][l] = _rowwise(
            qk_bwd_fn, [_row(dqn), _row(dkn), _row(dv), _row(sv["proj"], SB_WIDTH, 0), _row(sv["proj"], SB_WIDTH, 1)],
            [sv["gq"], sv["gk"]], [(SB_WIDTH, BF16)] * 3, [(1, SB_WIDTH), (1, SB_WIDTH)], "qk_norm_bwd")

        dproj = jnp.concatenate([dq, dk, dvb, du, dvs, dga, dgb], axis=1)
        grad_in, landed[W_FF1] = _matmul(sv["hb"], dproj, "tn", BF16, "in_dw", jobs=[scatter(W_FF1, grad_ff1, l)])
        dh, landed[W_IN] = _matmul(dproj, full[W_IN][l], "nt", F32, "in_dx", jobs=[scatter(W_IN, grad_in, l)])
        dx, dxb, small["g_mix"][l] = rms_bwd(dh, sv["x"], dx1, g_mix[l:l + 1], "rms_mix_bwd")

    partial = []
    for t in range(len(big)):
        flat = landed[t].reshape(-1, landed[t].shape[-1])
        shard_rows = flat.shape[0] // 4
        width = _pick(flat.shape[1], 1152)
        tr = min(256, shard_rows)
        rows = [_row(flat, width, 0, s * (shard_rows // tr)) for s in range(4)]
        partial.append(_rowwise(lambda a, b, c, d: ((a.astype(F32) + b.astype(F32)) + c.astype(F32)) + d.astype(F32),
                                rows, [], [(width, F32)], [], "sum_chips", tr=tr, ncol=flat.shape[1] // width, n_rows=shard_rows)[0])
    from_sibling = _swap_with_sibling(partial)
    big_out = []
    for t in range(len(big)):
        res = _adamw_sharded(partial[t], from_sibling[t], _flat(big[t]), _flat(big_m[t]), _flat(big_v[t]), "adamw")
        big_out.append([r.reshape(big[t].shape) for r in res])

    names = ["g_mix", "g_q", "g_k", "ln_g", "ln_b", "w_s", "b_s", "g_ff"]
    small_w = [g_mix, g_q, g_k, sgu_ln_g, sgu_ln_b, w_spatial, b_spatial, g_ff]
    small_m = [m_g_mix, m_g_q, m_g_k, m_sgu_ln_g, m_sgu_ln_b, m_w_spatial, m_b_spatial, m_g_ff]
    small_v = [v_g_mix, v_g_q, v_g_k, v_sgu_ln_g, v_sgu_ln_b, v_w_spatial, v_b_spatial, v_g_ff]

    def pack(parts):
        return jnp.concatenate([p.reshape(-1, LANE) for p in parts], axis=0)

    local_small = pack([jnp.stack(small[n]) for n in names])
    g_small = _sum_over_devices(local_small)
    d_small, m_small, v_small = _adamw_small(g_small, pack(small_w), pack(small_m), pack(small_v))

    def unpack(packed):
        outs, row = [], 0
        for w in small_w:
            n = w.size // LANE
            outs.append(packed[row:row + n].reshape(w.shape))
            row += n
        return outs

    small_out = [unpack(p) for p in (g_small, d_small, m_small, v_small)]

    order = [("s", 0), ("b", W_IN), ("s", 1), ("s", 2), ("s", 3), ("s", 4), ("s", 5), ("s", 6),
             ("b", W_OA), ("b", W_OB), ("b", W_OUT), ("s", 7), ("b", W_FF1), ("b", W_FF2)]
    result = [loss, dx.reshape(x.shape)]
    for kind in range(4):
        for which, idx in order:
            result.append(small_out[kind][idx] if which == "s" else big_out[idx][kind])
    return tuple(result)
```

```python
import functools

import jax
import jax.numpy as jnp
from jax import lax
from jax.experimental import pallas as pl
from jax.experimental.pallas import tpu as pltpu

F32 = jnp.float32
BF16 = jnp.bfloat16
MESH = pl.DeviceIdType.MESH

EPS = 1e-6
HEADS = 8
HEAD_DIM = 128
SB_WIDTH = HEADS * HEAD_DIM
GROUPS = 8
GROUP_DIM = 128
SGU_WIDTH = GROUPS * GROUP_DIM
SGU_LEN = 128
CHUNK = 64
ATT_BLOCK = 128
LOG_ZERO = -104.0

ADAM_LR = 0.001
ADAM_B1 = 0.9
ADAM_B2 = 0.999
ADAM_EPS = 1e-08
ADAM_WD = 0.01
ADAM_STEP = 10

N_CHIPS = 4
N_DEV = 8
V7X_VMEM_LIMIT = 48 * 1024 * 1024
WGRAD_TK = 2048
LANE = 128


def _params(sem):
    return pltpu.CompilerParams(dimension_semantics=sem, vmem_limit_bytes=V7X_VMEM_LIMIT)


def _pick(dim, pref):
    if dim <= pref:
        return dim
    for t in range(pref - pref % LANE, 0, -LANE):
        if dim % t == 0:
            return t
    raise ValueError(f"no tile for {dim}")


HBM_SPEC = pl.BlockSpec(memory_space=pl.ANY)


def _place():
    return lax.axis_index("x"), lax.axis_index("y"), lax.axis_index("c")


def _other_chips(x, y):
    return [(1 - x, y), (x, 1 - y), (1 - x, 1 - y)]


def _shard_view(ref, axis, index):
    size = ref.shape[axis] // N_CHIPS
    start = pl.multiple_of(index * size, size)
    if axis == 0:
        return ref.at[pl.ds(start, size), :]
    return ref.at[:, pl.ds(start, size)]


def _half_view(view, half):
    rows = view.shape[0] // 2
    return view.at[pl.ds(pl.multiple_of(half * rows, 8), rows), :]


def _gather_job(shards, layer, axis):
    return dict(kind="gather", src=shards, layer=layer, axis=axis - 1)


def _forward_job(weight, axis):
    return dict(kind="forward", src=weight, axis=axis - 1)


def _scatter_job(grad, landed, shape, layer, axis):
    return dict(kind="scatter", src=grad, buf=landed, shape=shape, layer=layer, axis=axis - 1)


def _swap_job(arr):
    return dict(kind="swap", src=arr)


def _jobs_io(jobs):
    ins, outs, alias = [], [], {}
    for job in jobs:
        src = job["src"]
        ins.append(src)
        if job["kind"] == "gather":
            shape = list(src.shape[1:])
            shape[job["axis"]] *= N_CHIPS
            outs.append(jax.ShapeDtypeStruct(tuple(shape), src.dtype))
        elif job["kind"] == "forward":
            alias[len(ins) - 1] = len(outs)
            outs.append(jax.ShapeDtypeStruct(src.shape, src.dtype))
        elif job["kind"] == "scatter":
            if job["buf"] is not None:
                ins.append(job["buf"])
                alias[len(ins) - 1] = len(outs)
            outs.append(jax.ShapeDtypeStruct(job["shape"], src.dtype))
        else:
            outs.append(jax.ShapeDtypeStruct(src.shape, src.dtype))
    return ins, outs, alias


def _jobs_sems(jobs):
    n = len(jobs)
    return [pltpu.SemaphoreType.DMA((3 * n,)), pltpu.SemaphoreType.DMA((3 * n,)), pltpu.SemaphoreType.DMA((n,))]


def _jobs_copies(jobs, in_refs, out_refs, sems):
    send_sems, recv_sems, local_sems = sems
    x, y, c = _place()
    me = 2 * x + y
    sibling = (x, y, 1 - c)
    triples, ip = [], 0
    for n, (job, out) in enumerate(zip(jobs, out_refs)):
        kind = job["kind"]
        src = in_refs[ip]
        ip += 2 if (kind == "scatter" and job["buf"] is not None) else 1

        def remote(k, src_ref, dst_ref, to):
            return pltpu.make_async_remote_copy(
                src_ref=src_ref, dst_ref=dst_ref, send_sem=send_sems.at[3 * n + k], recv_sem=recv_sems.at[3 * n + k],
                device_id=to, device_id_type=MESH)

        local, sends, recvs = [], [], []
        if kind == "gather":
            axis = job["axis"]
            mine = src.at[job["layer"]]
            local.append(pltpu.make_async_copy(mine, _shard_view(out, axis, me), local_sems.at[n]))
            for k, (px, py) in enumerate(_other_chips(x, y)):
                sends.append(remote(k, _half_view(mine, c), _half_view(_shard_view(out, axis, me), c), (px, py, c)))
                recvs.append(remote(k, _half_view(mine, c), _half_view(_shard_view(out, axis, 2 * px + py), c), (px, py, c)))
        elif kind == "forward":
            axis = job["axis"]
            for k, (px, py) in enumerate(_other_chips(x, y)):
                got = _shard_view(src, axis, 2 * px + py)
                lands = _shard_view(out, axis, 2 * px + py)
                sends.append(remote(k, _half_view(got, c), _half_view(lands, c), sibling))
                recvs.append(remote(k, _half_view(got, c), _half_view(lands, 1 - c), sibling))
        elif kind == "scatter":
            axis, layer = job["axis"], job["layer"]
            local.append(pltpu.make_async_copy(_shard_view(src, axis, me), out.at[3, layer], local_sems.at[n]))
            for k, (px, py) in enumerate(_other_chips(x, y)):
                cp = remote(k, _shard_view(src, axis, 2 * px + py), out.at[k, layer], (px, py, c))
                sends.append(cp)
                recvs.append(cp)
        else:
            cp = remote(0, src, out, sibling)
            sends.append(cp)
            recvs.append(cp)
        triples.append((local, sends, recvs))
    return triples


def _jobs_start(jobs, in_refs, out_refs, sems):
    for local, sends, _ in _jobs_copies(jobs, in_refs, out_refs, sems):
        for cp in local + sends:
            cp.start()


def _jobs_wait(jobs, in_refs, out_refs, sems):
    triples = _jobs_copies(jobs, in_refs, out_refs, sems)
    for local, sends, _ in triples:
        for cp in local:
            cp.wait()
        for cp in sends:
            cp.wait_send()
    for _, _, recvs in triples:
        for cp in recvs:
            cp.wait_recv()


def _run_jobs(jobs, name):
    ins, outs, alias = _jobs_io(jobs)

    def body(*refs):
        in_refs, out_refs, sems = refs[:len(ins)], refs[len(ins):len(ins) + len(outs)], refs[len(ins) + len(outs):]
        _jobs_start(jobs, in_refs, out_refs, sems)
        _jobs_wait(jobs, in_refs, out_refs, sems)

    return pl.pallas_call(
        body, name=name, in_specs=[HBM_SPEC] * len(ins), out_specs=[HBM_SPEC] * len(outs), out_shape=outs,
        scratch_shapes=_jobs_sems(jobs), input_output_aliases=alias,
    )(*ins)


_DIMS = {
    "nn": (((1,), (0,)), ((), ())),
    "nt": (((1,), (1,)), ((), ())),
    "tn": (((0,), (0,)), ((), ())),
}


def _matmul(a, b, mode, out_dtype, name, *, extras=(), post=None, jobs=(), tm=1024, tn=1024, tk=2048):
    assert a.dtype == BF16 and b.dtype == BF16
    if mode == "nn":
        (m, k), (k2, n) = a.shape, b.shape
    elif mode == "nt":
        (m, k), (n, k2) = a.shape, b.shape
    else:
        (k, m), (k2, n) = a.shape, b.shape
    assert k == k2
    tm, tn, tk = _pick(m, tm), _pick(n, tn), _pick(k, tk)
    nk = k // tk
    grid = (m // tm, n // tn, nk)
    dims = _DIMS[mode]
    out_dtypes = out_dtype if isinstance(out_dtype, (tuple, list)) else (out_dtype,)
    n_main, n_out = 2 + len(extras), len(out_dtypes)

    a_spec = pl.BlockSpec((tk, tm), lambda i, j, kk: (kk, i)) if mode == "tn" else pl.BlockSpec((tm, tk), lambda i, j, kk: (i, kk))
    b_spec = pl.BlockSpec((tn, tk), lambda i, j, kk: (j, kk)) if mode == "nt" else pl.BlockSpec((tk, tn), lambda i, j, kk: (kk, j))
    tile_spec = pl.BlockSpec((tm, tn), lambda i, j, kk: (i, j))
    job_ins, job_outs, job_alias = _jobs_io(jobs)
    operands = [a, b, *extras, *job_ins]
    in_specs = [a_spec, b_spec] + [tile_spec] * len(extras) + [HBM_SPEC] * len(job_ins)
    out_shape = [jax.ShapeDtypeStruct((m, n), dt) for dt in out_dtypes] + job_outs
    out_specs = [tile_spec] * n_out + [HBM_SPEC] * len(job_outs)
    aliases = {n_main + i: n_out + o for i, o in job_alias.items()}
    scratch = ([pltpu.VMEM((tm, tn), F32)] if nk > 1 else []) + (_jobs_sems(jobs) if jobs else [])

    def body(*refs):
        a_ref, b_ref = refs[0], refs[1]
        extra_refs = refs[2:n_main]
        job_in_refs = refs[n_main:n_main + len(job_ins)]
        outs = refs[n_main + len(job_ins):]
        o_refs, job_out_refs = outs[:n_out], outs[n_out:n_out + len(job_outs)]
        rest = outs[n_out + len(job_outs):]
        acc_ref = rest[0] if nk > 1 else None
        sems = rest[1:] if nk > 1 else rest
        step = (pl.program_id(0) * grid[1] + pl.program_id(1)) * grid[2] + pl.program_id(2)
        if jobs:
            pl.when(step == 0)(lambda: _jobs_start(jobs, job_in_refs, job_out_refs, sems))

        p = lax.dot_general(a_ref[...], b_ref[...], dims, preferred_element_type=F32)

        def finish(total):
            vals = post(total, *[r[...] for r in extra_refs]) if post is not None else total
            vals = vals if isinstance(vals, (tuple, list)) else (vals,)
            for o_ref, v in zip(o_refs, vals, strict=True):
                o_ref[...] = v.astype(o_ref.dtype)

        if nk == 1:
            finish(p)
        else:
            kk = pl.program_id(2)

            @pl.when(kk == 0)
            def _():
                acc_ref[...] = p

            @pl.when(kk > 0)
            def _():
                acc_ref[...] += p

            @pl.when(kk == nk - 1)
            def _():
                finish(acc_ref[...])

        if jobs:
            pl.when(step == grid[0] * grid[1] * grid[2] - 1)(lambda: _jobs_wait(jobs, job_in_refs, job_out_refs, sems))

    res = pl.pallas_call(
        body,
        name=name,
        grid=grid,
        in_specs=in_specs,
        out_specs=out_specs,
        out_shape=out_shape,
        scratch_shapes=scratch,
        input_output_aliases=aliases,
        compiler_params=_params(("arbitrary", "arbitrary", "arbitrary") if jobs else ("parallel", "parallel", "arbitrary")),
    )(*operands)
    return res[0] if len(res) == 1 else res


def _rowwise(fn, rows, consts, outs, reds, name, *, tr=256, ncol=1, n_rows=None, jobs=()):
    if n_rows is None:
        n_rows = rows[0][0].shape[0]
    tr = min(tr, n_rows)
    assert n_rows % tr == 0
    in_specs, operands = [], []
    for arr, width, coloff, rowoff in rows:
        in_specs.append(pl.BlockSpec((tr, width), functools.partial(lambda j, i, c, r: (i + r, j + c), c=coloff, r=rowoff)))
        operands.append(arr)
    for arr in consts:
        if arr.ndim == 3:
            in_specs.append(pl.BlockSpec(arr.shape, lambda j, i: (0, 0, 0)))
        else:
            in_specs.append(pl.BlockSpec(arr.shape, lambda j, i: (0, 0)))
        operands.append(arr)
    out_specs, out_shape = [], []
    for width, dtype in outs:
        out_specs.append(pl.BlockSpec((tr, width), lambda j, i: (i, j)))
        out_shape.append(jax.ShapeDtypeStruct((n_rows, ncol * width), dtype))
    for shape in reds:
        if len(shape) == 3:
            out_specs.append(pl.BlockSpec(shape, lambda j, i: (0, 0, 0)))
            out_shape.append(jax.ShapeDtypeStruct(shape, F32))
        else:
            out_specs.append(pl.BlockSpec(shape, lambda j, i: (0, j)))
            out_shape.append(jax.ShapeDtypeStruct((shape[0], ncol * shape[1]), F32))
    n_in, n_out, n_red = len(operands), len(outs), len(reds)
    job_ins, job_outs, job_alias = _jobs_io(jobs)
    n_ji, n_jo = len(job_ins), len(job_outs)
    n_steps = ncol * (n_rows // tr)

    def body(*refs):
        job_in_refs = refs[n_in:n_in + n_ji]
        o_refs = refs[n_in + n_ji:n_in + n_ji + n_out]
        red_refs = refs[n_in + n_ji + n_out:n_in + n_ji + n_out + n_red]
        job_out_refs = refs[n_in + n_ji + n_out + n_red:n_in + n_ji + n_out + n_red + n_jo]
        sems = refs[n_in + n_ji + n_out + n_red + n_jo:]
        step = pl.program_id(0) * (n_rows // tr) + pl.program_id(1)
        if jobs:
            pl.when(step == 0)(lambda: _jobs_start(jobs, job_in_refs, job_out_refs, sems))
        vals = fn(*[r[...] for r in refs[:n_in]])
        if not isinstance(vals, (tuple, list)):
            vals = (vals,)
        assert len(vals) == n_out + n_red
        for o_ref, v in zip(o_refs, vals[:n_out]):
            o_ref[...] = v.astype(o_ref.dtype)
        if n_red:
            first = pl.program_id(1) == 0
            for o_ref, v in zip(red_refs, vals[n_out:]):
                @pl.when(first)
                def _(o_ref=o_ref, v=v):
                    o_ref[...] = v

                @pl.when(jnp.logical_not(first))
                def _(o_ref=o_ref, v=v):
                    o_ref[...] += v
        if jobs:
            pl.when(step == n_steps - 1)(lambda: _jobs_wait(jobs, job_in_refs, job_out_refs, sems))

    res = pl.pallas_call(
        body,
        name=name,
        grid=(ncol, n_rows // tr),
        in_specs=in_specs + [HBM_SPEC] * n_ji,
        out_specs=out_specs + [HBM_SPEC] * n_jo,
        out_shape=out_shape + job_outs,
        scratch_shapes=_jobs_sems(jobs) if jobs else [],
        input_output_aliases={n_in + i: n_out + n_red + o for i, o in job_alias.items()},
        compiler_params=_params(("arbitrary", "arbitrary") if jobs else ("parallel", "arbitrary")),
    )(*(operands + job_ins))
    return res


def _row(arr, width=None, coloff=0, rowoff=0):
    return (arr, arr.shape[1] if width is None else width, coloff, rowoff)


def _rms(x, g):
    return x * lax.rsqrt(jnp.mean(x * x, axis=-1, keepdims=True) + EPS) * g


def _erf(x):
    x = jnp.clip(x, -4.0, 4.0)
    x2 = x * x
    alpha = x2 * -2.72614225801306e-10 + 2.77068142495902e-08
    alpha = alpha * x2 - 2.10102402082508e-06
    alpha = alpha * x2 - 5.69250639462346e-05
    alpha = alpha * x2 - 7.34990630326855e-04
    alpha = alpha * x2 - 2.95459980854025e-03
    alpha = alpha * x2 - 1.60960333262415e-02
    beta = x2 * -1.45660718464996e-05 - 2.13374055278905e-04
    beta = beta * x2 - 1.68282697438203e-03
    beta = beta * x2 - 7.37332916720468e-03
    beta = beta * x2 - 1.42647390514189e-02
    return x * alpha / beta


def _gelu(x):
    return 0.5 * x * (1.0 + _erf(x * (2.0 ** -0.5)))


def _gelu_grad(x):
    cdf = 0.5 * (1.0 + _erf(x * (2.0 ** -0.5)))
    pdf = jnp.exp(-0.5 * x * x) * (2.0 * jnp.pi) ** -0.5
    return cdf + x * pdf


def _layer_norm(x, g, b):
    mu = jnp.mean(x, axis=-1, keepdims=True)
    xc = x - mu
    return xc * lax.rsqrt(jnp.mean(xc * xc, axis=-1, keepdims=True) + EPS) * g + b


def _per_head(fn, x, g):
    parts = []
    for h in range(HEADS):
        sl = slice(h * HEAD_DIM, (h + 1) * HEAD_DIM)
        parts.append(fn(x[:, sl], g[:, sl]))
    return parts


def _sgu_mask():
    i = lax.broadcasted_iota(jnp.int32, (SGU_LEN, SGU_LEN), 0)
    j = lax.broadcasted_iota(jnp.int32, (SGU_LEN, SGU_LEN), 1)
    return (j // CHUNK) <= (i // CHUNK)


def _sgu_mix(vln_bf, w_bf, bias):
    rows = vln_bf.shape[0]
    out_rows = []
    for c in range(rows // SGU_LEN):
        cols = []
        for g in range(GROUPS):
            blk = vln_bf[c * SGU_LEN:(c + 1) * SGU_LEN, g * GROUP_DIM:(g + 1) * GROUP_DIM]
            mixed = jnp.dot(w_bf[g], blk, preferred_element_type=F32) + bias[g]
            cols.append(mixed)
        out_rows.append(jnp.concatenate(cols, axis=1))
    return out_rows[0] if len(out_rows) == 1 else jnp.concatenate(out_rows, axis=0)


def _split_dots(xs, m_bf):
    his = [x.astype(BF16) for x in xs]
    los = [(x - hi.astype(F32)).astype(BF16) for x, hi in zip(xs, his)]
    res = jnp.dot(jnp.concatenate(his + los, axis=0), m_bf, preferred_element_type=F32)
    n, rows = len(xs), xs[0].shape[0]
    return [res[h * rows:(h + 1) * rows] + res[(n + h) * rows:(n + h + 1) * rows] for h in range(n)]


def _att_tiles(qs, kbs, run_keeps, diag, want_sig=False):
    blk = ATT_BLOCK
    row = lax.broadcasted_iota(jnp.int32, (blk, blk), 0)
    col = lax.broadcasted_iota(jnp.int32, (blk, blk), 1)
    later = (row > col).astype(BF16)
    past = col < row
    zs = [lax.dot_general(q, kb, _DIMS["nt"], preferred_element_type=F32) * (HEAD_DIM ** -0.5) for q, kb in zip(qs, kbs)]
    es = [jnp.exp(-jnp.abs(z)) for z in zs]
    softplus = [jnp.maximum(z, 0.0) + jnp.log(1.0 + e) for z, e in zip(zs, es)]
    log_keeps = [jnp.where(past, -sp, 0.0) if diag else -sp for sp in softplus]
    tails = _split_dots(log_keeps, later)
    weights = [jnp.exp((z - sp) + (tail + keep)) for z, sp, tail, keep in zip(zs, softplus, tails, run_keeps)]
    if diag:
        weights = [jnp.where(past, a, 0.0) for a in weights]
    sigs = None
    if want_sig:
        invs = [1.0 / (1.0 + e) for e in es]
        sigs = [jnp.where(z >= 0.0, inv, e * inv) for z, e, inv in zip(zs, es, invs)]
    return weights, sigs, [jnp.sum(lk, axis=1, keepdims=True) for lk in log_keeps]


def _any_alive(keeps):
    return jnp.max(functools.reduce(jnp.maximum, keeps)) >= LOG_ZERO


def _attention_fwd(qn, kn, vb, *, heads_per_step=4, jobs=()):
    t = qn.shape[0]
    blk = ATT_BLOCK
    nq = t // blk
    hp = heads_per_step
    groups, width = HEADS // hp, hp * HEAD_DIM
    heads = [slice(h * HEAD_DIM, (h + 1) * HEAD_DIM) for h in range(hp)]
    job_ins, job_outs, job_alias = _jobs_io(jobs)
    n_ji, n_jo = len(job_ins), len(job_outs)

    def body(q_ref, k_ref, v_ref, *rest):
        job_in_refs, o_ref, o32_ref = rest[:n_ji], rest[n_ji], rest[n_ji + 1]
        job_out_refs, sems = rest[n_ji + 2:n_ji + 2 + n_jo], rest[n_ji + 2 + n_jo:]
        i = pl.program_id(1)
        step = pl.program_id(0) * nq + i
        if jobs:
            pl.when(step == 0)(lambda: _jobs_start(jobs, job_in_refs, job_out_refs, sems))
        qs = [q_ref[:, sl] for sl in heads]

        def tile(j, keeps, accs, diag):
            rows = pl.ds(pl.multiple_of(j * blk, blk), blk)
            weights, _, keep_sums = _att_tiles(qs, [k_ref[rows, sl] for sl in heads], keeps, diag)
            accs = [acc + jnp.dot(a.astype(BF16), v_ref[rows, sl], preferred_element_type=F32)
                    for acc, a, sl in zip(accs, weights, heads)]
            return tuple(k + s for k, s in zip(keeps, keep_sums)), tuple(accs)

        zero = tuple(jnp.zeros((blk, 1), F32) for _ in heads)
        keeps, accs = tile(i, zero, tuple(jnp.zeros((blk, HEAD_DIM), F32) for _ in heads), True)

        def cond(c):
            return jnp.logical_and(c[0] >= 0, _any_alive(c[1]))

        def step_fn(c):
            j, keeps, accs = c
            keeps, accs = tile(j, keeps, accs, False)
            return j - 1, keeps, accs

        _, _, accs = lax.while_loop(cond, step_fn, (i - 1, keeps, accs))
        for h, sl in enumerate(heads):
            o_ref[:, sl] = accs[h].astype(o_ref.dtype)
            o32_ref[:, sl] = accs[h]
        if jobs:
            pl.when(step == groups * nq - 1)(lambda: _jobs_wait(jobs, job_in_refs, job_out_refs, sems))

    blk_spec = pl.BlockSpec((blk, width), lambda g, i: (i, g))
    head_spec = pl.BlockSpec((t, width), lambda g, i: (0, g))
    return pl.pallas_call(
        body,
        name="attention_fwd",
        grid=(groups, nq),
        in_specs=[blk_spec, head_spec, head_spec] + [HBM_SPEC] * n_ji,
        out_specs=[blk_spec, blk_spec] + [HBM_SPEC] * n_jo,
        out_shape=[jax.ShapeDtypeStruct((t, SB_WIDTH), BF16), jax.ShapeDtypeStruct((t, SB_WIDTH), F32)] + job_outs,
        scratch_shapes=_jobs_sems(jobs) if jobs else [],
        input_output_aliases={3 + i: 2 + o for i, o in job_alias.items()},
        compiler_params=_params(("arbitrary", "arbitrary")),
    )(qn, kn, vb, *job_ins)


def _attention_bwd(qn, kn, vb, do, o32, *, heads_per_step=2, jobs=()):
    t = qn.shape[0]
    blk = ATT_BLOCK
    nq = t // blk
    hp = heads_per_step
    groups, width = HEADS // hp, hp * HEAD_DIM
    heads = [slice(h * HEAD_DIM, (h + 1) * HEAD_DIM) for h in range(hp)]
    job_ins, job_outs, job_alias = _jobs_io(jobs)
    n_ji, n_jo = len(job_ins), len(job_outs)

    def body(q_ref, k_ref, v_ref, do_ref, o32_ref, *rest):
        job_in_refs = rest[:n_ji]
        dq_ref, dk_ref, dv_ref = rest[n_ji:n_ji + 3]
        job_out_refs, sems = rest[n_ji + 3:n_ji + 3 + n_jo], rest[n_ji + 3 + n_jo:]
        i = pl.program_id(1)
        step = pl.program_id(0) * nq + i
        if jobs:
            pl.when(step == 0)(lambda: _jobs_start(jobs, job_in_refs, job_out_refs, sems))

        @pl.when(i == 0)
        def _():
            dk_ref[...] = jnp.zeros_like(dk_ref)
            dv_ref[...] = jnp.zeros_like(dv_ref)

        qs = [q_ref[:, sl] for sl in heads]
        dos = [do_ref[:, sl] for sl in heads]
        totals = [jnp.sum(d.astype(F32) * o32_ref[:, sl], axis=1, keepdims=True) for d, sl in zip(dos, heads)]
        row = lax.broadcasted_iota(jnp.int32, (blk, blk), 0)
        col = lax.broadcasted_iota(jnp.int32, (blk, blk), 1)
        from_here = (row >= col).astype(BF16)

        def tile(j, keeps, run_dlas, dqs, diag):
            rows = pl.ds(pl.multiple_of(j * blk, blk), blk)
            kbs = [k_ref[rows, sl] for sl in heads]
            weights, sigs, keep_sums = _att_tiles(qs, kbs, keeps, diag, want_sig=True)
            wbs = [a.astype(BF16) for a in weights]
            dps = [lax.dot_general(d, v_ref[rows, sl], _DIMS["nt"], preferred_element_type=F32) for d, sl in zip(dos, heads)]
            dlas = [wb.astype(F32) * dp for wb, dp in zip(wbs, dps)]
            later_sums = _split_dots(dlas, from_here)
            dzbs = []
            for dla, sig, total, later, run in zip(dlas, sigs, totals, later_sums, run_dlas):
                dz = dla * (1.0 - sig) - sig * (total - (later + run))
                if diag:
                    dz = jnp.where(col < row, dz, 0.0)
                dzbs.append((dz * (HEAD_DIM ** -0.5)).astype(BF16))
            dqs = [dq + jnp.dot(dzb, kb, preferred_element_type=F32) for dq, dzb, kb in zip(dqs, dzbs, kbs)]
            for sl, dzb, wb, q, d in zip(heads, dzbs, wbs, qs, dos):
                dk_ref[rows, sl] += lax.dot_general(dzb, q, _DIMS["tn"], preferred_element_type=F32)
                dv_ref[rows, sl] += lax.dot_general(wb, d, _DIMS["tn"], preferred_element_type=F32)
            keeps = tuple(k + s for k, s in zip(keeps, keep_sums))
            run_dlas = tuple(r + jnp.sum(dla, axis=1, keepdims=True) for r, dla in zip(run_dlas, dlas))
            return keeps, run_dlas, tuple(dqs)

        zero = tuple(jnp.zeros((blk, 1), F32) for _ in heads)
        keeps, run_dlas, dqs = tile(i, zero, zero, tuple(jnp.zeros((blk, HEAD_DIM), F32) for _ in heads), True)

        def cond(c):
            return jnp.logical_and(c[0] >= 0, _any_alive(c[1]))

        def step_fn(c):
            j, keeps, run_dlas, dqs = c
            keeps, run_dlas, dqs = tile(j, keeps, run_dlas, dqs, False)
            return j - 1, keeps, run_dlas, dqs

        _, _, _, dqs = lax.while_loop(cond, step_fn, (i - 1, keeps, run_dlas, dqs))
        for h, sl in enumerate(heads):
            dq_ref[:, sl] = dqs[h]
        if jobs:
            pl.when(step == groups * nq - 1)(lambda: _jobs_wait(jobs, job_in_refs, job_out_refs, sems))

    blk_spec = pl.BlockSpec((blk, width), lambda g, i: (i, g))
    head_spec = pl.BlockSpec((t, width), lambda g, i: (0, g), pipeline_mode=pl.Buffered(1))
    full = jax.ShapeDtypeStruct((t, SB_WIDTH), F32)
    return pl.pallas_call(
        body,
        name="attention_bwd",
        grid=(groups, nq),
        in_specs=[blk_spec, head_spec, head_spec, blk_spec, blk_spec] + [HBM_SPEC] * n_ji,
        out_specs=[blk_spec, head_spec, head_spec] + [HBM_SPEC] * n_jo,
        out_shape=[full, full, full] + job_outs,
        scratch_shapes=_jobs_sems(jobs) if jobs else [],
        input_output_aliases={5 + i: 3 + o for i, o in job_alias.items()},
        compiler_params=_params(("arbitrary", "arbitrary")),
    )(qn, kn, vb, do, o32, *job_ins)


def _sum_over_devices(v):
    rows, width = v.shape

    def body(v_ref, o_ref, buf, send_sems, recv_sems):
        x, y, c = _place()
        me = 4 * x + 2 * y + c
        buf[me] = v_ref[...]
        sends = []
        for d in range(1, N_DEV):
            px, py, pc = x ^ (d >> 2), y ^ ((d >> 1) & 1), c ^ (d & 1)
            cp = pltpu.make_async_remote_copy(
                src_ref=v_ref, dst_ref=buf.at[me], send_sem=send_sems.at[d - 1], recv_sem=recv_sems.at[d - 1],
                device_id=(px, py, pc), device_id_type=MESH)
            cp.start()
            sends.append((cp, 4 * px + 2 * py + pc))
        for cp, _ in sends:
            cp.wait_send()
        for d, (cp, peer) in enumerate(sends):
            pltpu.make_async_remote_copy(
                src_ref=v_ref, dst_ref=buf.at[peer], send_sem=send_sems.at[d], recv_sem=recv_sems.at[d],
                device_id=(x, y, c), device_id_type=MESH).wait_recv()
        total = buf[0]
        for d in range(1, N_DEV):
            total = total + buf[d]
        o_ref[...] = total

    vmem = pl.BlockSpec(memory_space=pltpu.VMEM)
    return pl.pallas_call(
        body,
        name="sum_over_devices",
        in_specs=[vmem],
        out_specs=vmem,
        out_shape=jax.ShapeDtypeStruct((rows, width), F32),
        scratch_shapes=[pltpu.VMEM((N_DEV, rows, width), F32), pltpu.SemaphoreType.DMA((N_DEV - 1,)), pltpu.SemaphoreType.DMA((N_DEV - 1,))],
        compiler_params=pltpu.CompilerParams(vmem_limit_bytes=V7X_VMEM_LIMIT),
    )(v)


def _adamw_math(w, g, m, v):
    m = ADAM_B1 * m + (1.0 - ADAM_B1) * g
    v = ADAM_B2 * v + (1.0 - ADAM_B2) * (g * g)
    m_hat = m / (1.0 - ADAM_B1 ** ADAM_STEP)
    v_hat = v / (1.0 - ADAM_B2 ** ADAM_STEP)
    delta = -ADAM_LR * (m_hat / (jnp.sqrt(v_hat) + ADAM_EPS) + ADAM_WD * w)
    return delta, m, v


def _adamw_sharded(part_a, part_b, w, m, v, name, jobs=()):
    width = _pick(w.shape[1], 1152)
    ncol = w.shape[1] // width

    def fn(a, b, w, m, v):
        g = a + b
        delta, m, v = _adamw_math(w, g, m, v)
        return g, delta, m, v

    return _rowwise(fn, [_row(t, width) for t in (part_a, part_b, w, m, v)], [], [(width, F32)] * 4, [], name, ncol=ncol, jobs=jobs)


def _adamw_small(g, w, m, v):
    def fn(g, w, m, v):
        return _adamw_math(w, g, m, v)

    return _rowwise(fn, [_row(t) for t in (g, w, m, v)], [], [(g.shape[1], F32)] * 3, [], "adamw_small", tr=g.shape[0])


def _flat(a):
    return a.reshape(-1, a.shape[-1])


def kernel(x, g_mix, w_in, g_q, g_k, sgu_ln_g, sgu_ln_b, w_spatial, b_spatial, w_oa, w_ob, w_out, g_ff, w_ff1, w_ff2, loss_target, m_g_mix, m_w_in, m_g_q, m_g_k, m_sgu_ln_g, m_sgu_ln_b, m_w_spatial, m_b_spatial, m_w_oa, m_w_ob, m_w_out, m_g_ff, m_w_ff1, m_w_ff2, v_g_mix, v_w_in, v_g_q, v_g_k, v_sgu_ln_g, v_sgu_ln_b, v_w_spatial, v_b_spatial, v_w_oa, v_w_ob, v_w_out, v_g_ff, v_w_ff1, v_w_ff2):
    depth = g_mix.shape[0]
    seq, d_model = x.shape[1], x.shape[2]
    d_ff = w_ff1.shape[2] * N_CHIPS
    in_cols = w_in.shape[2] * N_CHIPS
    col_gate_a = 3 * SB_WIDTH + 2 * SGU_WIDTH
    assert in_cols == col_gate_a + 2 * d_model

    big = [w_in, w_oa, w_ob, w_out, w_ff1, w_ff2]
    big_m = [m_w_in, m_w_oa, m_w_ob, m_w_out, m_w_ff1, m_w_ff2]
    big_v = [v_w_in, v_w_oa, v_w_ob, v_w_out, v_w_ff1, v_w_ff2]
    axes = [2, 2, 2, 1, 2, 1]
    W_IN, W_OA, W_OB, W_OUT, W_FF1, W_FF2 = range(6)
    shards = [w.astype(BF16) for w in big]
    full = [[None] * depth for _ in big]

    def gather(t, l):
        return _gather_job(shards[t], l, axes[t])

    def forward(t, l):
        return _forward_job(full[t][l], axes[t])

    def add_residual(total, res):
        return total + res

    (full[W_IN][0],) = _run_jobs([gather(W_IN, 0)], "gather_first")
    (full[W_IN][0],) = _run_jobs([forward(W_IN, 0)], "forward_first")

    gate_w = _pick(d_model, 1024)
    gate_n = d_model // gate_w
    bias_col = b_spatial[..., None]

    def rms_fwd(xin, g, name):
        return _rowwise(lambda xb, gb: _rms(xb, gb), [_row(xin)], [g], [(d_model, BF16)], [], name)[0]

    saved = []
    cur = x.reshape(seq, d_model)
    for l in range(depth):
        hb = rms_fwd(cur, g_mix[l:l + 1], "rms_mix")
        early = [W_FF1, W_OA, W_OB, W_OUT]
        proj, *got = _matmul(hb, full[W_IN][l], "nn", F32, "proj", jobs=[gather(t, l) for t in early])
        for t, w in zip(early, got):
            full[t][l] = w

        def qk_fn(q, k, v, gq, gk):
            qn = jnp.concatenate(_per_head(_rms, q, gq), axis=1)
            kn = jnp.concatenate(_per_head(_rms, k, gk), axis=1)
            return qn, kn, v

        gq, gk = g_q[l].reshape(1, SB_WIDTH), g_k[l].reshape(1, SB_WIDTH)
        qn, kn, vb = _rowwise(qk_fn, [_row(proj, SB_WIDTH, 0), _row(proj, SB_WIDTH, 1), _row(proj, SB_WIDTH, 2)],
                              [gq, gk], [(SB_WIDTH, BF16)] * 3, [], "qk_norm")
        o, o32, *got = _attention_fwd(qn, kn, vb, jobs=[forward(t, l) for t in early] + [gather(W_FF2, l)])
        for t, w in zip(early + [W_FF2], got):
            full[t][l] = w
        ya = _matmul(o, full[W_OA][l], "nn", F32, "proj_a")

        def sgu_fn(u_pre, v_pre, ln_g, ln_b, w_s, b_t):
            w_bf = jnp.where(_sgu_mask()[None], w_s, 0.0).astype(BF16)
            vln = _layer_norm(_gelu(v_pre), ln_g, ln_b)
            return _gelu(u_pre) * _sgu_mix(vln.astype(BF16), w_bf, b_t)

        sgu_consts = [sgu_ln_g[l:l + 1], sgu_ln_b[l:l + 1], w_spatial[l], bias_col[l]]
        s = _rowwise(sgu_fn, [_row(proj, SGU_WIDTH, 3), _row(proj, SGU_WIDTH, 4)], sgu_consts, [(SGU_WIDTH, BF16)], [], "sgu_fwd")[0]
        yb = _matmul(s, full[W_OB][l], "nn", F32, "proj_b")

        def merge_fn(ga, gb, a, b):
            return jax.nn.sigmoid(ga) * a + jax.nn.sigmoid(gb) * b

        gate_rows = [_row(proj, gate_w, col_gate_a // gate_w), _row(proj, gate_w, (col_gate_a + d_model) // gate_w)]
        merged = _rowwise(merge_fn, gate_rows + [_row(ya, gate_w), _row(yb, gate_w)], [], [(gate_w, BF16)], [], "merge", ncol=gate_n)[0]
        x1 = _matmul(merged, full[W_OUT][l], "nn", F32, "proj_out", extras=[cur], post=add_residual)
        h2 = rms_fwd(x1, g_ff[l:l + 1], "rms_ff")
        more = l + 1 < depth
        a1, r, full[W_FF2][l], *got = _matmul(h2, full[W_FF1][l], "nn", (F32, BF16), "ff1",
                                              jobs=[forward(W_FF2, l)] + ([gather(W_IN, l + 1)] if more else []),
                                              post=lambda total: (total, jnp.square(jnp.maximum(total, 0.0))))
        if more:
            full[W_IN][l + 1] = got[0]
            x2, full[W_IN][l + 1] = _matmul(r, full[W_FF2][l], "nn", F32, "ff2", extras=[x1], post=add_residual,
                                            jobs=[forward(W_IN, l + 1)])
        else:
            x2 = _matmul(r, full[W_FF2][l], "nn", F32, "ff2", extras=[x1], post=add_residual)
        saved.append(dict(x=cur, hb=hb, proj=proj, qn=qn, kn=kn, vb=vb, o=o, o32=o32, ya=ya, yb=yb, s=s, merged=merged,
                          x1=x1, h2=h2, a1=a1, r=r, gq=gq, gk=gk, sgu_consts=sgu_consts, gate_rows=gate_rows))
        cur = x2

    def loss_fn(y, target):
        err = y - target
        per_row = jnp.sum(err * err, axis=-1, keepdims=True) * (1.0 / d_model)
        part = 0.5 * jnp.sum(per_row, axis=0, keepdims=True)
        dy = err * (1.0 / d_model)
        return dy, dy, jnp.broadcast_to(part, (8, LANE))

    dx, dxb, loss_part = _rowwise(loss_fn, [_row(cur), _row(loss_target.reshape(seq, d_model))], [],
                                  [(d_model, F32), (d_model, BF16)], [(8, LANE)], "loss")
    loss = lax.psum(loss_part[0, 0], ("x", "y", "c"))

    landed = [None] * len(big)

    def scatter(t, grad, l):
        return _scatter_job(grad, landed[t], (4,) + shards[t].shape, l, axes[t])

    small = {n: [None] * depth for n in ("g_mix", "g_q", "g_k", "ln_g", "ln_b", "w_s", "b_s", "g_ff")}

    def rms_bwd(dh, xin, dres, g, name):
        def fn(dh, xin, dres, g):
            _, vjp = jax.vjp(_rms, xin, g)
            dxin, dg = vjp(dh)
            total = dres + dxin
            return total, total, dg

        return _rowwise(fn, [_row(dh), _row(xin), _row(dres)], [g], [(d_model, F32), (d_model, BF16)], [(1, d_model)], name)

    for l in reversed(range(depth)):
        sv = saved[l]
        da1 = _matmul(dxb, full[W_FF2][l], "nt", BF16, "ff2_dx", extras=[sv["a1"]],
                      post=lambda total, a: total * (2.0 * jnp.maximum(a, 0.0)))
        grad_ff2 = _matmul(sv["r"], dxb, "tn", BF16, "ff2_dw", tk=WGRAD_TK)
        dh2 = _matmul(da1, full[W_FF1][l], "nt", F32, "ff1_dx")
        grad_ff1 = _matmul(sv["h2"], da1, "tn", BF16, "ff1_dw", tk=WGRAD_TK)
        dx1, dx1b, small["g_ff"][l] = rms_bwd(dh2, sv["x1"], dx, g_ff[l:l + 1], "rms_ff_bwd")

        dmerged = _matmul(dx1b, full[W_OUT][l], "nt", F32, "out_dx")
        grad_out = _matmul(sv["merged"], dx1b, "tn", BF16, "out_dw", tk=WGRAD_TK)

        def merge_bwd_fn(dm, ga, gb, a, b):
            sa, sb = jax.nn.sigmoid(ga), jax.nn.sigmoid(gb)
            return dm * a * sa * (1.0 - sa), dm * b * sb * (1.0 - sb), dm * sa, dm * sb

        dga, dgb, dya, dyb = _rowwise(merge_bwd_fn, [_row(dmerged, gate_w)] + sv["gate_rows"] + [_row(sv["ya"], gate_w), _row(sv["yb"], gate_w)],
                                      [], [(gate_w, BF16)] * 4, [], "merge_bwd", ncol=gate_n)

        ds = _matmul(dyb, full[W_OB][l], "nt", F32, "ob_dx")
        grad_ob = _matmul(sv["s"], dyb, "tn", BF16, "ob_dw", tk=WGRAD_TK)

        def sgu_bwd_fn(ds, u_pre, v_pre, ln_g, ln_b, w_s, b_t):
            mask = _sgu_mask()
            w_bf = jnp.where(mask[None], w_s, 0.0).astype(BF16)
            vln, ln_vjp = jax.vjp(lambda vg, g, b: _layer_norm(vg, g, b), _gelu(v_pre), ln_g, ln_b)
            vln_bf = vln.astype(BF16)
            mixed = _sgu_mix(vln_bf, w_bf, b_t)
            du_pre = ds * mixed * _gelu_grad(u_pre)
            dmixed = ds * _gelu(u_pre)
            dm_bf = dmixed.astype(BF16)
            dw = [jnp.zeros((SGU_LEN, SGU_LEN), F32) for _ in range(GROUPS)]
            db = [jnp.zeros((SGU_LEN, 1), F32) for _ in range(GROUPS)]
            dvln_rows = []
            for c in range(ds.shape[0] // SGU_LEN):
                rows = slice(c * SGU_LEN, (c + 1) * SGU_LEN)
                cols = []
                for g in range(GROUPS):
                    sl = slice(g * GROUP_DIM, (g + 1) * GROUP_DIM)
                    cols.append(lax.dot_general(w_bf[g], dm_bf[rows, sl], _DIMS["tn"], preferred_element_type=F32))
                    dw[g] = dw[g] + lax.dot_general(dm_bf[rows, sl], vln_bf[rows, sl], _DIMS["nt"], preferred_element_type=F32)
                    db[g] = db[g] + jnp.sum(dmixed[rows, sl], axis=1, keepdims=True)
                dvln_rows.append(jnp.concatenate(cols, axis=1))
            dvln = dvln_rows[0] if len(dvln_rows) == 1 else jnp.concatenate(dvln_rows, axis=0)
            dvg, dln_g, dln_b = ln_vjp(dvln)
            dv_pre = dvg * _gelu_grad(v_pre)
            dw_s = jnp.stack([jnp.where(mask, d, 0.0) for d in dw])
            return du_pre, dv_pre, dln_g, dln_b, dw_s, jnp.stack(db)

        du, dvs, small["ln_g"][l], small["ln_b"][l], small["w_s"][l], db_col = _rowwise(
            sgu_bwd_fn, [_row(ds), _row(sv["proj"], SGU_WIDTH, 3), _row(sv["proj"], SGU_WIDTH, 4)], sv["sgu_consts"],
            [(SGU_WIDTH, BF16)] * 2, [(1, SGU_WIDTH), (1, SGU_WIDTH), (GROUPS, SGU_LEN, SGU_LEN), (GROUPS, SGU_LEN, 1)], "sgu_bwd", tr=128)
        small["b_s"][l] = db_col[..., 0]

        do = _matmul(dya, full[W_OA][l], "nt", BF16, "oa_dx")
        grad_oa = _matmul(sv["o"], dya, "tn", BF16, "oa_dw", tk=WGRAD_TK)
        dqn, dkn, dv, landed[W_FF2], landed[W_OUT], landed[W_OB], landed[W_OA] = _attention_bwd(
            sv["qn"], sv["kn"], sv["vb"], do, sv["o32"],
            jobs=[scatter(W_FF2, grad_ff2, l), scatter(W_OUT, grad_out, l), scatter(W_OB, grad_ob, l), scatter(W_OA, grad_oa, l)])

        def qk_bwd_fn(dqn, dkn, dv, q, k, gq, gk):
            outs = []
            for d, xin, g in ((dqn, q, gq), (dkn, k, gk)):
                dxs, dgs = [], []
                for h in range(HEADS):
                    sl = slice(h * HEAD_DIM, (h + 1) * HEAD_DIM)
                    _, vjp = jax.vjp(_rms, xin[:, sl], g[:, sl])
                    dxh, dgh = vjp(d[:, sl])
                    dxs.append(dxh)
                    dgs.append(dgh)
                outs.append((jnp.concatenate(dxs, axis=1), jnp.concatenate(dgs, axis=1)))
            return outs[0][0], outs[1][0], dv, outs[0][1], outs[1][1]

        dq, dk, dvb, small["g_q"][l], small["g_k"][l] = _rowwise(
            qk_bwd_fn, [_row(dqn), _row(dkn), _row(dv), _row(sv["proj"], SB_WIDTH, 0), _row(sv["proj"], SB_WIDTH, 1)],
            [sv["gq"], sv["gk"]], [(SB_WIDTH, BF16)] * 3, [(1, SB_WIDTH), (1, SB_WIDTH)], "qk_norm_bwd")

        dproj = jnp.concatenate([dq, dk, dvb, du, dvs, dga, dgb], axis=1)
        grad_in, landed[W_FF1] = _matmul(sv["hb"], dproj, "tn", BF16, "in_dw", tk=WGRAD_TK, jobs=[scatter(W_FF1, grad_ff1, l)])
        dh, landed[W_IN] = _matmul(dproj, full[W_IN][l], "nt", F32, "in_dx", tk=3072, jobs=[scatter(W_IN, grad_in, l)])
        dx, dxb, small["g_mix"][l] = rms_bwd(dh, sv["x"], dx1, g_mix[l:l + 1], "rms_mix_bwd")

    def sum_chips(t, jobs):
        flat = landed[t].reshape(-1, landed[t].shape[-1])
        shard_rows = flat.shape[0] // 4
        width = _pick(flat.shape[1], 1152)
        tr = min(256, shard_rows)
        rows = [_row(flat, width, 0, s * (shard_rows // tr)) for s in range(4)]
        return _rowwise(lambda a, b, c, d: ((a.astype(F32) + b.astype(F32)) + c.astype(F32)) + d.astype(F32),
                        rows, [], [(width, F32)], [], "sum_chips", tr=tr, ncol=flat.shape[1] // width, n_rows=shard_rows, jobs=jobs)

    partial, from_sibling, big_out = [None] * len(big), [None] * len(big), [None] * len(big)
    chain = [W_OA, W_OB, W_OUT, W_FF1, W_FF2, W_IN]
    for prev, t in zip([None] + chain[:-1], chain):
        partial[t], *got = sum_chips(t, [_swap_job(partial[prev])] if prev is not None else [])
        if got:
            from_sibling[prev] = got[0]
    for n, t in enumerate([W_FF1, W_FF2, W_OA, W_OB, W_OUT, W_IN]):
        *res, = _adamw_sharded(partial[t], from_sibling[t], _flat(big[t]), _flat(big_m[t]), _flat(big_v[t]), "adamw",
                               jobs=[_swap_job(partial[W_IN])] if n == 0 else [])
        if n == 0:
            from_sibling[W_IN] = res.pop()
        big_out[t] = [r.reshape(big[t].shape) for r in res]

    names = ["g_mix", "g_q", "g_k", "ln_g", "ln_b", "w_s", "b_s", "g_ff"]
    small_w = [g_mix, g_q, g_k, sgu_ln_g, sgu_ln_b, w_spatial, b_spatial, g_ff]
    small_m = [m_g_mix, m_g_q, m_g_k, m_sgu_ln_g, m_sgu_ln_b, m_w_spatial, m_b_spatial, m_g_ff]
    small_v = [v_g_mix, v_g_q, v_g_k, v_sgu_ln_g, v_sgu_ln_b, v_w_spatial, v_b_spatial, v_g_ff]

    def pack(parts):
        return jnp.concatenate([p.reshape(-1, LANE) for p in parts], axis=0)

    local_small = pack([jnp.stack(small[n]) for n in names])
    g_small = _sum_over_devices(local_small)
    d_small, m_small, v_small = _adamw_small(g_small, pack(small_w), pack(small_m), pack(small_v))

    def unpack(packed):
        outs, row = [], 0
        for w in small_w:
            n = w.size // LANE
            outs.append(packed[row:row + n].reshape(w.shape))
            row += n
        return outs

    small_out = [unpack(p) for p in (g_small, d_small, m_small, v_small)]

    order = [("s", 0), ("b", W_IN), ("s", 1), ("s", 2), ("s", 3), ("s", 4), ("s", 5), ("s", 6),
             ("b", W_OA), ("b", W_OB), ("b", W_OUT), ("s", 7), ("b", W_FF1), ("b", W_FF2)]
    result = [loss, dx.reshape(x.shape)]
    for kind in range(4):
        for which, idx in order:
            result.append(small_out[kind][idx] if which == "s" else big_out[idx][kind])
    return tuple(result)
```

```python
import functools

import jax
import jax.numpy as jnp
from jax import lax
from jax.experimental import pallas as pl
from jax.experimental.pallas import tpu as pltpu

F32 = jnp.float32
BF16 = jnp.bfloat16
MESH = pl.DeviceIdType.MESH

EPS = 1e-6
HEADS = 8
HEAD_DIM = 128
SB_WIDTH = HEADS * HEAD_DIM
GROUPS = 8
GROUP_DIM = 128
SGU_WIDTH = GROUPS * GROUP_DIM
SGU_LEN = 128
CHUNK = 64
ATT_BLOCK = 128
LOG_ZERO = -104.0

ADAM_LR = 0.001
ADAM_B1 = 0.9
ADAM_B2 = 0.999
ADAM_EPS = 1e-08
ADAM_WD = 0.01
ADAM_STEP = 10

N_CHIPS = 4
N_DEV = 8
V7X_VMEM_LIMIT = 48 * 1024 * 1024
WGRAD_TK = 2048
LANE = 128


def _params(sem):
    return pltpu.CompilerParams(dimension_semantics=sem, vmem_limit_bytes=V7X_VMEM_LIMIT)


def _pick(dim, pref):
    if dim <= pref:
        return dim
    for t in range(pref - pref % LANE, 0, -LANE):
        if dim % t == 0:
            return t
    raise ValueError(f"no tile for {dim}")


HBM_SPEC = pl.BlockSpec(memory_space=pl.ANY)


def _place():
    return lax.axis_index("x"), lax.axis_index("y"), lax.axis_index("c")


def _other_chips(x, y):
    return [(1 - x, y), (x, 1 - y), (1 - x, 1 - y)]


def _shard_view(ref, axis, index):
    size = ref.shape[axis] // N_CHIPS
    start = pl.multiple_of(index * size, size)
    if axis == 0:
        return ref.at[pl.ds(start, size), :]
    return ref.at[:, pl.ds(start, size)]


def _half_view(view, half):
    rows = view.shape[0] // 2
    return view.at[pl.ds(pl.multiple_of(half * rows, 8), rows), :]


def _gather_job(shards, layer, axis):
    return dict(kind="gather", src=shards, layer=layer, axis=axis - 1)


def _forward_job(weight, axis):
    return dict(kind="forward", src=weight, axis=axis - 1)


def _scatter_job(grad, landed, shape, layer, axis):
    return dict(kind="scatter", src=grad, buf=landed, shape=shape, layer=layer, axis=axis - 1)


def _swap_job(arr):
    return dict(kind="swap", src=arr)


def _jobs_io(jobs):
    ins, outs, alias = [], [], {}
    for job in jobs:
        src = job["src"]
        ins.append(src)
        if job["kind"] == "gather":
            shape = list(src.shape[1:])
            shape[job["axis"]] *= N_CHIPS
            outs.append(jax.ShapeDtypeStruct(tuple(shape), src.dtype))
        elif job["kind"] == "forward":
            alias[len(ins) - 1] = len(outs)
            outs.append(jax.ShapeDtypeStruct(src.shape, src.dtype))
        elif job["kind"] == "scatter":
            if job["buf"] is not None:
                ins.append(job["buf"])
                alias[len(ins) - 1] = len(outs)
            outs.append(jax.ShapeDtypeStruct(job["shape"], src.dtype))
        else:
            outs.append(jax.ShapeDtypeStruct(src.shape, src.dtype))
    return ins, outs, alias


def _jobs_sems(jobs):
    n = len(jobs)
    return [pltpu.SemaphoreType.DMA((3 * n,)), pltpu.SemaphoreType.DMA((3 * n,)), pltpu.SemaphoreType.DMA((n,))]


def _jobs_copies(jobs, in_refs, out_refs, sems):
    send_sems, recv_sems, local_sems = sems
    x, y, c = _place()
    me = 2 * x + y
    sibling = (x, y, 1 - c)
    triples, ip = [], 0
    for n, (job, out) in enumerate(zip(jobs, out_refs)):
        kind = job["kind"]
        src = in_refs[ip]
        ip += 2 if (kind == "scatter" and job["buf"] is not None) else 1

        def remote(k, src_ref, dst_ref, to):
            return pltpu.make_async_remote_copy(
                src_ref=src_ref, dst_ref=dst_ref, send_sem=send_sems.at[3 * n + k], recv_sem=recv_sems.at[3 * n + k],
                device_id=to, device_id_type=MESH)

        local, sends, recvs = [], [], []
        if kind == "gather":
            axis = job["axis"]
            mine = src.at[job["layer"]]
            local.append(pltpu.make_async_copy(mine, _shard_view(out, axis, me), local_sems.at[n]))
            for k, (px, py) in enumerate(_other_chips(x, y)):
                sends.append(remote(k, _half_view(mine, c), _half_view(_shard_view(out, axis, me), c), (px, py, c)))
                recvs.append(remote(k, _half_view(mine, c), _half_view(_shard_view(out, axis, 2 * px + py), c), (px, py, c)))
        elif kind == "forward":
            axis = job["axis"]
            for k, (px, py) in enumerate(_other_chips(x, y)):
                got = _shard_view(src, axis, 2 * px + py)
                lands = _shard_view(out, axis, 2 * px + py)
                sends.append(remote(k, _half_view(got, c), _half_view(lands, c), sibling))
                recvs.append(remote(k, _half_view(got, c), _half_view(lands, 1 - c), sibling))
        elif kind == "scatter":
            axis, layer = job["axis"], job["layer"]
            local.append(pltpu.make_async_copy(_shard_view(src, axis, me), out.at[3, layer], local_sems.at[n]))
            for k, (px, py) in enumerate(_other_chips(x, y)):
                cp = remote(k, _shard_view(src, axis, 2 * px + py), out.at[k, layer], (px, py, c))
                sends.append(cp)
                recvs.append(cp)
        else:
            cp = remote(0, src, out, sibling)
            sends.append(cp)
            recvs.append(cp)
        triples.append((local, sends, recvs))
    return triples


def _jobs_start(jobs, in_refs, out_refs, sems):
    for local, sends, _ in _jobs_copies(jobs, in_refs, out_refs, sems):
        for cp in local + sends:
            cp.start()


def _jobs_wait(jobs, in_refs, out_refs, sems):
    triples = _jobs_copies(jobs, in_refs, out_refs, sems)
    for local, sends, _ in triples:
        for cp in local:
            cp.wait()
        for cp in sends:
            cp.wait_send()
    for _, _, recvs in triples:
        for cp in recvs:
            cp.wait_recv()


def _run_jobs(jobs, name):
    ins, outs, alias = _jobs_io(jobs)

    def body(*refs):
        in_refs, out_refs, sems = refs[:len(ins)], refs[len(ins):len(ins) + len(outs)], refs[len(ins) + len(outs):]
        _jobs_start(jobs, in_refs, out_refs, sems)
        _jobs_wait(jobs, in_refs, out_refs, sems)

    return pl.pallas_call(
        body, name=name, in_specs=[HBM_SPEC] * len(ins), out_specs=[HBM_SPEC] * len(outs), out_shape=outs,
        scratch_shapes=_jobs_sems(jobs), input_output_aliases=alias,
    )(*ins)


_DIMS = {
    "nn": (((1,), (0,)), ((), ())),
    "nt": (((1,), (1,)), ((), ())),
    "tn": (((0,), (0,)), ((), ())),
}


def _matmul(a, b, mode, out_dtype, name, *, extras=(), post=None, jobs=(), tm=1024, tn=1024, tk=2048):
    assert a.dtype == BF16 and b.dtype == BF16
    if mode == "nn":
        (m, k), (k2, n) = a.shape, b.shape
    elif mode == "nt":
        (m, k), (n, k2) = a.shape, b.shape
    else:
        (k, m), (k2, n) = a.shape, b.shape
    assert k == k2
    tm, tn, tk = _pick(m, tm), _pick(n, tn), _pick(k, tk)
    nk = k // tk
    grid = (m // tm, n // tn, nk)
    dims = _DIMS[mode]
    out_dtypes = out_dtype if isinstance(out_dtype, (tuple, list)) else (out_dtype,)
    n_main, n_out = 2 + len(extras), len(out_dtypes)

    a_spec = pl.BlockSpec((tk, tm), lambda i, j, kk: (kk, i)) if mode == "tn" else pl.BlockSpec((tm, tk), lambda i, j, kk: (i, kk))
    b_spec = pl.BlockSpec((tn, tk), lambda i, j, kk: (j, kk)) if mode == "nt" else pl.BlockSpec((tk, tn), lambda i, j, kk: (kk, j))
    tile_spec = pl.BlockSpec((tm, tn), lambda i, j, kk: (i, j))
    extras = [e if isinstance(e, tuple) else (e, 0) for e in extras]
    extra_specs = [pl.BlockSpec((tm, tn), functools.partial(lambda i, j, kk, c: (i, j + c), c=c)) for _, c in extras]
    job_ins, job_outs, job_alias = _jobs_io(jobs)
    operands = [a, b, *[e for e, _ in extras], *job_ins]
    in_specs = [a_spec, b_spec] + extra_specs + [HBM_SPEC] * len(job_ins)
    out_shape = [jax.ShapeDtypeStruct((m, n), dt) for dt in out_dtypes] + job_outs
    out_specs = [tile_spec] * n_out + [HBM_SPEC] * len(job_outs)
    aliases = {n_main + i: n_out + o for i, o in job_alias.items()}
    scratch = ([pltpu.VMEM((tm, tn), F32)] if nk > 1 else []) + (_jobs_sems(jobs) if jobs else [])

    def body(*refs):
        a_ref, b_ref = refs[0], refs[1]
        extra_refs = refs[2:n_main]
        job_in_refs = refs[n_main:n_main + len(job_ins)]
        outs = refs[n_main + len(job_ins):]
        o_refs, job_out_refs = outs[:n_out], outs[n_out:n_out + len(job_outs)]
        rest = outs[n_out + len(job_outs):]
        acc_ref = rest[0] if nk > 1 else None
        sems = rest[1:] if nk > 1 else rest
        step = (pl.program_id(0) * grid[1] + pl.program_id(1)) * grid[2] + pl.program_id(2)
        if jobs:
            pl.when(step == 0)(lambda: _jobs_start(jobs, job_in_refs, job_out_refs, sems))

        p = lax.dot_general(a_ref[...], b_ref[...], dims, preferred_element_type=F32)

        def finish(total):
            vals = post(total, *[r[...] for r in extra_refs]) if post is not None else total
            vals = vals if isinstance(vals, (tuple, list)) else (vals,)
            for o_ref, v in zip(o_refs, vals, strict=True):
                o_ref[...] = v.astype(o_ref.dtype)

        if nk == 1:
            finish(p)
        else:
            kk = pl.program_id(2)

            @pl.when(kk == 0)
            def _():
                acc_ref[...] = p

            @pl.when(kk > 0)
            def _():
                acc_ref[...] += p

            @pl.when(kk == nk - 1)
            def _():
                finish(acc_ref[...])

        if jobs:
            pl.when(step == grid[0] * grid[1] * grid[2] - 1)(lambda: _jobs_wait(jobs, job_in_refs, job_out_refs, sems))

    res = pl.pallas_call(
        body,
        name=name,
        grid=grid,
        in_specs=in_specs,
        out_specs=out_specs,
        out_shape=out_shape,
        scratch_shapes=scratch,
        input_output_aliases=aliases,
        compiler_params=_params(("arbitrary", "arbitrary", "arbitrary") if jobs else ("parallel", "parallel", "arbitrary")),
    )(*operands)
    return res[0] if len(res) == 1 else res


def _rowwise(fn, rows, consts, outs, reds, name, *, tr=256, ncol=1, n_rows=None, jobs=()):
    if n_rows is None:
        n_rows = rows[0][0].shape[0]
    tr = min(tr, n_rows)
    assert n_rows % tr == 0
    in_specs, operands = [], []
    for arr, width, coloff, rowoff in rows:
        in_specs.append(pl.BlockSpec((tr, width), functools.partial(lambda j, i, c, r: (i + r, j + c), c=coloff, r=rowoff)))
        operands.append(arr)
    for arr in consts:
        if arr.ndim == 3:
            in_specs.append(pl.BlockSpec(arr.shape, lambda j, i: (0, 0, 0)))
        else:
            in_specs.append(pl.BlockSpec(arr.shape, lambda j, i: (0, 0)))
        operands.append(arr)
    out_specs, out_shape = [], []
    for width, dtype in outs:
        out_specs.append(pl.BlockSpec((tr, width), lambda j, i: (i, j)))
        out_shape.append(jax.ShapeDtypeStruct((n_rows, ncol * width), dtype))
    for shape in reds:
        if len(shape) == 3:
            out_specs.append(pl.BlockSpec(shape, lambda j, i: (0, 0, 0)))
            out_shape.append(jax.ShapeDtypeStruct(shape, F32))
        else:
            out_specs.append(pl.BlockSpec(shape, lambda j, i: (0, j)))
            out_shape.append(jax.ShapeDtypeStruct((shape[0], ncol * shape[1]), F32))
    n_in, n_out, n_red = len(operands), len(outs), len(reds)
    job_ins, job_outs, job_alias = _jobs_io(jobs)
    n_ji, n_jo = len(job_ins), len(job_outs)
    n_steps = ncol * (n_rows // tr)

    def body(*refs):
        job_in_refs = refs[n_in:n_in + n_ji]
        o_refs = refs[n_in + n_ji:n_in + n_ji + n_out]
        red_refs = refs[n_in + n_ji + n_out:n_in + n_ji + n_out + n_red]
        job_out_refs = refs[n_in + n_ji + n_out + n_red:n_in + n_ji + n_out + n_red + n_jo]
        sems = refs[n_in + n_ji + n_out + n_red + n_jo:]
        step = pl.program_id(0) * (n_rows // tr) + pl.program_id(1)
        if jobs:
            pl.when(step == 0)(lambda: _jobs_start(jobs, job_in_refs, job_out_refs, sems))
        vals = fn(*[r[...] for r in refs[:n_in]])
        if not isinstance(vals, (tuple, list)):
            vals = (vals,)
        assert len(vals) == n_out + n_red
        for o_ref, v in zip(o_refs, vals[:n_out]):
            o_ref[...] = v.astype(o_ref.dtype)
        if n_red:
            first = pl.program_id(1) == 0
            for o_ref, v in zip(red_refs, vals[n_out:]):
                @pl.when(first)
                def _(o_ref=o_ref, v=v):
                    o_ref[...] = v

                @pl.when(jnp.logical_not(first))
                def _(o_ref=o_ref, v=v):
                    o_ref[...] += v
        if jobs:
            pl.when(step == n_steps - 1)(lambda: _jobs_wait(jobs, job_in_refs, job_out_refs, sems))

    res = pl.pallas_call(
        body,
        name=name,
        grid=(ncol, n_rows // tr),
        in_specs=in_specs + [HBM_SPEC] * n_ji,
        out_specs=out_specs + [HBM_SPEC] * n_jo,
        out_shape=out_shape + job_outs,
        scratch_shapes=_jobs_sems(jobs) if jobs else [],
        input_output_aliases={n_in + i: n_out + n_red + o for i, o in job_alias.items()},
        compiler_params=_params(("arbitrary", "arbitrary") if jobs else ("parallel", "arbitrary")),
    )(*(operands + job_ins))
    return res


def _row(arr, width=None, coloff=0, rowoff=0):
    return (arr, arr.shape[1] if width is None else width, coloff, rowoff)


def _rms(x, g):
    return x * lax.rsqrt(jnp.mean(x * x, axis=-1, keepdims=True) + EPS) * g


def _erf(x):
    x = jnp.clip(x, -4.0, 4.0)
    x2 = x * x
    alpha = x2 * -2.72614225801306e-10 + 2.77068142495902e-08
    alpha = alpha * x2 - 2.10102402082508e-06
    alpha = alpha * x2 - 5.69250639462346e-05
    alpha = alpha * x2 - 7.34990630326855e-04
    alpha = alpha * x2 - 2.95459980854025e-03
    alpha = alpha * x2 - 1.60960333262415e-02
    beta = x2 * -1.45660718464996e-05 - 2.13374055278905e-04
    beta = beta * x2 - 1.68282697438203e-03
    beta = beta * x2 - 7.37332916720468e-03
    beta = beta * x2 - 1.42647390514189e-02
    return x * alpha / beta


def _gelu(x):
    return 0.5 * x * (1.0 + _erf(x * (2.0 ** -0.5)))


def _gelu_and_grad(x):
    cdf = 0.5 * (1.0 + _erf(x * (2.0 ** -0.5)))
    pdf = jnp.exp(-0.5 * x * x) * (2.0 * jnp.pi) ** -0.5
    return x * cdf, cdf + x * pdf


def _layer_norm(x, g, b):
    mu = jnp.mean(x, axis=-1, keepdims=True)
    xc = x - mu
    return xc * lax.rsqrt(jnp.mean(xc * xc, axis=-1, keepdims=True) + EPS) * g + b


def _per_head(fn, x, g):
    parts = []
    for h in range(HEADS):
        sl = slice(h * HEAD_DIM, (h + 1) * HEAD_DIM)
        parts.append(fn(x[:, sl], g[:, sl]))
    return parts


def _sgu_mask():
    i = lax.broadcasted_iota(jnp.int32, (SGU_LEN, SGU_LEN), 0)
    j = lax.broadcasted_iota(jnp.int32, (SGU_LEN, SGU_LEN), 1)
    return (j // CHUNK) <= (i // CHUNK)


def _sgu_mix(vln_bf, w_bf, bias):
    rows = vln_bf.shape[0]
    out_rows = []
    for c in range(rows // SGU_LEN):
        cols = []
        for g in range(GROUPS):
            blk = vln_bf[c * SGU_LEN:(c + 1) * SGU_LEN, g * GROUP_DIM:(g + 1) * GROUP_DIM]
            mixed = jnp.dot(w_bf[g], blk, preferred_element_type=F32) + bias[g]
            cols.append(mixed)
        out_rows.append(jnp.concatenate(cols, axis=1))
    return out_rows[0] if len(out_rows) == 1 else jnp.concatenate(out_rows, axis=0)


def _split_dots(xs, m_bf):
    his = [x.astype(BF16) for x in xs]
    los = [(x - hi.astype(F32)).astype(BF16) for x, hi in zip(xs, his)]
    res = jnp.dot(jnp.concatenate(his + los, axis=0), m_bf, preferred_element_type=F32)
    n, rows = len(xs), xs[0].shape[0]
    return [res[h * rows:(h + 1) * rows] + res[(n + h) * rows:(n + h + 1) * rows] for h in range(n)]


def _att_tiles(qs, kbs, run_keeps, diag, want_sig=False):
    blk = ATT_BLOCK
    row = lax.broadcasted_iota(jnp.int32, (blk, blk), 0)
    col = lax.broadcasted_iota(jnp.int32, (blk, blk), 1)
    later = (row > col).astype(BF16)
    past = col < row
    zs = [lax.dot_general(q, kb, _DIMS["nt"], preferred_element_type=F32) * (HEAD_DIM ** -0.5) for q, kb in zip(qs, kbs)]
    es = [jnp.exp(-jnp.abs(z)) for z in zs]
    softplus = [jnp.maximum(z, 0.0) + jnp.log(1.0 + e) for z, e in zip(zs, es)]
    log_keeps = [jnp.where(past, -sp, 0.0) if diag else -sp for sp in softplus]
    tails = _split_dots(log_keeps, later)
    weights = [jnp.exp((z - sp) + (tail + keep)) for z, sp, tail, keep in zip(zs, softplus, tails, run_keeps)]
    if diag:
        weights = [jnp.where(past, a, 0.0) for a in weights]
    sigs = None
    if want_sig:
        invs = [1.0 / (1.0 + e) for e in es]
        sigs = [jnp.where(z >= 0.0, inv, e * inv) for z, e, inv in zip(zs, es, invs)]
    return weights, sigs, [jnp.sum(lk, axis=1, keepdims=True) for lk in log_keeps]


def _any_alive(keeps):
    return jnp.max(functools.reduce(jnp.maximum, keeps)) >= LOG_ZERO


def _attention_fwd(qn, kn, vb, *, heads_per_step=4, jobs=()):
    t = qn.shape[0]
    blk = ATT_BLOCK
    nq = t // blk
    hp = heads_per_step
    groups, width = HEADS // hp, hp * HEAD_DIM
    heads = [slice(h * HEAD_DIM, (h + 1) * HEAD_DIM) for h in range(hp)]
    job_ins, job_outs, job_alias = _jobs_io(jobs)
    n_ji, n_jo = len(job_ins), len(job_outs)

    def body(q_ref, k_ref, v_ref, *rest):
        job_in_refs, o_ref, o32_ref = rest[:n_ji], rest[n_ji], rest[n_ji + 1]
        job_out_refs, sems = rest[n_ji + 2:n_ji + 2 + n_jo], rest[n_ji + 2 + n_jo:]
        i = pl.program_id(1)
        step = pl.program_id(0) * nq + i
        if jobs:
            pl.when(step == 0)(lambda: _jobs_start(jobs, job_in_refs, job_out_refs, sems))
        qs = [q_ref[:, sl] for sl in heads]

        def tile(j, keeps, accs, diag):
            rows = pl.ds(pl.multiple_of(j * blk, blk), blk)
            weights, _, keep_sums = _att_tiles(qs, [k_ref[rows, sl] for sl in heads], keeps, diag)
            accs = [acc + jnp.dot(a.astype(BF16), v_ref[rows, sl], preferred_element_type=F32)
                    for acc, a, sl in zip(accs, weights, heads)]
            return tuple(k + s for k, s in zip(keeps, keep_sums)), tuple(accs)

        zero = tuple(jnp.zeros((blk, 1), F32) for _ in heads)
        keeps, accs = tile(i, zero, tuple(jnp.zeros((blk, HEAD_DIM), F32) for _ in heads), True)

        def cond(c):
            return jnp.logical_and(c[0] >= 0, _any_alive(c[1]))

        def step_fn(c):
            j, keeps, accs = c
            keeps, accs = tile(j, keeps, accs, False)
            return j - 1, keeps, accs

        _, _, accs = lax.while_loop(cond, step_fn, (i - 1, keeps, accs))
        for h, sl in enumerate(heads):
            o_ref[:, sl] = accs[h].astype(o_ref.dtype)
            o32_ref[:, sl] = accs[h]
        if jobs:
            pl.when(step == groups * nq - 1)(lambda: _jobs_wait(jobs, job_in_refs, job_out_refs, sems))

    blk_spec = pl.BlockSpec((blk, width), lambda g, i: (i, g))
    head_spec = pl.BlockSpec((t, width), lambda g, i: (0, g))
    return pl.pallas_call(
        body,
        name="attention_fwd",
        grid=(groups, nq),
        in_specs=[blk_spec, head_spec, head_spec] + [HBM_SPEC] * n_ji,
        out_specs=[blk_spec, blk_spec] + [HBM_SPEC] * n_jo,
        out_shape=[jax.ShapeDtypeStruct((t, SB_WIDTH), BF16), jax.ShapeDtypeStruct((t, SB_WIDTH), F32)] + job_outs,
        scratch_shapes=_jobs_sems(jobs) if jobs else [],
        input_output_aliases={3 + i: 2 + o for i, o in job_alias.items()},
        compiler_params=_params(("arbitrary", "arbitrary")),
    )(qn, kn, vb, *job_ins)


def _attention_bwd(qn, kn, vb, do, o32, *, heads_per_step=2, jobs=()):
    t = qn.shape[0]
    blk = ATT_BLOCK
    nq = t // blk
    hp = heads_per_step
    groups, width = HEADS // hp, hp * HEAD_DIM
    heads = [slice(h * HEAD_DIM, (h + 1) * HEAD_DIM) for h in range(hp)]
    job_ins, job_outs, job_alias = _jobs_io(jobs)
    n_ji, n_jo = len(job_ins), len(job_outs)

    def body(q_ref, k_ref, v_ref, do_ref, o32_ref, *rest):
        job_in_refs = rest[:n_ji]
        dq_ref, dk_ref, dv_ref = rest[n_ji:n_ji + 3]
        job_out_refs, sems = rest[n_ji + 3:n_ji + 3 + n_jo], rest[n_ji + 3 + n_jo:]
        i = pl.program_id(1)
        step = pl.program_id(0) * nq + i
        if jobs:
            pl.when(step == 0)(lambda: _jobs_start(jobs, job_in_refs, job_out_refs, sems))

        @pl.when(i == 0)
        def _():
            dk_ref[...] = jnp.zeros_like(dk_ref)
            dv_ref[...] = jnp.zeros_like(dv_ref)

        qs = [q_ref[:, sl] for sl in heads]
        dos = [do_ref[:, sl] for sl in heads]
        totals = [jnp.sum(d.astype(F32) * o32_ref[:, sl], axis=1, keepdims=True) for d, sl in zip(dos, heads)]
        row = lax.broadcasted_iota(jnp.int32, (blk, blk), 0)
        col = lax.broadcasted_iota(jnp.int32, (blk, blk), 1)
        from_here = (row >= col).astype(BF16)

        def tile(j, keeps, run_dlas, dqs, diag):
            rows = pl.ds(pl.multiple_of(j * blk, blk), blk)
            kbs = [k_ref[rows, sl] for sl in heads]
            weights, sigs, keep_sums = _att_tiles(qs, kbs, keeps, diag, want_sig=True)
            wbs = [a.astype(BF16) for a in weights]
            dps = [lax.dot_general(d, v_ref[rows, sl], _DIMS["nt"], preferred_element_type=F32) for d, sl in zip(dos, heads)]
            dlas = [wb.astype(F32) * dp for wb, dp in zip(wbs, dps)]
            later_sums = _split_dots(dlas, from_here)
            dzbs = []
            for dla, sig, total, later, run in zip(dlas, sigs, totals, later_sums, run_dlas):
                dz = dla * (1.0 - sig) - sig * (total - (later + run))
                if diag:
                    dz = jnp.where(col < row, dz, 0.0)
                dzbs.append((dz * (HEAD_DIM ** -0.5)).astype(BF16))
            dqs = [dq + jnp.dot(dzb, kb, preferred_element_type=F32) for dq, dzb, kb in zip(dqs, dzbs, kbs)]
            for sl, dzb, wb, q, d in zip(heads, dzbs, wbs, qs, dos):
                dk_ref[rows, sl] += lax.dot_general(dzb, q, _DIMS["tn"], preferred_element_type=F32)
                dv_ref[rows, sl] += lax.dot_general(wb, d, _DIMS["tn"], preferred_element_type=F32)
            keeps = tuple(k + s for k, s in zip(keeps, keep_sums))
            run_dlas = tuple(r + jnp.sum(dla, axis=1, keepdims=True) for r, dla in zip(run_dlas, dlas))
            return keeps, run_dlas, tuple(dqs)

        zero = tuple(jnp.zeros((blk, 1), F32) for _ in heads)
        keeps, run_dlas, dqs = tile(i, zero, zero, tuple(jnp.zeros((blk, HEAD_DIM), F32) for _ in heads), True)

        def cond(c):
            return jnp.logical_and(c[0] >= 0, _any_alive(c[1]))

        def step_fn(c):
            j, keeps, run_dlas, dqs = c
            keeps, run_dlas, dqs = tile(j, keeps, run_dlas, dqs, False)
            return j - 1, keeps, run_dlas, dqs

        _, _, _, dqs = lax.while_loop(cond, step_fn, (i - 1, keeps, run_dlas, dqs))
        for h, sl in enumerate(heads):
            dq_ref[:, sl] = dqs[h]
        if jobs:
            pl.when(step == groups * nq - 1)(lambda: _jobs_wait(jobs, job_in_refs, job_out_refs, sems))

    blk_spec = pl.BlockSpec((blk, width), lambda g, i: (i, g))
    head_spec = pl.BlockSpec((t, width), lambda g, i: (0, g), pipeline_mode=pl.Buffered(1))
    full = jax.ShapeDtypeStruct((t, SB_WIDTH), F32)
    return pl.pallas_call(
        body,
        name="attention_bwd",
        grid=(groups, nq),
        in_specs=[blk_spec, head_spec, head_spec, blk_spec, blk_spec] + [HBM_SPEC] * n_ji,
        out_specs=[blk_spec, head_spec, head_spec] + [HBM_SPEC] * n_jo,
        out_shape=[full, full, full] + job_outs,
        scratch_shapes=_jobs_sems(jobs) if jobs else [],
        input_output_aliases={5 + i: 3 + o for i, o in job_alias.items()},
        compiler_params=_params(("arbitrary", "arbitrary")),
    )(qn, kn, vb, do, o32, *job_ins)


def _sum_over_devices(v, jobs=()):
    rows, width = v.shape
    job_ins, job_outs, job_alias = _jobs_io(jobs)
    n_ji, n_jo = len(job_ins), len(job_outs)

    def body(v_ref, *rest):
        job_in_refs, o_ref = rest[:n_ji], rest[n_ji]
        job_out_refs = rest[n_ji + 1:n_ji + 1 + n_jo]
        buf, send_sems, recv_sems = rest[n_ji + 1 + n_jo:n_ji + 4 + n_jo]
        job_sems = rest[n_ji + 4 + n_jo:]
        if jobs:
            _jobs_start(jobs, job_in_refs, job_out_refs, job_sems)
        x, y, c = _place()
        me = 4 * x + 2 * y + c
        buf[me] = v_ref[...]
        sends = []
        for d in range(1, N_DEV):
            px, py, pc = x ^ (d >> 2), y ^ ((d >> 1) & 1), c ^ (d & 1)
            cp = pltpu.make_async_remote_copy(
                src_ref=v_ref, dst_ref=buf.at[me], send_sem=send_sems.at[d - 1], recv_sem=recv_sems.at[d - 1],
                device_id=(px, py, pc), device_id_type=MESH)
            cp.start()
            sends.append((cp, 4 * px + 2 * py + pc))
        for cp, _ in sends:
            cp.wait_send()
        for d, (cp, peer) in enumerate(sends):
            pltpu.make_async_remote_copy(
                src_ref=v_ref, dst_ref=buf.at[peer], send_sem=send_sems.at[d], recv_sem=recv_sems.at[d],
                device_id=(x, y, c), device_id_type=MESH).wait_recv()
        total = buf[0]
        for d in range(1, N_DEV):
            total = total + buf[d]
        o_ref[...] = total
        if jobs:
            _jobs_wait(jobs, job_in_refs, job_out_refs, job_sems)

    vmem = pl.BlockSpec(memory_space=pltpu.VMEM)
    return pl.pallas_call(
        body,
        name="sum_over_devices",
        in_specs=[vmem] + [HBM_SPEC] * n_ji,
        out_specs=[vmem] + [HBM_SPEC] * n_jo,
        out_shape=[jax.ShapeDtypeStruct((rows, width), F32)] + job_outs,
        scratch_shapes=[pltpu.VMEM((N_DEV, rows, width), F32), pltpu.SemaphoreType.DMA((N_DEV - 1,)), pltpu.SemaphoreType.DMA((N_DEV - 1,))]
        + (_jobs_sems(jobs) if jobs else []),
        input_output_aliases={1 + i: 1 + o for i, o in job_alias.items()},
        compiler_params=pltpu.CompilerParams(vmem_limit_bytes=V7X_VMEM_LIMIT),
    )(v, *job_ins)


def _adamw_math(w, g, m, v):
    m = ADAM_B1 * m + (1.0 - ADAM_B1) * g
    v = ADAM_B2 * v + (1.0 - ADAM_B2) * (g * g)
    m_hat = m / (1.0 - ADAM_B1 ** ADAM_STEP)
    v_hat = v / (1.0 - ADAM_B2 ** ADAM_STEP)
    delta = -ADAM_LR * (m_hat / (jnp.sqrt(v_hat) + ADAM_EPS) + ADAM_WD * w)
    return delta, m, v


def _adamw_sharded(part_a, part_b, w, m, v, name, jobs=()):
    width = _pick(w.shape[1], 1152)
    ncol = w.shape[1] // width

    def fn(a, b, w, m, v):
        g = a + b
        delta, m, v = _adamw_math(w, g, m, v)
        return g, delta, m, v

    return _rowwise(fn, [_row(t, width) for t in (part_a, part_b, w, m, v)], [], [(width, F32)] * 4, [], name, ncol=ncol, jobs=jobs)


def _adamw_small(g, w, m, v):
    def fn(g, w, m, v):
        return _adamw_math(w, g, m, v)

    return _rowwise(fn, [_row(t) for t in (g, w, m, v)], [], [(g.shape[1], F32)] * 3, [], "adamw_small", tr=g.shape[0])


def _flat(a):
    return a.reshape(-1, a.shape[-1])


def kernel(x, g_mix, w_in, g_q, g_k, sgu_ln_g, sgu_ln_b, w_spatial, b_spatial, w_oa, w_ob, w_out, g_ff, w_ff1, w_ff2, loss_target, m_g_mix, m_w_in, m_g_q, m_g_k, m_sgu_ln_g, m_sgu_ln_b, m_w_spatial, m_b_spatial, m_w_oa, m_w_ob, m_w_out, m_g_ff, m_w_ff1, m_w_ff2, v_g_mix, v_w_in, v_g_q, v_g_k, v_sgu_ln_g, v_sgu_ln_b, v_w_spatial, v_b_spatial, v_w_oa, v_w_ob, v_w_out, v_g_ff, v_w_ff1, v_w_ff2):
    depth = g_mix.shape[0]
    seq, d_model = x.shape[1], x.shape[2]
    d_ff = w_ff1.shape[2] * N_CHIPS
    in_cols = w_in.shape[2] * N_CHIPS
    col_gate_a = 3 * SB_WIDTH + 2 * SGU_WIDTH
    assert in_cols == col_gate_a + 2 * d_model

    big = [w_in, w_oa, w_ob, w_out, w_ff1, w_ff2]
    big_m = [m_w_in, m_w_oa, m_w_ob, m_w_out, m_w_ff1, m_w_ff2]
    big_v = [v_w_in, v_w_oa, v_w_ob, v_w_out, v_w_ff1, v_w_ff2]
    axes = [2, 2, 2, 1, 2, 1]
    W_IN, W_OA, W_OB, W_OUT, W_FF1, W_FF2 = range(6)
    shards = [w.astype(BF16) for w in big]
    full = [[None] * depth for _ in big]

    def gather(t, l):
        return _gather_job(shards[t], l, axes[t])

    def forward(t, l):
        return _forward_job(full[t][l], axes[t])

    def add_residual(total, res):
        return total + res

    (full[W_IN][0],) = _run_jobs([gather(W_IN, 0)], "gather_first")
    (full[W_IN][0],) = _run_jobs([forward(W_IN, 0)], "forward_first")

    gate_w = _pick(d_model, 1024)
    bias_col = b_spatial[..., None]

    def rms_fwd(xin, g, name):
        return _rowwise(lambda xb, gb: _rms(xb, gb), [_row(xin)], [g], [(d_model, BF16)], [], name)[0]

    saved = []
    cur = x.reshape(seq, d_model)
    for l in range(depth):
        hb = rms_fwd(cur, g_mix[l:l + 1], "rms_mix")
        early = [W_FF1, W_OA, W_OB, W_OUT]
        proj, *got = _matmul(hb, full[W_IN][l], "nn", F32, "proj", jobs=[gather(t, l) for t in early])
        for t, w in zip(early, got):
            full[t][l] = w

        def qk_fn(q, k, v, gq, gk):
            qn = jnp.concatenate(_per_head(_rms, q, gq), axis=1)
            kn = jnp.concatenate(_per_head(_rms, k, gk), axis=1)
            return qn, kn, v

        gq, gk = g_q[l].reshape(1, SB_WIDTH), g_k[l].reshape(1, SB_WIDTH)
        qn, kn, vb = _rowwise(qk_fn, [_row(proj, SB_WIDTH, 0), _row(proj, SB_WIDTH, 1), _row(proj, SB_WIDTH, 2)],
                              [gq, gk], [(SB_WIDTH, BF16)] * 3, [], "qk_norm")
        o, o32, *got = _attention_fwd(qn, kn, vb, jobs=[forward(t, l) for t in early] + [gather(W_FF2, l)])
        for t, w in zip(early + [W_FF2], got):
            full[t][l] = w
        ya = _matmul(o, full[W_OA][l], "nn", F32, "proj_a")

        def sgu_fn(u_pre, v_pre, ln_g, ln_b, w_s, b_t):
            w_bf = jnp.where(_sgu_mask()[None], w_s, 0.0).astype(BF16)
            vln = _layer_norm(_gelu(v_pre), ln_g, ln_b)
            return _gelu(u_pre) * _sgu_mix(vln.astype(BF16), w_bf, b_t)

        sgu_consts = [sgu_ln_g[l:l + 1], sgu_ln_b[l:l + 1], w_spatial[l], bias_col[l]]
        s = _rowwise(sgu_fn, [_row(proj, SGU_WIDTH, 3), _row(proj, SGU_WIDTH, 4)], sgu_consts, [(SGU_WIDTH, BF16)], [], "sgu_fwd")[0]
        def merge_fn(b, ga, gb, a):
            return b, jax.nn.sigmoid(ga) * a + jax.nn.sigmoid(gb) * b

        gates = [(proj, col_gate_a // gate_w), (proj, (col_gate_a + d_model) // gate_w)]
        yb, merged = _matmul(s, full[W_OB][l], "nn", (F32, BF16), "proj_b", extras=gates + [ya], post=merge_fn, tm=512, tn=gate_w)
        x1 = _matmul(merged, full[W_OUT][l], "nn", F32, "proj_out", extras=[cur], post=add_residual)
        h2 = rms_fwd(x1, g_ff[l:l + 1], "rms_ff")
        more = l + 1 < depth
        a1, r, full[W_FF2][l], *got = _matmul(h2, full[W_FF1][l], "nn", (F32, BF16), "ff1",
                                              jobs=[forward(W_FF2, l)] + ([gather(W_IN, l + 1)] if more else []),
                                              post=lambda total: (total, jnp.square(jnp.maximum(total, 0.0))))
        if more:
            full[W_IN][l + 1] = got[0]
            x2, full[W_IN][l + 1] = _matmul(r, full[W_FF2][l], "nn", F32, "ff2", extras=[x1], post=add_residual,
                                            jobs=[forward(W_IN, l + 1)])
        else:
            x2 = _matmul(r, full[W_FF2][l], "nn", F32, "ff2", extras=[x1], post=add_residual)
        saved.append(dict(x=cur, hb=hb, proj=proj, qn=qn, kn=kn, vb=vb, o=o, o32=o32, ya=ya, yb=yb, s=s, merged=merged,
                          x1=x1, h2=h2, a1=a1, r=r, gq=gq, gk=gk, sgu_consts=sgu_consts, gates=gates))
        cur = x2

    def loss_fn(y, target):
        err = y - target
        per_row = jnp.sum(err * err, axis=-1, keepdims=True) * (1.0 / d_model)
        part = 0.5 * jnp.sum(per_row, axis=0, keepdims=True)
        dy = err * (1.0 / d_model)
        return dy, dy, jnp.broadcast_to(part, (8, LANE))

    dx, dxb, loss_part = _rowwise(loss_fn, [_row(cur), _row(loss_target.reshape(seq, d_model))], [],
                                  [(d_model, F32), (d_model, BF16)], [(8, LANE)], "loss")
    loss = lax.psum(loss_part[0, 0], ("x", "y", "c"))

    landed = [None] * len(big)

    def scatter(t, grad, l):
        return _scatter_job(grad, landed[t], (4,) + shards[t].shape, l, axes[t])

    small = {n: [None] * depth for n in ("g_mix", "g_q", "g_k", "ln_g", "ln_b", "w_s", "b_s", "g_ff")}

    def rms_bwd(dh, xin, dres, g, name):
        def fn(dh, xin, dres, g):
            _, vjp = jax.vjp(_rms, xin, g)
            dxin, dg = vjp(dh)
            total = dres + dxin
            return total, total, dg

        return _rowwise(fn, [_row(dh), _row(xin), _row(dres)], [g], [(d_model, F32), (d_model, BF16)], [(1, d_model)], name)

    for l in reversed(range(depth)):
        sv = saved[l]
        da1 = _matmul(dxb, full[W_FF2][l], "nt", BF16, "ff2_dx", extras=[sv["a1"]],
                      post=lambda total, a: total * (2.0 * jnp.maximum(a, 0.0)))
        grad_ff2 = _matmul(sv["r"], dxb, "tn", BF16, "ff2_dw", tk=WGRAD_TK)
        dh2 = _matmul(da1, full[W_FF1][l], "nt", F32, "ff1_dx")
        grad_ff1 = _matmul(sv["h2"], da1, "tn", BF16, "ff1_dw", tk=WGRAD_TK)
        dx1, dx1b, small["g_ff"][l] = rms_bwd(dh2, sv["x1"], dx, g_ff[l:l + 1], "rms_ff_bwd")

        def merge_bwd_fn(dm, ga, gb, a, b):
            sa, sb = jax.nn.sigmoid(ga), jax.nn.sigmoid(gb)
            return dm * a * sa * (1.0 - sa), dm * b * sb * (1.0 - sb), dm * sa, dm * sb

        dga, dgb, dya, dyb = _matmul(dx1b, full[W_OUT][l], "nt", (BF16,) * 4, "out_dx", tm=512, tn=gate_w,
                                     extras=sv["gates"] + [sv["ya"], sv["yb"]], post=merge_bwd_fn)
        grad_out = _matmul(sv["merged"], dx1b, "tn", BF16, "out_dw", tk=WGRAD_TK)

        ds = _matmul(dyb, full[W_OB][l], "nt", F32, "ob_dx")
        grad_ob = _matmul(sv["s"], dyb, "tn", BF16, "ob_dw", tk=WGRAD_TK)

        def sgu_bwd_fn(ds, u_pre, v_pre, ln_g, ln_b, w_s, b_t):
            mask = _sgu_mask()
            w_bf = jnp.where(mask[None], w_s, 0.0).astype(BF16)
            u, u_grad = _gelu_and_grad(u_pre)
            vg, vg_grad = _gelu_and_grad(v_pre)
            vln, ln_vjp = jax.vjp(_layer_norm, vg, ln_g, ln_b)
            vln_bf = vln.astype(BF16)
            mixed = _sgu_mix(vln_bf, w_bf, b_t)
            du_pre = ds * mixed * u_grad
            dmixed = ds * u
            dm_bf = dmixed.astype(BF16)
            dw = [jnp.zeros((SGU_LEN, SGU_LEN), F32) for _ in range(GROUPS)]
            db = [jnp.zeros((SGU_LEN, 1), F32) for _ in range(GROUPS)]
            dvln_rows = []
            for c in range(ds.shape[0] // SGU_LEN):
                rows = slice(c * SGU_LEN, (c + 1) * SGU_LEN)
                cols = []
                for g in range(GROUPS):
                    sl = slice(g * GROUP_DIM, (g + 1) * GROUP_DIM)
                    cols.append(lax.dot_general(w_bf[g], dm_bf[rows, sl], _DIMS["tn"], preferred_element_type=F32))
                    dw[g] = dw[g] + lax.dot_general(dm_bf[rows, sl], vln_bf[rows, sl], _DIMS["nt"], preferred_element_type=F32)
                    db[g] = db[g] + jnp.sum(dmixed[rows, sl], axis=1, keepdims=True)
                dvln_rows.append(jnp.concatenate(cols, axis=1))
            dvln = dvln_rows[0] if len(dvln_rows) == 1 else jnp.concatenate(dvln_rows, axis=0)
            dvg, dln_g, dln_b = ln_vjp(dvln)
            dv_pre = dvg * vg_grad
            dw_s = jnp.stack([jnp.where(mask, d, 0.0) for d in dw])
            return du_pre, dv_pre, dln_g, dln_b, dw_s, jnp.stack(db)

        du, dvs, small["ln_g"][l], small["ln_b"][l], small["w_s"][l], db_col = _rowwise(
            sgu_bwd_fn, [_row(ds), _row(sv["proj"], SGU_WIDTH, 3), _row(sv["proj"], SGU_WIDTH, 4)], sv["sgu_consts"],
            [(SGU_WIDTH, BF16)] * 2, [(1, SGU_WIDTH), (1, SGU_WIDTH), (GROUPS, SGU_LEN, SGU_LEN), (GROUPS, SGU_LEN, 1)], "sgu_bwd", tr=128)
        small["b_s"][l] = db_col[..., 0]

        do = _matmul(dya, full[W_OA][l], "nt", BF16, "oa_dx")
        grad_oa = _matmul(sv["o"], dya, "tn", BF16, "oa_dw", tk=WGRAD_TK)
        dqn, dkn, dv, landed[W_FF2], landed[W_OUT], landed[W_OB], landed[W_OA] = _attention_bwd(
            sv["qn"], sv["kn"], sv["vb"], do, sv["o32"],
            jobs=[scatter(W_FF2, grad_ff2, l), scatter(W_OUT, grad_out, l), scatter(W_OB, grad_ob, l), scatter(W_OA, grad_oa, l)])

        def qk_bwd_fn(dqn, dkn, dv, q, k, gq, gk):
            outs = []
            for d, xin, g in ((dqn, q, gq), (dkn, k, gk)):
                dxs, dgs = [], []
                for h in range(HEADS):
                    sl = slice(h * HEAD_DIM, (h + 1) * HEAD_DIM)
                    _, vjp = jax.vjp(_rms, xin[:, sl], g[:, sl])
                    dxh, dgh = vjp(d[:, sl])
                    dxs.append(dxh)
                    dgs.append(dgh)
                outs.append((jnp.concatenate(dxs, axis=1), jnp.concatenate(dgs, axis=1)))
            return outs[0][0], outs[1][0], dv, outs[0][1], outs[1][1]

        dq, dk, dvb, small["g_q"][l], small["g_k"][l] = _rowwise(
            qk_bwd_fn, [_row(dqn), _row(dkn), _row(dv), _row(sv["proj"], SB_WIDTH, 0), _row(sv["proj"], SB_WIDTH, 1)],
            [sv["gq"], sv["gk"]], [(SB_WIDTH, BF16)] * 3, [(1, SB_WIDTH), (1, SB_WIDTH)], "qk_norm_bwd")

        dproj = jnp.concatenate([dq, dk, dvb, du, dvs, dga, dgb], axis=1)
        grad_in, landed[W_FF1] = _matmul(sv["hb"], dproj, "tn", BF16, "in_dw", tk=WGRAD_TK, jobs=[scatter(W_FF1, grad_ff1, l)])
        dh, landed[W_IN] = _matmul(dproj, full[W_IN][l], "nt", F32, "in_dx", tk=3072, jobs=[scatter(W_IN, grad_in, l)])
        dx, dxb, small["g_mix"][l] = rms_bwd(dh, sv["x"], dx1, g_mix[l:l + 1], "rms_mix_bwd")

    def sum_chips(t, jobs):
        flat = landed[t].reshape(-1, landed[t].shape[-1])
        shard_rows = flat.shape[0] // 4
        width = _pick(flat.shape[1], 1152)
        tr = min(256, shard_rows)
        rows = [_row(flat, width, 0, s * (shard_rows // tr)) for s in range(4)]
        return _rowwise(lambda a, b, c, d: ((a.astype(F32) + b.astype(F32)) + c.astype(F32)) + d.astype(F32),
                        rows, [], [(width, F32)], [], "sum_chips", tr=tr, ncol=flat.shape[1] // width, n_rows=shard_rows, jobs=jobs)

    partial, from_sibling, big_out = [None] * len(big), [None] * len(big), [None] * len(big)
    chain = [W_OA, W_OB, W_OUT, W_IN, W_FF1, W_FF2]
    for prev, t in zip([None] + chain[:-1], chain):
        partial[t], *got = sum_chips(t, [_swap_job(partial[prev])] if prev is not None else [])
        if got:
            from_sibling[prev] = got[0]

    names = ["g_mix", "g_q", "g_k", "ln_g", "ln_b", "w_s", "b_s", "g_ff"]
    small_w = [g_mix, g_q, g_k, sgu_ln_g, sgu_ln_b, w_spatial, b_spatial, g_ff]
    small_m = [m_g_mix, m_g_q, m_g_k, m_sgu_ln_g, m_sgu_ln_b, m_w_spatial, m_b_spatial, m_g_ff]
    small_v = [v_g_mix, v_g_q, v_g_k, v_sgu_ln_g, v_sgu_ln_b, v_w_spatial, v_b_spatial, v_g_ff]

    def pack(parts):
        return jnp.concatenate([p.reshape(-1, LANE) for p in parts], axis=0)

    local_small = pack([jnp.stack(small[n]) for n in names])
    g_small, from_sibling[chain[-1]] = _sum_over_devices(local_small, [_swap_job(partial[chain[-1]])])
    d_small, m_small, v_small = _adamw_small(g_small, pack(small_w), pack(small_m), pack(small_v))
    for t in range(len(big)):
        res = _adamw_sharded(partial[t], from_sibling[t], _flat(big[t]), _flat(big_m[t]), _flat(big_v[t]), "adamw")
        big_out[t] = [r.reshape(big[t].shape) for r in res]

    def unpack(packed):
        outs, row = [], 0
        for w in small_w:
            n = w.size // LANE
            outs.append(packed[row:row + n].reshape(w.shape))
            row += n
        return outs

    small_out = [unpack(p) for p in (g_small, d_small, m_small, v_small)]

    order = [("s", 0), ("b", W_IN), ("s", 1), ("s", 2), ("s", 3), ("s", 4), ("s", 5), ("s", 6),
             ("b", W_OA), ("b", W_OB), ("b", W_OUT), ("s", 7), ("b", W_FF1), ("b", W_FF2)]
    result = [loss, dx.reshape(x.shape)]
    for kind in range(4):
        for which, idx in order:
            result.append(small_out[kind][idx] if which == "s" else big_out[idx][kind])
    return tuple(result)
```

```python
import functools

import jax
import jax.numpy as jnp
from jax import lax
from jax.experimental import pallas as pl
from jax.experimental.pallas import tpu as pltpu

F32 = jnp.float32
BF16 = jnp.bfloat16
MESH = pl.DeviceIdType.MESH

EPS = 1e-6
HEADS = 8
HEAD_DIM = 128
SB_WIDTH = HEADS * HEAD_DIM
GROUPS = 8
GROUP_DIM = 128
SGU_WIDTH = GROUPS * GROUP_DIM
SGU_LEN = 128
CHUNK = 64
ATT_BLOCK = 128
LOG_ZERO = -104.0

ADAM_LR = 0.001
ADAM_B1 = 0.9
ADAM_B2 = 0.999
ADAM_EPS = 1e-08
ADAM_WD = 0.01
ADAM_STEP = 10

N_CHIPS = 4
N_DEV = 8
V7X_VMEM_LIMIT = 48 * 1024 * 1024
ATT_BWD_VMEM_LIMIT = 56 * 1024 * 1024
WGRAD_TK = 2048
LANE = 128


def _params(sem, vmem_limit=V7X_VMEM_LIMIT):
    return pltpu.CompilerParams(dimension_semantics=sem, vmem_limit_bytes=vmem_limit)


def _pick(dim, pref):
    if dim <= pref:
        return dim
    for t in range(pref - pref % LANE, 0, -LANE):
        if dim % t == 0:
            return t
    raise ValueError(f"no tile for {dim}")


HBM_SPEC = pl.BlockSpec(memory_space=pl.ANY)


def _place():
    return lax.axis_index("x"), lax.axis_index("y"), lax.axis_index("c")


def _other_chips(x, y):
    return [(1 - x, y), (x, 1 - y), (1 - x, 1 - y)]


def _shard_view(ref, axis, index):
    size = ref.shape[axis] // N_CHIPS
    start = pl.multiple_of(index * size, size)
    if axis == 0:
        return ref.at[pl.ds(start, size), :]
    return ref.at[:, pl.ds(start, size)]


def _half_view(view, half):
    rows = view.shape[0] // 2
    return view.at[pl.ds(pl.multiple_of(half * rows, 8), rows), :]


def _gather_job(shards, layer, axis):
    return dict(kind="gather", src=shards, layer=layer, axis=axis - 1)


def _forward_job(weight, axis):
    return dict(kind="forward", src=weight, axis=axis - 1)


def _scatter_job(grad, landed, shape, layer, axis):
    return dict(kind="scatter", src=grad, buf=landed, shape=shape, layer=layer, axis=axis - 1)


def _mirror_job(landed, theirs, layer):
    return dict(kind="mirror", src=landed, buf=theirs, layer=layer)


def _jobs_io(jobs):
    ins, outs, alias = [], [], {}
    for job in jobs:
        src = job["src"]
        ins.append(src)
        if job["kind"] == "gather":
            shape = list(src.shape[1:])
            shape[job["axis"]] *= N_CHIPS
            outs.append(jax.ShapeDtypeStruct(tuple(shape), src.dtype))
        elif job["kind"] == "forward":
            alias[len(ins) - 1] = len(outs)
            outs.append(jax.ShapeDtypeStruct(src.shape, src.dtype))
        elif job["kind"] == "scatter":
            if job["buf"] is not None:
                ins.append(job["buf"])
                alias[len(ins) - 1] = len(outs)
            outs.append(jax.ShapeDtypeStruct(job["shape"], src.dtype))
        else:
            if job["buf"] is not None:
                ins.append(job["buf"])
                alias[len(ins) - 1] = len(outs)
            outs.append(jax.ShapeDtypeStruct(src.shape, src.dtype))
    return ins, outs, alias


def _jobs_sems(jobs):
    n = len(jobs)
    return [pltpu.SemaphoreType.DMA((3 * n,)), pltpu.SemaphoreType.DMA((3 * n,)), pltpu.SemaphoreType.DMA((n,))]


def _jobs_copies(jobs, in_refs, out_refs, sems):
    send_sems, recv_sems, local_sems = sems
    x, y, c = _place()
    me = 2 * x + y
    sibling = (x, y, 1 - c)
    triples, ip = [], 0
    for n, (job, out) in enumerate(zip(jobs, out_refs)):
        kind = job["kind"]
        src = in_refs[ip]
        ip += 2 if (kind in ("scatter", "mirror") and job["buf"] is not None) else 1

        def remote(k, src_ref, dst_ref, to):
            return pltpu.make_async_remote_copy(
                src_ref=src_ref, dst_ref=dst_ref, send_sem=send_sems.at[3 * n + k], recv_sem=recv_sems.at[3 * n + k],
                device_id=to, device_id_type=MESH)

        local, sends, recvs = [], [], []
        if kind == "gather":
            axis = job["axis"]
            mine = src.at[job["layer"]]
            local.append(pltpu.make_async_copy(mine, _shard_view(out, axis, me), local_sems.at[n]))
            for k, (px, py) in enumerate(_other_chips(x, y)):
                sends.append(remote(k, _half_view(mine, c), _half_view(_shard_view(out, axis, me), c), (px, py, c)))
                recvs.append(remote(k, _half_view(mine, c), _half_view(_shard_view(out, axis, 2 * px + py), c), (px, py, c)))
        elif kind == "forward":
            axis = job["axis"]
            for k, (px, py) in enumerate(_other_chips(x, y)):
                got = _shard_view(src, axis, 2 * px + py)
                lands = _shard_view(out, axis, 2 * px + py)
                sends.append(remote(k, _half_view(got, c), _half_view(lands, c), sibling))
                recvs.append(remote(k, _half_view(got, c), _half_view(lands, 1 - c), sibling))
        elif kind == "scatter":
            axis, layer = job["axis"], job["layer"]
            local.append(pltpu.make_async_copy(_shard_view(src, axis, me), out.at[3, layer], local_sems.at[n]))
            for k, (px, py) in enumerate(_other_chips(x, y)):
                cp = remote(k, _shard_view(src, axis, 2 * px + py), out.at[k, layer], (px, py, c))
                sends.append(cp)
                recvs.append(cp)
        else:
            cp = remote(0, src.at[:, job["layer"]], out.at[:, job["layer"]], sibling)
            sends.append(cp)
            recvs.append(cp)
        triples.append((local, sends, recvs))
    return triples


def _jobs_start(jobs, in_refs, out_refs, sems):
    for local, sends, _ in _jobs_copies(jobs, in_refs, out_refs, sems):
        for cp in local + sends:
            cp.start()


def _jobs_wait(jobs, in_refs, out_refs, sems):
    triples = _jobs_copies(jobs, in_refs, out_refs, sems)
    for local, sends, _ in triples:
        for cp in local:
            cp.wait()
        for cp in sends:
            cp.wait_send()
    for _, _, recvs in triples:
        for cp in recvs:
            cp.wait_recv()


def _run_jobs(jobs, name):
    ins, outs, alias = _jobs_io(jobs)

    def body(*refs):
        in_refs, out_refs, sems = refs[:len(ins)], refs[len(ins):len(ins) + len(outs)], refs[len(ins) + len(outs):]
        _jobs_start(jobs, in_refs, out_refs, sems)
        _jobs_wait(jobs, in_refs, out_refs, sems)

    return pl.pallas_call(
        body, name=name, in_specs=[HBM_SPEC] * len(ins), out_specs=[HBM_SPEC] * len(outs), out_shape=outs,
        scratch_shapes=_jobs_sems(jobs), input_output_aliases=alias,
    )(*ins)


_DIMS = {
    "nn": (((1,), (0,)), ((), ())),
    "nt": (((1,), (1,)), ((), ())),
    "tn": (((0,), (0,)), ((), ())),
}


def _matmul(a, b, mode, out_dtype, name, *, extras=(), post=None, jobs=(), tm=1024, tn=1024, tk=2048):
    assert a.dtype == BF16 and b.dtype == BF16
    if mode == "nn":
        (m, k), (k2, n) = a.shape, b.shape
    elif mode == "nt":
        (m, k), (n, k2) = a.shape, b.shape
    else:
        (k, m), (k2, n) = a.shape, b.shape
    assert k == k2
    tm, tn, tk = _pick(m, tm), _pick(n, tn), _pick(k, tk)
    nk = k // tk
    grid = (m // tm, n // tn, nk)
    dims = _DIMS[mode]
    out_dtypes = out_dtype if isinstance(out_dtype, (tuple, list)) else (out_dtype,)
    n_main, n_out = 2 + len(extras), len(out_dtypes)

    a_spec = pl.BlockSpec((tk, tm), lambda i, j, kk: (kk, i)) if mode == "tn" else pl.BlockSpec((tm, tk), lambda i, j, kk: (i, kk))
    b_spec = pl.BlockSpec((tn, tk), lambda i, j, kk: (j, kk)) if mode == "nt" else pl.BlockSpec((tk, tn), lambda i, j, kk: (kk, j))
    tile_spec = pl.BlockSpec((tm, tn), lambda i, j, kk: (i, j))
    extras = [e if isinstance(e, tuple) else (e, 0) for e in extras]
    extra_specs = [pl.BlockSpec((tm, tn), functools.partial(lambda i, j, kk, c: (i, j + c), c=c)) for _, c in extras]
    job_ins, job_outs, job_alias = _jobs_io(jobs)
    operands = [a, b, *[e for e, _ in extras], *job_ins]
    in_specs = [a_spec, b_spec] + extra_specs + [HBM_SPEC] * len(job_ins)
    out_shape = [jax.ShapeDtypeStruct((m, n), dt) for dt in out_dtypes] + job_outs
    out_specs = [tile_spec] * n_out + [HBM_SPEC] * len(job_outs)
    aliases = {n_main + i: n_out + o for i, o in job_alias.items()}
    scratch = ([pltpu.VMEM((tm, tn), F32)] if nk > 1 else []) + (_jobs_sems(jobs) if jobs else [])

    def body(*refs):
        a_ref, b_ref = refs[0], refs[1]
        extra_refs = refs[2:n_main]
        job_in_refs = refs[n_main:n_main + len(job_ins)]
        outs = refs[n_main + len(job_ins):]
        o_refs, job_out_refs = outs[:n_out], outs[n_out:n_out + len(job_outs)]
        rest = outs[n_out + len(job_outs):]
        acc_ref = rest[0] if nk > 1 else None
        sems = rest[1:] if nk > 1 else rest
        step = (pl.program_id(0) * grid[1] + pl.program_id(1)) * grid[2] + pl.program_id(2)
        if jobs:
            pl.when(step == 0)(lambda: _jobs_start(jobs, job_in_refs, job_out_refs, sems))

        p = lax.dot_general(a_ref[...], b_ref[...], dims, preferred_element_type=F32)

        def finish(total):
            vals = post(total, *[r[...] for r in extra_refs]) if post is not None else total
            vals = vals if isinstance(vals, (tuple, list)) else (vals,)
            for o_ref, v in zip(o_refs, vals, strict=True):
                o_ref[...] = v.astype(o_ref.dtype)

        if nk == 1:
            finish(p)
        else:
            kk = pl.program_id(2)

            @pl.when(kk == 0)
            def _():
                acc_ref[...] = p

            @pl.when(kk > 0)
            def _():
                acc_ref[...] += p

            @pl.when(kk == nk - 1)
            def _():
                finish(acc_ref[...])

        if jobs:
            pl.when(step == grid[0] * grid[1] * grid[2] - 1)(lambda: _jobs_wait(jobs, job_in_refs, job_out_refs, sems))

    res = pl.pallas_call(
        body,
        name=name,
        grid=grid,
        in_specs=in_specs,
        out_specs=out_specs,
        out_shape=out_shape,
        scratch_shapes=scratch,
        input_output_aliases=aliases,
        compiler_params=_params(("arbitrary", "arbitrary", "arbitrary") if jobs else ("parallel", "parallel", "arbitrary")),
    )(*operands)
    return res[0] if len(res) == 1 else res


def _rowwise(fn, rows, consts, outs, reds, name, *, tr=256, ncol=1, n_rows=None, jobs=()):
    if n_rows is None:
        n_rows = rows[0][0].shape[0]
    tr = min(tr, n_rows)
    assert n_rows % tr == 0
    in_specs, operands = [], []
    for arr, width, coloff, rowoff in rows:
        in_specs.append(pl.BlockSpec((tr, width), functools.partial(lambda j, i, c, r: (i + r, j + c), c=coloff, r=rowoff)))
        operands.append(arr)
    for arr in consts:
        if arr.ndim == 3:
            in_specs.append(pl.BlockSpec(arr.shape, lambda j, i: (0, 0, 0)))
        else:
            in_specs.append(pl.BlockSpec(arr.shape, lambda j, i: (0, 0)))
        operands.append(arr)
    out_specs, out_shape = [], []
    for width, dtype in outs:
        out_specs.append(pl.BlockSpec((tr, width), lambda j, i: (i, j)))
        out_shape.append(jax.ShapeDtypeStruct((n_rows, ncol * width), dtype))
    for shape in reds:
        if len(shape) == 3:
            out_specs.append(pl.BlockSpec(shape, lambda j, i: (0, 0, 0)))
            out_shape.append(jax.ShapeDtypeStruct(shape, F32))
        else:
            out_specs.append(pl.BlockSpec(shape, lambda j, i: (0, j)))
            out_shape.append(jax.ShapeDtypeStruct((shape[0], ncol * shape[1]), F32))
    n_in, n_out, n_red = len(operands), len(outs), len(reds)
    job_ins, job_outs, job_alias = _jobs_io(jobs)
    n_ji, n_jo = len(job_ins), len(job_outs)
    n_steps = ncol * (n_rows // tr)

    def body(*refs):
        job_in_refs = refs[n_in:n_in + n_ji]
        o_refs = refs[n_in + n_ji:n_in + n_ji + n_out]
        red_refs = refs[n_in + n_ji + n_out:n_in + n_ji + n_out + n_red]
        job_out_refs = refs[n_in + n_ji + n_out + n_red:n_in + n_ji + n_out + n_red + n_jo]
        sems = refs[n_in + n_ji + n_out + n_red + n_jo:]
        step = pl.program_id(0) * (n_rows // tr) + pl.program_id(1)
        if jobs:
            pl.when(step == 0)(lambda: _jobs_start(jobs, job_in_refs, job_out_refs, sems))
        vals = fn(*[r[...] for r in refs[:n_in]])
        if not isinstance(vals, (tuple, list)):
            vals = (vals,)
        assert len(vals) == n_out + n_red
        for o_ref, v in zip(o_refs, vals[:n_out]):
            o_ref[...] = v.astype(o_ref.dtype)
        if n_red:
            first = pl.program_id(1) == 0
            for o_ref, v in zip(red_refs, vals[n_out:]):
                @pl.when(first)
                def _(o_ref=o_ref, v=v):
                    o_ref[...] = v

                @pl.when(jnp.logical_not(first))
                def _(o_ref=o_ref, v=v):
                    o_ref[...] += v
        if jobs:
            pl.when(step == n_steps - 1)(lambda: _jobs_wait(jobs, job_in_refs, job_out_refs, sems))

    res = pl.pallas_call(
        body,
        name=name,
        grid=(ncol, n_rows // tr),
        in_specs=in_specs + [HBM_SPEC] * n_ji,
        out_specs=out_specs + [HBM_SPEC] * n_jo,
        out_shape=out_shape + job_outs,
        scratch_shapes=_jobs_sems(jobs) if jobs else [],
        input_output_aliases={n_in + i: n_out + n_red + o for i, o in job_alias.items()},
        compiler_params=_params(("arbitrary", "arbitrary") if jobs else ("parallel", "arbitrary")),
    )(*(operands + job_ins))
    return res


def _row(arr, width=None, coloff=0, rowoff=0):
    return (arr, arr.shape[1] if width is None else width, coloff, rowoff)


def _rms(x, g):
    return x * lax.rsqrt(jnp.mean(x * x, axis=-1, keepdims=True) + EPS) * g


def _erf(x):
    x = jnp.clip(x, -4.0, 4.0)
    x2 = x * x
    alpha = x2 * -2.72614225801306e-10 + 2.77068142495902e-08
    alpha = alpha * x2 - 2.10102402082508e-06
    alpha = alpha * x2 - 5.69250639462346e-05
    alpha = alpha * x2 - 7.34990630326855e-04
    alpha = alpha * x2 - 2.95459980854025e-03
    alpha = alpha * x2 - 1.60960333262415e-02
    beta = x2 * -1.45660718464996e-05 - 2.13374055278905e-04
    beta = beta * x2 - 1.68282697438203e-03
    beta = beta * x2 - 7.37332916720468e-03
    beta = beta * x2 - 1.42647390514189e-02
    return x * alpha / beta


def _gelu(x):
    return 0.5 * x * (1.0 + _erf(x * (2.0 ** -0.5)))


def _gelu_and_grad(x):
    cdf = 0.5 * (1.0 + _erf(x * (2.0 ** -0.5)))
    pdf = jnp.exp(-0.5 * x * x) * (2.0 * jnp.pi) ** -0.5
    return x * cdf, cdf + x * pdf


def _layer_norm(x, g, b):
    mu = jnp.mean(x, axis=-1, keepdims=True)
    xc = x - mu
    return xc * lax.rsqrt(jnp.mean(xc * xc, axis=-1, keepdims=True) + EPS) * g + b


def _per_head(fn, x, g):
    parts = []
    for h in range(HEADS):
        sl = slice(h * HEAD_DIM, (h + 1) * HEAD_DIM)
        parts.append(fn(x[:, sl], g[:, sl]))
    return parts


def _sgu_mask():
    i = lax.broadcasted_iota(jnp.int32, (SGU_LEN, SGU_LEN), 0)
    j = lax.broadcasted_iota(jnp.int32, (SGU_LEN, SGU_LEN), 1)
    return (j // CHUNK) <= (i // CHUNK)


def _sgu_mix(vln_bf, w_bf, bias):
    rows = vln_bf.shape[0]
    out_rows = []
    for c in range(rows // SGU_LEN):
        cols = []
        for g in range(GROUPS):
            blk = vln_bf[c * SGU_LEN:(c + 1) * SGU_LEN, g * GROUP_DIM:(g + 1) * GROUP_DIM]
            mixed = jnp.dot(w_bf[g], blk, preferred_element_type=F32) + bias[g]
            cols.append(mixed)
        out_rows.append(jnp.concatenate(cols, axis=1))
    return out_rows[0] if len(out_rows) == 1 else jnp.concatenate(out_rows, axis=0)


def _split_dots(xs, m_bf):
    his = [x.astype(BF16) for x in xs]
    los = [(x - hi.astype(F32)).astype(BF16) for x, hi in zip(xs, his)]
    res = jnp.dot(jnp.concatenate(his + los, axis=0), m_bf, preferred_element_type=F32)
    n, rows = len(xs), xs[0].shape[0]
    return [res[h * rows:(h + 1) * rows] + res[(n + h) * rows:(n + h + 1) * rows] for h in range(n)]


def _att_tiles(qs, kbs, run_keeps, diag, want_sig=False):
    blk = ATT_BLOCK
    row = lax.broadcasted_iota(jnp.int32, (blk, blk), 0)
    col = lax.broadcasted_iota(jnp.int32, (blk, blk), 1)
    later = (row > col).astype(BF16)
    past = col < row
    zs = [lax.dot_general(q, kb, _DIMS["nt"], preferred_element_type=F32) * (HEAD_DIM ** -0.5) for q, kb in zip(qs, kbs)]
    es = [jnp.exp(-jnp.abs(z)) for z in zs]
    softplus = [jnp.maximum(z, 0.0) + jnp.log(1.0 + e) for z, e in zip(zs, es)]
    log_keeps = [jnp.where(past, -sp, 0.0) if diag else -sp for sp in softplus]
    tails = _split_dots(log_keeps, later)
    weights = [jnp.exp((z - sp) + (tail + keep)) for z, sp, tail, keep in zip(zs, softplus, tails, run_keeps)]
    if diag:
        weights = [jnp.where(past, a, 0.0) for a in weights]
    sigs = None
    if want_sig:
        invs = [1.0 / (1.0 + e) for e in es]
        sigs = [jnp.where(z >= 0.0, inv, e * inv) for z, e, inv in zip(zs, es, invs)]
    return weights, sigs, [jnp.sum(lk, axis=1, keepdims=True) for lk in log_keeps]


def _any_alive(keeps):
    return jnp.max(functools.reduce(jnp.maximum, keeps)) >= LOG_ZERO


def _attention_fwd(qn, kn, vb, *, heads_per_step=4, jobs=()):
    t = qn.shape[0]
    blk = ATT_BLOCK
    nq = t // blk
    hp = heads_per_step
    groups, width = HEADS // hp, hp * HEAD_DIM
    heads = [slice(h * HEAD_DIM, (h + 1) * HEAD_DIM) for h in range(hp)]
    job_ins, job_outs, job_alias = _jobs_io(jobs)
    n_ji, n_jo = len(job_ins), len(job_outs)

    def body(q_ref, k_ref, v_ref, *rest):
        job_in_refs, o_ref, o32_ref = rest[:n_ji], rest[n_ji], rest[n_ji + 1]
        job_out_refs, sems = rest[n_ji + 2:n_ji + 2 + n_jo], rest[n_ji + 2 + n_jo:]
        i = pl.program_id(1)
        step = pl.program_id(0) * nq + i
        if jobs:
            pl.when(step == 0)(lambda: _jobs_start(jobs, job_in_refs, job_out_refs, sems))
        qs = [q_ref[:, sl] for sl in heads]

        def tile(j, keeps, accs, diag):
            rows = pl.ds(pl.multiple_of(j * blk, blk), blk)
            weights, _, keep_sums = _att_tiles(qs, [k_ref[rows, sl] for sl in heads], keeps, diag)
            accs = [acc + jnp.dot(a.astype(BF16), v_ref[rows, sl], preferred_element_type=F32)
                    for acc, a, sl in zip(accs, weights, heads)]
            return tuple(k + s for k, s in zip(keeps, keep_sums)), tuple(accs)

        zero = tuple(jnp.zeros((blk, 1), F32) for _ in heads)
        keeps, accs = tile(i, zero, tuple(jnp.zeros((blk, HEAD_DIM), F32) for _ in heads), True)

        def cond(c):
            return jnp.logical_and(c[0] >= 0, _any_alive(c[1]))

        def step_fn(c):
            j, keeps, accs = c
            keeps, accs = tile(j, keeps, accs, False)
            return j - 1, keeps, accs

        _, _, accs = lax.while_loop(cond, step_fn, (i - 1, keeps, accs))
        for h, sl in enumerate(heads):
            o_ref[:, sl] = accs[h].astype(o_ref.dtype)
            o32_ref[:, sl] = accs[h]
        if jobs:
            pl.when(step == groups * nq - 1)(lambda: _jobs_wait(jobs, job_in_refs, job_out_refs, sems))

    blk_spec = pl.BlockSpec((blk, width), lambda g, i: (i, g))
    head_spec = pl.BlockSpec((t, width), lambda g, i: (0, g))
    return pl.pallas_call(
        body,
        name="attention_fwd",
        grid=(groups, nq),
        in_specs=[blk_spec, head_spec, head_spec] + [HBM_SPEC] * n_ji,
        out_specs=[blk_spec, blk_spec] + [HBM_SPEC] * n_jo,
        out_shape=[jax.ShapeDtypeStruct((t, SB_WIDTH), BF16), jax.ShapeDtypeStruct((t, SB_WIDTH), F32)] + job_outs,
        scratch_shapes=_jobs_sems(jobs) if jobs else [],
        input_output_aliases={3 + i: 2 + o for i, o in job_alias.items()},
        compiler_params=_params(("arbitrary", "arbitrary")),
    )(qn, kn, vb, *job_ins)


def _attention_bwd(qn, kn, vb, do, o32, *, heads_per_step=4, jobs=()):
    t = qn.shape[0]
    blk = ATT_BLOCK
    nq = t // blk
    hp = heads_per_step
    groups, width = HEADS // hp, hp * HEAD_DIM
    heads = [slice(h * HEAD_DIM, (h + 1) * HEAD_DIM) for h in range(hp)]
    job_ins, job_outs, job_alias = _jobs_io(jobs)
    n_ji, n_jo = len(job_ins), len(job_outs)

    def body(q_ref, k_ref, v_ref, do_ref, o32_ref, *rest):
        job_in_refs = rest[:n_ji]
        dq_ref, dk_ref, dv_ref = rest[n_ji:n_ji + 3]
        job_out_refs, sems = rest[n_ji + 3:n_ji + 3 + n_jo], rest[n_ji + 3 + n_jo:]
        i = pl.program_id(1)
        step = pl.program_id(0) * nq + i
        if jobs:
            pl.when(step == 0)(lambda: _jobs_start(jobs, job_in_refs, job_out_refs, sems))

        @pl.when(i == 0)
        def _():
            dk_ref[...] = jnp.zeros_like(dk_ref)
            dv_ref[...] = jnp.zeros_like(dv_ref)

        qs = [q_ref[:, sl] for sl in heads]
        dos = [do_ref[:, sl] for sl in heads]
        totals = [jnp.sum(d.astype(F32) * o32_ref[:, sl], axis=1, keepdims=True) for d, sl in zip(dos, heads)]
        row = lax.broadcasted_iota(jnp.int32, (blk, blk), 0)
        col = lax.broadcasted_iota(jnp.int32, (blk, blk), 1)
        from_here = (row >= col).astype(BF16)

        def tile(j, keeps, run_dlas, dqs, diag):
            rows = pl.ds(pl.multiple_of(j * blk, blk), blk)
            kbs = [k_ref[rows, sl] for sl in heads]
            weights, sigs, keep_sums = _att_tiles(qs, kbs, keeps, diag, want_sig=True)
            wbs = [a.astype(BF16) for a in weights]
            dps = [lax.dot_general(d, v_ref[rows, sl], _DIMS["nt"], preferred_element_type=F32) for d, sl in zip(dos, heads)]
            dlas = [wb.astype(F32) * dp for wb, dp in zip(wbs, dps)]
            later_sums = _split_dots(dlas, from_here)
            dzbs = []
            for dla, sig, total, later, run in zip(dlas, sigs, totals, later_sums, run_dlas):
                dz = dla * (1.0 - sig) - sig * (total - (later + run))
                if diag:
                    dz = jnp.where(col < row, dz, 0.0)
                dzbs.append((dz * (HEAD_DIM ** -0.5)).astype(BF16))
            dqs = [dq + jnp.dot(dzb, kb, preferred_element_type=F32) for dq, dzb, kb in zip(dqs, dzbs, kbs)]
            for sl, dzb, wb, q, d in zip(heads, dzbs, wbs, qs, dos):
                dk_ref[rows, sl] += lax.dot_general(dzb, q, _DIMS["tn"], preferred_element_type=F32)
                dv_ref[rows, sl] += lax.dot_general(wb, d, _DIMS["tn"], preferred_element_type=F32)
            keeps = tuple(k + s for k, s in zip(keeps, keep_sums))
            run_dlas = tuple(r + jnp.sum(dla, axis=1, keepdims=True) for r, dla in zip(run_dlas, dlas))
            return keeps, run_dlas, tuple(dqs)

        zero = tuple(jnp.zeros((blk, 1), F32) for _ in heads)
        keeps, run_dlas, dqs = tile(i, zero, zero, tuple(jnp.zeros((blk, HEAD_DIM), F32) for _ in heads), True)

        def cond(c):
            return jnp.logical_and(c[0] >= 0, _any_alive(c[1]))

        def step_fn(c):
            j, keeps, run_dlas, dqs = c
            keeps, run_dlas, dqs = tile(j, keeps, run_dlas, dqs, False)
            return j - 1, keeps, run_dlas, dqs

        _, _, _, dqs = lax.while_loop(cond, step_fn, (i - 1, keeps, run_dlas, dqs))
        for h, sl in enumerate(heads):
            dq_ref[:, sl] = dqs[h]
        if jobs:
            pl.when(step == groups * nq - 1)(lambda: _jobs_wait(jobs, job_in_refs, job_out_refs, sems))

    blk_spec = pl.BlockSpec((blk, width), lambda g, i: (i, g))
    head_spec = pl.BlockSpec((t, width), lambda g, i: (0, g), pipeline_mode=pl.Buffered(1))
    full = jax.ShapeDtypeStruct((t, SB_WIDTH), F32)
    return pl.pallas_call(
        body,
        name="attention_bwd",
        grid=(groups, nq),
        in_specs=[blk_spec, head_spec, head_spec, blk_spec, blk_spec] + [HBM_SPEC] * n_ji,
        out_specs=[blk_spec, head_spec, head_spec] + [HBM_SPEC] * n_jo,
        out_shape=[full, full, full] + job_outs,
        scratch_shapes=_jobs_sems(jobs) if jobs else [],
        input_output_aliases={5 + i: 3 + o for i, o in job_alias.items()},
        compiler_params=_params(("arbitrary", "arbitrary"), ATT_BWD_VMEM_LIMIT),
    )(qn, kn, vb, do, o32, *job_ins)


def _sum_over_devices(v, jobs=()):
    rows, width = v.shape
    job_ins, job_outs, job_alias = _jobs_io(jobs)
    n_ji, n_jo = len(job_ins), len(job_outs)

    def body(v_ref, *rest):
        job_in_refs, o_ref = rest[:n_ji], rest[n_ji]
        job_out_refs = rest[n_ji + 1:n_ji + 1 + n_jo]
        buf, send_sems, recv_sems = rest[n_ji + 1 + n_jo:n_ji + 4 + n_jo]
        job_sems = rest[n_ji + 4 + n_jo:]
        if jobs:
            _jobs_start(jobs, job_in_refs, job_out_refs, job_sems)
        x, y, c = _place()
        me = 4 * x + 2 * y + c
        buf[me] = v_ref[...]
        sends = []
        for d in range(1, N_DEV):
            px, py, pc = x ^ (d >> 2), y ^ ((d >> 1) & 1), c ^ (d & 1)
            cp = pltpu.make_async_remote_copy(
                src_ref=v_ref, dst_ref=buf.at[me], send_sem=send_sems.at[d - 1], recv_sem=recv_sems.at[d - 1],
                device_id=(px, py, pc), device_id_type=MESH)
            cp.start()
            sends.append((cp, 4 * px + 2 * py + pc))
        for cp, _ in sends:
            cp.wait_send()
        for d, (cp, peer) in enumerate(sends):
            pltpu.make_async_remote_copy(
                src_ref=v_ref, dst_ref=buf.at[peer], send_sem=send_sems.at[d], recv_sem=recv_sems.at[d],
                device_id=(x, y, c), device_id_type=MESH).wait_recv()
        total = buf[0]
        for d in range(1, N_DEV):
            total = total + buf[d]
        o_ref[...] = total
        if jobs:
            _jobs_wait(jobs, job_in_refs, job_out_refs, job_sems)

    vmem = pl.BlockSpec(memory_space=pltpu.VMEM)
    return pl.pallas_call(
        body,
        name="sum_over_devices",
        in_specs=[vmem] + [HBM_SPEC] * n_ji,
        out_specs=[vmem] + [HBM_SPEC] * n_jo,
        out_shape=[jax.ShapeDtypeStruct((rows, width), F32)] + job_outs,
        scratch_shapes=[pltpu.VMEM((N_DEV, rows, width), F32), pltpu.SemaphoreType.DMA((N_DEV - 1,)), pltpu.SemaphoreType.DMA((N_DEV - 1,))]
        + (_jobs_sems(jobs) if jobs else []),
        input_output_aliases={1 + i: 1 + o for i, o in job_alias.items()},
        compiler_params=pltpu.CompilerParams(vmem_limit_bytes=V7X_VMEM_LIMIT),
    )(v, *job_ins)


def _adamw_math(w, g, m, v):
    m = ADAM_B1 * m + (1.0 - ADAM_B1) * g
    v = ADAM_B2 * v + (1.0 - ADAM_B2) * (g * g)
    m_hat = m / (1.0 - ADAM_B1 ** ADAM_STEP)
    v_hat = v / (1.0 - ADAM_B2 ** ADAM_STEP)
    delta = -ADAM_LR * (m_hat / (jnp.sqrt(v_hat) + ADAM_EPS) + ADAM_WD * w)
    return delta, m, v


def _adamw_sharded(mine, theirs, w, m, v, name):
    width = _pick(w.shape[1], 1152)
    ncol = w.shape[1] // width
    shard_rows = w.shape[0]
    tr = min(256, shard_rows)

    def slots(arr):
        return [_row(arr, width, 0, s * (shard_rows // tr)) for s in range(4)]

    def total(a, b, c, d):
        return ((a.astype(F32) + b.astype(F32)) + c.astype(F32)) + d.astype(F32)

    def fn(a0, a1, a2, a3, b0, b1, b2, b3, w, m, v):
        g = total(a0, a1, a2, a3) + total(b0, b1, b2, b3)
        delta, m, v = _adamw_math(w, g, m, v)
        return g, delta, m, v

    return _rowwise(fn, slots(mine) + slots(theirs) + [_row(t, width) for t in (w, m, v)], [], [(width, F32)] * 4, [], name,
                    tr=tr, ncol=ncol, n_rows=shard_rows)


def _adamw_small(g, w, m, v):
    def fn(g, w, m, v):
        return _adamw_math(w, g, m, v)

    return _rowwise(fn, [_row(t) for t in (g, w, m, v)], [], [(g.shape[1], F32)] * 3, [], "adamw_small", tr=g.shape[0])


def _flat(a):
    return a.reshape(-1, a.shape[-1])


def kernel(x, g_mix, w_in, g_q, g_k, sgu_ln_g, sgu_ln_b, w_spatial, b_spatial, w_oa, w_ob, w_out, g_ff, w_ff1, w_ff2, loss_target, m_g_mix, m_w_in, m_g_q, m_g_k, m_sgu_ln_g, m_sgu_ln_b, m_w_spatial, m_b_spatial, m_w_oa, m_w_ob, m_w_out, m_g_ff, m_w_ff1, m_w_ff2, v_g_mix, v_w_in, v_g_q, v_g_k, v_sgu_ln_g, v_sgu_ln_b, v_w_spatial, v_b_spatial, v_w_oa, v_w_ob, v_w_out, v_g_ff, v_w_ff1, v_w_ff2):
    depth = g_mix.shape[0]
    seq, d_model = x.shape[1], x.shape[2]
    d_ff = w_ff1.shape[2] * N_CHIPS
    in_cols = w_in.shape[2] * N_CHIPS
    col_gate_a = 3 * SB_WIDTH + 2 * SGU_WIDTH
    assert in_cols == col_gate_a + 2 * d_model

    big = [w_in, w_oa, w_ob, w_out, w_ff1, w_ff2]
    big_m = [m_w_in, m_w_oa, m_w_ob, m_w_out, m_w_ff1, m_w_ff2]
    big_v = [v_w_in, v_w_oa, v_w_ob, v_w_out, v_w_ff1, v_w_ff2]
    axes = [2, 2, 2, 1, 2, 1]
    W_IN, W_OA, W_OB, W_OUT, W_FF1, W_FF2 = range(6)
    shards = [w.astype(BF16) for w in big]
    full = [[None] * depth for _ in big]

    def gather(t, l):
        return _gather_job(shards[t], l, axes[t])

    def forward(t, l):
        return _forward_job(full[t][l], axes[t])

    def add_residual(total, res):
        return total + res

    (full[W_IN][0],) = _run_jobs([gather(W_IN, 0)], "gather_first")
    (full[W_IN][0],) = _run_jobs([forward(W_IN, 0)], "forward_first")

    gate_w = _pick(d_model, 1024)
    bias_col = b_spatial[..., None]

    def rms_fwd(xin, g, name):
        return _rowwise(lambda xb, gb: _rms(xb, gb), [_row(xin)], [g], [(d_model, BF16)], [], name)[0]

    saved = []
    cur = x.reshape(seq, d_model)
    for l in range(depth):
        hb = rms_fwd(cur, g_mix[l:l + 1], "rms_mix")
        early = [W_FF1, W_OA, W_OB, W_OUT]
        proj, *got = _matmul(hb, full[W_IN][l], "nn", F32, "proj", jobs=[gather(t, l) for t in early])
        for t, w in zip(early, got):
            full[t][l] = w

        def qk_fn(q, k, v, gq, gk):
            qn = jnp.concatenate(_per_head(_rms, q, gq), axis=1)
            kn = jnp.concatenate(_per_head(_rms, k, gk), axis=1)
            return qn, kn, v

        gq, gk = g_q[l].reshape(1, SB_WIDTH), g_k[l].reshape(1, SB_WIDTH)
        qn, kn, vb = _rowwise(qk_fn, [_row(proj, SB_WIDTH, 0), _row(proj, SB_WIDTH, 1), _row(proj, SB_WIDTH, 2)],
                              [gq, gk], [(SB_WIDTH, BF16)] * 3, [], "qk_norm")
        o, o32, *got = _attention_fwd(qn, kn, vb, jobs=[forward(t, l) for t in early] + [gather(W_FF2, l)])
        for t, w in zip(early + [W_FF2], got):
            full[t][l] = w
        ya = _matmul(o, full[W_OA][l], "nn", F32, "proj_a")

        def sgu_fn(u_pre, v_pre, ln_g, ln_b, w_s, b_t):
            w_bf = jnp.where(_sgu_mask()[None], w_s, 0.0).astype(BF16)
            vln = _layer_norm(_gelu(v_pre), ln_g, ln_b)
            return _gelu(u_pre) * _sgu_mix(vln.astype(BF16), w_bf, b_t)

        sgu_consts = [sgu_ln_g[l:l + 1], sgu_ln_b[l:l + 1], w_spatial[l], bias_col[l]]
        s = _rowwise(sgu_fn, [_row(proj, SGU_WIDTH, 3), _row(proj, SGU_WIDTH, 4)], sgu_consts, [(SGU_WIDTH, BF16)], [], "sgu_fwd")[0]
        def merge_fn(b, ga, gb, a):
            return b, jax.nn.sigmoid(ga) * a + jax.nn.sigmoid(gb) * b

        gates = [(proj, col_gate_a // gate_w), (proj, (col_gate_a + d_model) // gate_w)]
        yb, merged = _matmul(s, full[W_OB][l], "nn", (F32, BF16), "proj_b", extras=gates + [ya], post=merge_fn, tm=512, tn=gate_w)
        x1 = _matmul(merged, full[W_OUT][l], "nn", F32, "proj_out", extras=[cur], post=add_residual)
        h2 = rms_fwd(x1, g_ff[l:l + 1], "rms_ff")
        more = l + 1 < depth
        a1, r, full[W_FF2][l], *got = _matmul(h2, full[W_FF1][l], "nn", (F32, BF16), "ff1",
                                              jobs=[forward(W_FF2, l)] + ([gather(W_IN, l + 1)] if more else []),
                                              post=lambda total: (total, jnp.square(jnp.maximum(total, 0.0))))
        if more:
            full[W_IN][l + 1] = got[0]
            x2, full[W_IN][l + 1] = _matmul(r, full[W_FF2][l], "nn", F32, "ff2", extras=[x1], post=add_residual,
                                            jobs=[forward(W_IN, l + 1)])
        else:
            x2 = _matmul(r, full[W_FF2][l], "nn", F32, "ff2", extras=[x1], post=add_residual)
        saved.append(dict(x=cur, hb=hb, proj=proj, qn=qn, kn=kn, vb=vb, o=o, o32=o32, ya=ya, yb=yb, s=s, merged=merged,
                          x1=x1, h2=h2, a1=a1, r=r, gq=gq, gk=gk, sgu_consts=sgu_consts, gates=gates))
        cur = x2

    def loss_fn(y, target):
        err = y - target
        per_row = jnp.sum(err * err, axis=-1, keepdims=True) * (1.0 / d_model)
        part = 0.5 * jnp.sum(per_row, axis=0, keepdims=True)
        dy = err * (1.0 / d_model)
        return dy, dy, jnp.broadcast_to(part, (8, LANE))

    dx, dxb, loss_part = _rowwise(loss_fn, [_row(cur), _row(loss_target.reshape(seq, d_model))], [],
                                  [(d_model, F32), (d_model, BF16)], [(8, LANE)], "loss")
    loss = lax.psum(loss_part[0, 0], ("x", "y", "c"))

    landed = [None] * len(big)

    theirs = [None] * len(big)

    def scatter(t, grad, l):
        return _scatter_job(grad, landed[t], (4,) + shards[t].shape, l, axes[t])

    def mirror(t, l):
        return _mirror_job(landed[t], theirs[t], l)

    small = {n: [None] * depth for n in ("g_mix", "g_q", "g_k", "ln_g", "ln_b", "w_s", "b_s", "g_ff")}

    def rms_bwd(dh, xin, dres, g, name):
        def fn(dh, xin, dres, g):
            _, vjp = jax.vjp(_rms, xin, g)
            dxin, dg = vjp(dh)
            total = dres + dxin
            return total, total, dg

        return _rowwise(fn, [_row(dh), _row(xin), _row(dres)], [g], [(d_model, F32), (d_model, BF16)], [(1, d_model)], name)

    for l in reversed(range(depth)):
        sv = saved[l]
        da1 = _matmul(dxb, full[W_FF2][l], "nt", BF16, "ff2_dx", extras=[sv["a1"]],
                      post=lambda total, a: total * (2.0 * jnp.maximum(a, 0.0)),
                      jobs=[mirror(t, l + 1) for t in range(len(big))] if l + 1 < depth else [])
        if l + 1 < depth:
            da1, *theirs = da1
        grad_ff2 = _matmul(sv["r"], dxb, "tn", BF16, "ff2_dw", tk=WGRAD_TK)
        dh2 = _matmul(da1, full[W_FF1][l], "nt", F32, "ff1_dx")
        grad_ff1 = _matmul(sv["h2"], da1, "tn", BF16, "ff1_dw", tk=WGRAD_TK)
        dx1, dx1b, small["g_ff"][l] = rms_bwd(dh2, sv["x1"], dx, g_ff[l:l + 1], "rms_ff_bwd")

        def merge_bwd_fn(dm, ga, gb, a, b):
            sa, sb = jax.nn.sigmoid(ga), jax.nn.sigmoid(gb)
            return dm * a * sa * (1.0 - sa), dm * b * sb * (1.0 - sb), dm * sa, dm * sb

        dga, dgb, dya, dyb = _matmul(dx1b, full[W_OUT][l], "nt", (BF16,) * 4, "out_dx", tm=512, tn=gate_w,
                                     extras=sv["gates"] + [sv["ya"], sv["yb"]], post=merge_bwd_fn)
        grad_out = _matmul(sv["merged"], dx1b, "tn", BF16, "out_dw", tk=WGRAD_TK)

        ds = _matmul(dyb, full[W_OB][l], "nt", F32, "ob_dx")
        grad_ob = _matmul(sv["s"], dyb, "tn", BF16, "ob_dw", tk=WGRAD_TK)

        def sgu_bwd_fn(ds, u_pre, v_pre, ln_g, ln_b, w_s, b_t):
            mask = _sgu_mask()
            w_bf = jnp.where(mask[None], w_s, 0.0).astype(BF16)
            u, u_grad = _gelu_and_grad(u_pre)
            vg, vg_grad = _gelu_and_grad(v_pre)
            vln, ln_vjp = jax.vjp(_layer_norm, vg, ln_g, ln_b)
            vln_bf = vln.astype(BF16)
            mixed = _sgu_mix(vln_bf, w_bf, b_t)
            du_pre = ds * mixed * u_grad
            dmixed = ds * u
            dm_bf = dmixed.astype(BF16)
            dw = [jnp.zeros((SGU_LEN, SGU_LEN), F32) for _ in range(GROUPS)]
            db = [jnp.zeros((SGU_LEN, 1), F32) for _ in range(GROUPS)]
            dvln_rows = []
            for c in range(ds.shape[0] // SGU_LEN):
                rows = slice(c * SGU_LEN, (c + 1) * SGU_LEN)
                cols = []
                for g in range(GROUPS):
                    sl = slice(g * GROUP_DIM, (g + 1) * GROUP_DIM)
                    cols.append(lax.dot_general(w_bf[g], dm_bf[rows, sl], _DIMS["tn"], preferred_element_type=F32))
                    dw[g] = dw[g] + lax.dot_general(dm_bf[rows, sl], vln_bf[rows, sl], _DIMS["nt"], preferred_element_type=F32)
                    db[g] = db[g] + jnp.sum(dmixed[rows, sl], axis=1, keepdims=True)
                dvln_rows.append(jnp.concatenate(cols, axis=1))
            dvln = dvln_rows[0] if len(dvln_rows) == 1 else jnp.concatenate(dvln_rows, axis=0)
            dvg, dln_g, dln_b = ln_vjp(dvln)
            dv_pre = dvg * vg_grad
            dw_s = jnp.stack([jnp.where(mask, d, 0.0) for d in dw])
            return du_pre, dv_pre, dln_g, dln_b, dw_s, jnp.stack(db)

        du, dvs, small["ln_g"][l], small["ln_b"][l], small["w_s"][l], db_col = _rowwise(
            sgu_bwd_fn, [_row(ds), _row(sv["proj"], SGU_WIDTH, 3), _row(sv["proj"], SGU_WIDTH, 4)], sv["sgu_consts"],
            [(SGU_WIDTH, BF16)] * 2, [(1, SGU_WIDTH), (1, SGU_WIDTH), (GROUPS, SGU_LEN, SGU_LEN), (GROUPS, SGU_LEN, 1)], "sgu_bwd", tr=128)
        small["b_s"][l] = db_col[..., 0]

        do = _matmul(dya, full[W_OA][l], "nt", BF16, "oa_dx")
        grad_oa = _matmul(sv["o"], dya, "tn", BF16, "oa_dw", tk=WGRAD_TK)
        dqn, dkn, dv, landed[W_FF2], landed[W_OUT], landed[W_OB], landed[W_OA] = _attention_bwd(
            sv["qn"], sv["kn"], sv["vb"], do, sv["o32"],
            jobs=[scatter(W_FF2, grad_ff2, l), scatter(W_OUT, grad_out, l), scatter(W_OB, grad_ob, l), scatter(W_OA, grad_oa, l)])

        def qk_bwd_fn(dqn, dkn, dv, q, k, gq, gk):
            outs = []
            for d, xin, g in ((dqn, q, gq), (dkn, k, gk)):
                dxs, dgs = [], []
                for h in range(HEADS):
                    sl = slice(h * HEAD_DIM, (h + 1) * HEAD_DIM)
                    _, vjp = jax.vjp(_rms, xin[:, sl], g[:, sl])
                    dxh, dgh = vjp(d[:, sl])
                    dxs.append(dxh)
                    dgs.append(dgh)
                outs.append((jnp.concatenate(dxs, axis=1), jnp.concatenate(dgs, axis=1)))
            return outs[0][0], outs[1][0], dv, outs[0][1], outs[1][1]

        dq, dk, dvb, small["g_q"][l], small["g_k"][l] = _rowwise(
            qk_bwd_fn, [_row(dqn), _row(dkn), _row(dv), _row(sv["proj"], SB_WIDTH, 0), _row(sv["proj"], SB_WIDTH, 1)],
            [sv["gq"], sv["gk"]], [(SB_WIDTH, BF16)] * 3, [(1, SB_WIDTH), (1, SB_WIDTH)], "qk_norm_bwd")

        dproj = jnp.concatenate([dq, dk, dvb, du, dvs, dga, dgb], axis=1)
        grad_in, landed[W_FF1] = _matmul(sv["hb"], dproj, "tn", BF16, "in_dw", tk=WGRAD_TK, jobs=[scatter(W_FF1, grad_ff1, l)])
        dh, landed[W_IN] = _matmul(dproj, full[W_IN][l], "nt", F32, "in_dx", tk=3072, jobs=[scatter(W_IN, grad_in, l)])
        dx, dxb, small["g_mix"][l] = rms_bwd(dh, sv["x"], dx1, g_mix[l:l + 1], "rms_mix_bwd")

    names = ["g_mix", "g_q", "g_k", "ln_g", "ln_b", "w_s", "b_s", "g_ff"]
    small_w = [g_mix, g_q, g_k, sgu_ln_g, sgu_ln_b, w_spatial, b_spatial, g_ff]
    small_m = [m_g_mix, m_g_q, m_g_k, m_sgu_ln_g, m_sgu_ln_b, m_w_spatial, m_b_spatial, m_g_ff]
    small_v = [v_g_mix, v_g_q, v_g_k, v_sgu_ln_g, v_sgu_ln_b, v_w_spatial, v_b_spatial, v_g_ff]

    def pack(parts):
        return jnp.concatenate([p.reshape(-1, LANE) for p in parts], axis=0)

    local_small = pack([jnp.stack(small[n]) for n in names])
    g_small, *theirs = _sum_over_devices(local_small, [mirror(t, 0) for t in range(len(big))])
    d_small, m_small, v_small = _adamw_small(g_small, pack(small_w), pack(small_m), pack(small_v))

    big_out = []
    for t in range(len(big)):
        res = _adamw_sharded(_flat(landed[t]), _flat(theirs[t]), _flat(big[t]), _flat(big_m[t]), _flat(big_v[t]), "adamw")
        big_out.append([r.reshape(big[t].shape) for r in res])

    def unpack(packed):
        outs, row = [], 0
        for w in small_w:
            n = w.size // LANE
            outs.append(packed[row:row + n].reshape(w.shape))
            row += n
        return outs

    small_out = [unpack(p) for p in (g_small, d_small, m_small, v_small)]

    order = [("s", 0), ("b", W_IN), ("s", 1), ("s", 2), ("s", 3), ("s", 4), ("s", 5), ("s", 6),
             ("b", W_OA), ("b", W_OB), ("b", W_OUT), ("s", 7), ("b", W_FF1), ("b", W_FF2)]
    result = [loss, dx.reshape(x.shape)]
    for kind in range(4):
        for which, idx in order:
            result.append(small_out[kind][idx] if which == "s" else big_out[idx][kind])
    return tuple(result)
```

```python
import functools

import jax
import jax.numpy as jnp
from jax import lax
from jax.experimental import pallas as pl
from jax.experimental.pallas import tpu as pltpu

F32 = jnp.float32
BF16 = jnp.bfloat16
MESH = pl.DeviceIdType.MESH

EPS = 1e-6
HEADS = 8
HEAD_DIM = 128
SB_WIDTH = HEADS * HEAD_DIM
GROUPS = 8
GROUP_DIM = 128
SGU_WIDTH = GROUPS * GROUP_DIM
SGU_LEN = 128
CHUNK = 64
ATT_BLOCK = 128
LOG_ZERO = -104.0

ADAM_LR = 0.001
ADAM_B1 = 0.9
ADAM_B2 = 0.999
ADAM_EPS = 1e-08
ADAM_WD = 0.01
ADAM_STEP = 10

N_CHIPS = 4
N_DEV = 8
V7X_VMEM_LIMIT = 48 * 1024 * 1024
ATT_BWD_VMEM_LIMIT = 56 * 1024 * 1024
WGRAD_TK = 2048
LANE = 128


def _params(sem, vmem_limit=V7X_VMEM_LIMIT):
    return pltpu.CompilerParams(dimension_semantics=sem, vmem_limit_bytes=vmem_limit)


def _pick(dim, pref):
    if dim <= pref:
        return dim
    for t in range(pref - pref % LANE, 0, -LANE):
        if dim % t == 0:
            return t
    raise ValueError(f"no tile for {dim}")


HBM_SPEC = pl.BlockSpec(memory_space=pl.ANY)


def _place():
    return lax.axis_index("x"), lax.axis_index("y"), lax.axis_index("c")


def _other_chips(x, y):
    return [(1 - x, y), (x, 1 - y), (1 - x, 1 - y)]


def _shard_view(ref, axis, index):
    size = ref.shape[axis] // N_CHIPS
    start = pl.multiple_of(index * size, size)
    if axis == 0:
        return ref.at[pl.ds(start, size), :]
    return ref.at[:, pl.ds(start, size)]


def _half_view(view, half):
    rows = view.shape[0] // 2
    return view.at[pl.ds(pl.multiple_of(half * rows, 8), rows), :]


def _gather_job(shards, layer, axis):
    return dict(kind="gather", src=shards, layer=layer, axis=axis - 1)


def _forward_job(weight, axis):
    return dict(kind="forward", src=weight, axis=axis - 1)


def _scatter_job(grad, landed, shape, layer, axis):
    return dict(kind="scatter", src=grad, buf=landed, shape=shape, layer=layer, axis=axis - 1)


def _mirror_job(landed, theirs, layer):
    return dict(kind="mirror", src=landed, buf=theirs, layer=layer)


def _jobs_io(jobs):
    ins, outs, alias = [], [], {}
    for job in jobs:
        src = job["src"]
        ins.append(src)
        if job["kind"] == "gather":
            shape = list(src.shape[1:])
            shape[job["axis"]] *= N_CHIPS
            outs.append(jax.ShapeDtypeStruct(tuple(shape), src.dtype))
        elif job["kind"] == "forward":
            alias[len(ins) - 1] = len(outs)
            outs.append(jax.ShapeDtypeStruct(src.shape, src.dtype))
        elif job["kind"] == "scatter":
            if job["buf"] is not None:
                ins.append(job["buf"])
                alias[len(ins) - 1] = len(outs)
            outs.append(jax.ShapeDtypeStruct(job["shape"], src.dtype))
        else:
            if job["buf"] is not None:
                ins.append(job["buf"])
                alias[len(ins) - 1] = len(outs)
            outs.append(jax.ShapeDtypeStruct(src.shape, src.dtype))
    return ins, outs, alias


def _jobs_sems(jobs):
    n = len(jobs)
    return [pltpu.SemaphoreType.DMA((3 * n,)), pltpu.SemaphoreType.DMA((3 * n,)), pltpu.SemaphoreType.DMA((n,))]


def _jobs_copies(jobs, in_refs, out_refs, sems):
    send_sems, recv_sems, local_sems = sems
    x, y, c = _place()
    me = 2 * x + y
    sibling = (x, y, 1 - c)
    triples, ip = [], 0
    for n, (job, out) in enumerate(zip(jobs, out_refs)):
        kind = job["kind"]
        src = in_refs[ip]
        ip += 2 if (kind in ("scatter", "mirror") and job["buf"] is not None) else 1

        def remote(k, src_ref, dst_ref, to):
            return pltpu.make_async_remote_copy(
                src_ref=src_ref, dst_ref=dst_ref, send_sem=send_sems.at[3 * n + k], recv_sem=recv_sems.at[3 * n + k],
                device_id=to, device_id_type=MESH)

        local, sends, recvs = [], [], []
        if kind == "gather":
            axis = job["axis"]
            mine = src.at[job["layer"]]
            local.append(pltpu.make_async_copy(mine, _shard_view(out, axis, me), local_sems.at[n]))
            for k, (px, py) in enumerate(_other_chips(x, y)):
                sends.append(remote(k, _half_view(mine, c), _half_view(_shard_view(out, axis, me), c), (px, py, c)))
                recvs.append(remote(k, _half_view(mine, c), _half_view(_shard_view(out, axis, 2 * px + py), c), (px, py, c)))
        elif kind == "forward":
            axis = job["axis"]
            for k, (px, py) in enumerate(_other_chips(x, y)):
                got = _shard_view(src, axis, 2 * px + py)
                lands = _shard_view(out, axis, 2 * px + py)
                sends.append(remote(k, _half_view(got, c), _half_view(lands, c), sibling))
                recvs.append(remote(k, _half_view(got, c), _half_view(lands, 1 - c), sibling))
        elif kind == "scatter":
            axis, layer = job["axis"], job["layer"]
            local.append(pltpu.make_async_copy(_shard_view(src, axis, me), out.at[3, layer], local_sems.at[n]))
            for k, (px, py) in enumerate(_other_chips(x, y)):
                cp = remote(k, _shard_view(src, axis, 2 * px + py), out.at[k, layer], (px, py, c))
                sends.append(cp)
                recvs.append(cp)
        else:
            cp = remote(0, src.at[:, job["layer"]], out.at[:, job["layer"]], sibling)
            sends.append(cp)
            recvs.append(cp)
        triples.append((local, sends, recvs))
    return triples


def _jobs_start(jobs, in_refs, out_refs, sems):
    for local, sends, _ in _jobs_copies(jobs, in_refs, out_refs, sems):
        for cp in local + sends:
            cp.start()


def _jobs_wait(jobs, in_refs, out_refs, sems):
    triples = _jobs_copies(jobs, in_refs, out_refs, sems)
    for local, sends, _ in triples:
        for cp in local:
            cp.wait()
        for cp in sends:
            cp.wait_send()
    for _, _, recvs in triples:
        for cp in recvs:
            cp.wait_recv()


def _run_jobs(jobs, name):
    ins, outs, alias = _jobs_io(jobs)

    def body(*refs):
        in_refs, out_refs, sems = refs[:len(ins)], refs[len(ins):len(ins) + len(outs)], refs[len(ins) + len(outs):]
        _jobs_start(jobs, in_refs, out_refs, sems)
        _jobs_wait(jobs, in_refs, out_refs, sems)

    return pl.pallas_call(
        body, name=name, in_specs=[HBM_SPEC] * len(ins), out_specs=[HBM_SPEC] * len(outs), out_shape=outs,
        scratch_shapes=_jobs_sems(jobs), input_output_aliases=alias,
    )(*ins)


_DIMS = {
    "nn": (((1,), (0,)), ((), ())),
    "nt": (((1,), (1,)), ((), ())),
    "tn": (((0,), (0,)), ((), ())),
}


def _matmul(a, b, mode, out_dtype, name, *, extras=(), post=None, jobs=(), tm=1024, tn=1024, tk=2048):
    assert a.dtype == BF16 and b.dtype == BF16
    if mode == "nn":
        (m, k), (k2, n) = a.shape, b.shape
    elif mode == "nt":
        (m, k), (n, k2) = a.shape, b.shape
    else:
        (k, m), (k2, n) = a.shape, b.shape
    assert k == k2
    tm, tn, tk = _pick(m, tm), _pick(n, tn), _pick(k, tk)
    nk = k // tk
    grid = (m // tm, n // tn, nk)
    dims = _DIMS[mode]
    out_dtypes = out_dtype if isinstance(out_dtype, (tuple, list)) else (out_dtype,)
    n_main, n_out = 2 + len(extras), len(out_dtypes)

    a_spec = pl.BlockSpec((tk, tm), lambda i, j, kk: (kk, i)) if mode == "tn" else pl.BlockSpec((tm, tk), lambda i, j, kk: (i, kk))
    b_spec = pl.BlockSpec((tn, tk), lambda i, j, kk: (j, kk)) if mode == "nt" else pl.BlockSpec((tk, tn), lambda i, j, kk: (kk, j))
    tile_spec = pl.BlockSpec((tm, tn), lambda i, j, kk: (i, j))
    extras = [e if isinstance(e, tuple) else (e, 0) for e in extras]
    extra_specs = [pl.BlockSpec((tm, tn), functools.partial(lambda i, j, kk, c: (i, j + c), c=c)) for _, c in extras]
    job_ins, job_outs, job_alias = _jobs_io(jobs)
    operands = [a, b, *[e for e, _ in extras], *job_ins]
    in_specs = [a_spec, b_spec] + extra_specs + [HBM_SPEC] * len(job_ins)
    out_shape = [jax.ShapeDtypeStruct((m, n), dt) for dt in out_dtypes] + job_outs
    out_specs = [tile_spec] * n_out + [HBM_SPEC] * len(job_outs)
    aliases = {n_main + i: n_out + o for i, o in job_alias.items()}
    scratch = ([pltpu.VMEM((tm, tn), F32)] if nk > 1 else []) + (_jobs_sems(jobs) if jobs else [])

    def body(*refs):
        a_ref, b_ref = refs[0], refs[1]
        extra_refs = refs[2:n_main]
        job_in_refs = refs[n_main:n_main + len(job_ins)]
        outs = refs[n_main + len(job_ins):]
        o_refs, job_out_refs = outs[:n_out], outs[n_out:n_out + len(job_outs)]
        rest = outs[n_out + len(job_outs):]
        acc_ref = rest[0] if nk > 1 else None
        sems = rest[1:] if nk > 1 else rest
        step = (pl.program_id(0) * grid[1] + pl.program_id(1)) * grid[2] + pl.program_id(2)
        if jobs:
            pl.when(step == 0)(lambda: _jobs_start(jobs, job_in_refs, job_out_refs, sems))

        p = lax.dot_general(a_ref[...], b_ref[...], dims, preferred_element_type=F32)

        def finish(total):
            vals = post(total, *[r[...] for r in extra_refs]) if post is not None else total
            vals = vals if isinstance(vals, (tuple, list)) else (vals,)
            for o_ref, v in zip(o_refs, vals, strict=True):
                o_ref[...] = v.astype(o_ref.dtype)

        if nk == 1:
            finish(p)
        else:
            kk = pl.program_id(2)

            @pl.when(kk == 0)
            def _():
                acc_ref[...] = p

            @pl.when(kk > 0)
            def _():
                acc_ref[...] += p

            @pl.when(kk == nk - 1)
            def _():
                finish(acc_ref[...])

        if jobs:
            pl.when(step == grid[0] * grid[1] * grid[2] - 1)(lambda: _jobs_wait(jobs, job_in_refs, job_out_refs, sems))

    res = pl.pallas_call(
        body,
        name=name,
        grid=grid,
        in_specs=in_specs,
        out_specs=out_specs,
        out_shape=out_shape,
        scratch_shapes=scratch,
        input_output_aliases=aliases,
        compiler_params=_params(("arbitrary", "arbitrary", "arbitrary") if jobs else ("parallel", "parallel", "arbitrary")),
    )(*operands)
    return res[0] if len(res) == 1 else res


def _rowwise(fn, rows, consts, outs, reds, name, *, tr=512, ncol=1, n_rows=None, jobs=()):
    if n_rows is None:
        n_rows = rows[0][0].shape[0]
    tr = min(tr, n_rows)
    assert n_rows % tr == 0
    in_specs, operands = [], []
    for arr, width, coloff, rowoff in rows:
        in_specs.append(pl.BlockSpec((tr, width), functools.partial(lambda j, i, c, r: (i + r, j + c), c=coloff, r=rowoff)))
        operands.append(arr)
    for arr in consts:
        if arr.ndim == 3:
            in_specs.append(pl.BlockSpec(arr.shape, lambda j, i: (0, 0, 0)))
        else:
            in_specs.append(pl.BlockSpec(arr.shape, lambda j, i: (0, 0)))
        operands.append(arr)
    out_specs, out_shape = [], []
    for width, dtype in outs:
        out_specs.append(pl.BlockSpec((tr, width), lambda j, i: (i, j)))
        out_shape.append(jax.ShapeDtypeStruct((n_rows, ncol * width), dtype))
    for shape in reds:
        if len(shape) == 3:
            out_specs.append(pl.BlockSpec(shape, lambda j, i: (0, 0, 0)))
            out_shape.append(jax.ShapeDtypeStruct(shape, F32))
        else:
            out_specs.append(pl.BlockSpec(shape, lambda j, i: (0, j)))
            out_shape.append(jax.ShapeDtypeStruct((shape[0], ncol * shape[1]), F32))
    n_in, n_out, n_red = len(operands), len(outs), len(reds)
    job_ins, job_outs, job_alias = _jobs_io(jobs)
    n_ji, n_jo = len(job_ins), len(job_outs)
    n_steps = ncol * (n_rows // tr)

    def body(*refs):
        job_in_refs = refs[n_in:n_in + n_ji]
        o_refs = refs[n_in + n_ji:n_in + n_ji + n_out]
        red_refs = refs[n_in + n_ji + n_out:n_in + n_ji + n_out + n_red]
        job_out_refs = refs[n_in + n_ji + n_out + n_red:n_in + n_ji + n_out + n_red + n_jo]
        sems = refs[n_in + n_ji + n_out + n_red + n_jo:]
        step = pl.program_id(0) * (n_rows // tr) + pl.program_id(1)
        if jobs:
            pl.when(step == 0)(lambda: _jobs_start(jobs, job_in_refs, job_out_refs, sems))
        vals = fn(*[r[...] for r in refs[:n_in]])
        if not isinstance(vals, (tuple, list)):
            vals = (vals,)
        assert len(vals) == n_out + n_red
        for o_ref, v in zip(o_refs, vals[:n_out]):
            o_ref[...] = v.astype(o_ref.dtype)
        if n_red:
            first = pl.program_id(1) == 0
            for o_ref, v in zip(red_refs, vals[n_out:]):
                @pl.when(first)
                def _(o_ref=o_ref, v=v):
                    o_ref[...] = v

                @pl.when(jnp.logical_not(first))
                def _(o_ref=o_ref, v=v):
                    o_ref[...] += v
        if jobs:
            pl.when(step == n_steps - 1)(lambda: _jobs_wait(jobs, job_in_refs, job_out_refs, sems))

    res = pl.pallas_call(
        body,
        name=name,
        grid=(ncol, n_rows // tr),
        in_specs=in_specs + [HBM_SPEC] * n_ji,
        out_specs=out_specs + [HBM_SPEC] * n_jo,
        out_shape=out_shape + job_outs,
        scratch_shapes=_jobs_sems(jobs) if jobs else [],
        input_output_aliases={n_in + i: n_out + n_red + o for i, o in job_alias.items()},
        compiler_params=_params(("arbitrary", "arbitrary") if jobs else ("parallel", "arbitrary")),
    )(*(operands + job_ins))
    return res


def _row(arr, width=None, coloff=0, rowoff=0):
    return (arr, arr.shape[1] if width is None else width, coloff, rowoff)


def _rms(x, g):
    return x * lax.rsqrt(jnp.mean(x * x, axis=-1, keepdims=True) + EPS) * g


def _erf(x):
    x = jnp.clip(x, -4.0, 4.0)
    x2 = x * x
    alpha = x2 * -2.72614225801306e-10 + 2.77068142495902e-08
    alpha = alpha * x2 - 2.10102402082508e-06
    alpha = alpha * x2 - 5.69250639462346e-05
    alpha = alpha * x2 - 7.34990630326855e-04
    alpha = alpha * x2 - 2.95459980854025e-03
    alpha = alpha * x2 - 1.60960333262415e-02
    beta = x2 * -1.45660718464996e-05 - 2.13374055278905e-04
    beta = beta * x2 - 1.68282697438203e-03
    beta = beta * x2 - 7.37332916720468e-03
    beta = beta * x2 - 1.42647390514189e-02
    return x * alpha / beta


def _gelu(x):
    return 0.5 * x * (1.0 + _erf(x * (2.0 ** -0.5)))


def _gelu_and_grad(x):
    cdf = 0.5 * (1.0 + _erf(x * (2.0 ** -0.5)))
    pdf = jnp.exp(-0.5 * x * x) * (2.0 * jnp.pi) ** -0.5
    return x * cdf, cdf + x * pdf


def _layer_norm(x, g, b):
    mu = jnp.mean(x, axis=-1, keepdims=True)
    xc = x - mu
    return xc * lax.rsqrt(jnp.mean(xc * xc, axis=-1, keepdims=True) + EPS) * g + b


def _per_head(fn, x, g):
    parts = []
    for h in range(HEADS):
        sl = slice(h * HEAD_DIM, (h + 1) * HEAD_DIM)
        parts.append(fn(x[:, sl], g[:, sl]))
    return parts


def _sgu_mask():
    i = lax.broadcasted_iota(jnp.int32, (SGU_LEN, SGU_LEN), 0)
    j = lax.broadcasted_iota(jnp.int32, (SGU_LEN, SGU_LEN), 1)
    return (j // CHUNK) <= (i // CHUNK)


def _sgu_mix(vln_bf, w_bf, bias):
    rows = vln_bf.shape[0]
    out_rows = []
    for c in range(rows // SGU_LEN):
        cols = []
        for g in range(GROUPS):
            blk = vln_bf[c * SGU_LEN:(c + 1) * SGU_LEN, g * GROUP_DIM:(g + 1) * GROUP_DIM]
            mixed = jnp.dot(w_bf[g], blk, preferred_element_type=F32) + bias[g]
            cols.append(mixed)
        out_rows.append(jnp.concatenate(cols, axis=1))
    return out_rows[0] if len(out_rows) == 1 else jnp.concatenate(out_rows, axis=0)


def _split_dots(xs, m_bf):
    his = [x.astype(BF16) for x in xs]
    los = [(x - hi.astype(F32)).astype(BF16) for x, hi in zip(xs, his)]
    res = jnp.dot(jnp.concatenate(his + los, axis=0), m_bf, preferred_element_type=F32)
    n, rows = len(xs), xs[0].shape[0]
    return [res[h * rows:(h + 1) * rows] + res[(n + h) * rows:(n + h + 1) * rows] for h in range(n)]


def _att_tiles(qs, kbs, run_keeps, diag, want_sig=False):
    blk = ATT_BLOCK
    row = lax.broadcasted_iota(jnp.int32, (blk, blk), 0)
    col = lax.broadcasted_iota(jnp.int32, (blk, blk), 1)
    later = (row > col).astype(BF16)
    past = col < row
    zs = [lax.dot_general(q, kb, _DIMS["nt"], preferred_element_type=F32) * (HEAD_DIM ** -0.5) for q, kb in zip(qs, kbs)]
    es = [jnp.exp(-jnp.abs(z)) for z in zs]
    softplus = [jnp.maximum(z, 0.0) + jnp.log(1.0 + e) for z, e in zip(zs, es)]
    log_keeps = [jnp.where(past, -sp, 0.0) if diag else -sp for sp in softplus]
    tails = _split_dots(log_keeps, later)
    weights = [jnp.exp((z - sp) + (tail + keep)) for z, sp, tail, keep in zip(zs, softplus, tails, run_keeps)]
    if diag:
        weights = [jnp.where(past, a, 0.0) for a in weights]
    sigs = None
    if want_sig:
        invs = [1.0 / (1.0 + e) for e in es]
        sigs = [jnp.where(z >= 0.0, inv, e * inv) for z, e, inv in zip(zs, es, invs)]
    return weights, sigs, [jnp.sum(lk, axis=1, keepdims=True) for lk in log_keeps]


def _any_alive(keeps):
    return jnp.max(functools.reduce(jnp.maximum, keeps)) >= LOG_ZERO


def _attention_fwd(qn, kn, vb, *, heads_per_step=8, jobs=()):
    t = qn.shape[0]
    blk = ATT_BLOCK
    nq = t // blk
    hp = heads_per_step
    groups, width = HEADS // hp, hp * HEAD_DIM
    heads = [slice(h * HEAD_DIM, (h + 1) * HEAD_DIM) for h in range(hp)]
    job_ins, job_outs, job_alias = _jobs_io(jobs)
    n_ji, n_jo = len(job_ins), len(job_outs)

    def body(q_ref, k_ref, v_ref, *rest):
        job_in_refs, o_ref, o32_ref = rest[:n_ji], rest[n_ji], rest[n_ji + 1]
        job_out_refs, sems = rest[n_ji + 2:n_ji + 2 + n_jo], rest[n_ji + 2 + n_jo:]
        i = pl.program_id(1)
        step = pl.program_id(0) * nq + i
        if jobs:
            pl.when(step == 0)(lambda: _jobs_start(jobs, job_in_refs, job_out_refs, sems))
        qs = [q_ref[:, sl] for sl in heads]

        def tile(j, keeps, accs, diag):
            rows = pl.ds(pl.multiple_of(j * blk, blk), blk)
            weights, _, keep_sums = _att_tiles(qs, [k_ref[rows, sl] for sl in heads], keeps, diag)
            accs = [acc + jnp.dot(a.astype(BF16), v_ref[rows, sl], preferred_element_type=F32)
                    for acc, a, sl in zip(accs, weights, heads)]
            return tuple(k + s for k, s in zip(keeps, keep_sums)), tuple(accs)

        zero = tuple(jnp.zeros((blk, 1), F32) for _ in heads)
        keeps, accs = tile(i, zero, tuple(jnp.zeros((blk, HEAD_DIM), F32) for _ in heads), True)

        def cond(c):
            return jnp.logical_and(c[0] >= 0, _any_alive(c[1]))

        def step_fn(c):
            j, keeps, accs = c
            keeps, accs = tile(j, keeps, accs, False)
            return j - 1, keeps, accs

        _, _, accs = lax.while_loop(cond, step_fn, (i - 1, keeps, accs))
        for h, sl in enumerate(heads):
            o_ref[:, sl] = accs[h].astype(o_ref.dtype)
            o32_ref[:, sl] = accs[h]
        if jobs:
            pl.when(step == groups * nq - 1)(lambda: _jobs_wait(jobs, job_in_refs, job_out_refs, sems))

    blk_spec = pl.BlockSpec((blk, width), lambda g, i: (i, g))
    head_spec = pl.BlockSpec((t, width), lambda g, i: (0, g), pipeline_mode=pl.Buffered(1))
    return pl.pallas_call(
        body,
        name="attention_fwd",
        grid=(groups, nq),
        in_specs=[blk_spec, head_spec, head_spec] + [HBM_SPEC] * n_ji,
        out_specs=[blk_spec, blk_spec] + [HBM_SPEC] * n_jo,
        out_shape=[jax.ShapeDtypeStruct((t, SB_WIDTH), BF16), jax.ShapeDtypeStruct((t, SB_WIDTH), F32)] + job_outs,
        scratch_shapes=_jobs_sems(jobs) if jobs else [],
        input_output_aliases={3 + i: 2 + o for i, o in job_alias.items()},
        compiler_params=_params(("arbitrary", "arbitrary")),
    )(qn, kn, vb, *job_ins)


def _attention_bwd(qn, kn, vb, do, o32, *, heads_per_step=4, jobs=()):
    t = qn.shape[0]
    blk = ATT_BLOCK
    nq = t // blk
    hp = heads_per_step
    groups, width = HEADS // hp, hp * HEAD_DIM
    heads = [slice(h * HEAD_DIM, (h + 1) * HEAD_DIM) for h in range(hp)]
    job_ins, job_outs, job_alias = _jobs_io(jobs)
    n_ji, n_jo = len(job_ins), len(job_outs)

    def body(q_ref, k_ref, v_ref, do_ref, o32_ref, *rest):
        job_in_refs = rest[:n_ji]
        dq_ref, dk_ref, dv_ref = rest[n_ji:n_ji + 3]
        job_out_refs, sems = rest[n_ji + 3:n_ji + 3 + n_jo], rest[n_ji + 3 + n_jo:]
        i = pl.program_id(1)
        step = pl.program_id(0) * nq + i
        if jobs:
            pl.when(step == 0)(lambda: _jobs_start(jobs, job_in_refs, job_out_refs, sems))

        @pl.when(i == 0)
        def _():
            dk_ref[...] = jnp.zeros_like(dk_ref)
            dv_ref[...] = jnp.zeros_like(dv_ref)

        qs = [q_ref[:, sl] for sl in heads]
        dos = [do_ref[:, sl] for sl in heads]
        totals = [jnp.sum(d.astype(F32) * o32_ref[:, sl], axis=1, keepdims=True) for d, sl in zip(dos, heads)]
        row = lax.broadcasted_iota(jnp.int32, (blk, blk), 0)
        col = lax.broadcasted_iota(jnp.int32, (blk, blk), 1)
        from_here = (row >= col).astype(BF16)

        def tile(j, keeps, run_dlas, dqs, diag):
            rows = pl.ds(pl.multiple_of(j * blk, blk), blk)
            kbs = [k_ref[rows, sl] for sl in heads]
            weights, sigs, keep_sums = _att_tiles(qs, kbs, keeps, diag, want_sig=True)
            wbs = [a.astype(BF16) for a in weights]
            dps = [lax.dot_general(d, v_ref[rows, sl], _DIMS["nt"], preferred_element_type=F32) for d, sl in zip(dos, heads)]
            dlas = [wb.astype(F32) * dp for wb, dp in zip(wbs, dps)]
            later_sums = _split_dots(dlas, from_here)
            dzbs = []
            for dla, sig, total, later, run in zip(dlas, sigs, totals, later_sums, run_dlas):
                dz = dla * (1.0 - sig) - sig * (total - (later + run))
                if diag:
                    dz = jnp.where(col < row, dz, 0.0)
                dzbs.append((dz * (HEAD_DIM ** -0.5)).astype(BF16))
            dqs = [dq + jnp.dot(dzb, kb, preferred_element_type=F32) for dq, dzb, kb in zip(dqs, dzbs, kbs)]
            for sl, dzb, wb, q, d in zip(heads, dzbs, wbs, qs, dos):
                dk_ref[rows, sl] += lax.dot_general(dzb, q, _DIMS["tn"], preferred_element_type=F32)
                dv_ref[rows, sl] += lax.dot_general(wb, d, _DIMS["tn"], preferred_element_type=F32)
            keeps = tuple(k + s for k, s in zip(keeps, keep_sums))
            run_dlas = tuple(r + jnp.sum(dla, axis=1, keepdims=True) for r, dla in zip(run_dlas, dlas))
            return keeps, run_dlas, tuple(dqs)

        zero = tuple(jnp.zeros((blk, 1), F32) for _ in heads)
        keeps, run_dlas, dqs = tile(i, zero, zero, tuple(jnp.zeros((blk, HEAD_DIM), F32) for _ in heads), True)

        def cond(c):
            return jnp.logical_and(c[0] >= 0, _any_alive(c[1]))

        def step_fn(c):
            j, keeps, run_dlas, dqs = c
            keeps, run_dlas, dqs = tile(j, keeps, run_dlas, dqs, False)
            return j - 1, keeps, run_dlas, dqs

        _, _, _, dqs = lax.while_loop(cond, step_fn, (i - 1, keeps, run_dlas, dqs))
        for h, sl in enumerate(heads):
            dq_ref[:, sl] = dqs[h]
        if jobs:
            pl.when(step == groups * nq - 1)(lambda: _jobs_wait(jobs, job_in_refs, job_out_refs, sems))

    blk_spec = pl.BlockSpec((blk, width), lambda g, i: (i, g))
    head_spec = pl.BlockSpec((t, width), lambda g, i: (0, g), pipeline_mode=pl.Buffered(1))
    full = jax.ShapeDtypeStruct((t, SB_WIDTH), F32)
    return pl.pallas_call(
        body,
        name="attention_bwd",
        grid=(groups, nq),
        in_specs=[blk_spec, head_spec, head_spec, blk_spec, blk_spec] + [HBM_SPEC] * n_ji,
        out_specs=[blk_spec, head_spec, head_spec] + [HBM_SPEC] * n_jo,
        out_shape=[full, full, full] + job_outs,
        scratch_shapes=_jobs_sems(jobs) if jobs else [],
        input_output_aliases={5 + i: 3 + o for i, o in job_alias.items()},
        compiler_params=_params(("arbitrary", "arbitrary"), ATT_BWD_VMEM_LIMIT),
    )(qn, kn, vb, do, o32, *job_ins)


def _sum_over_devices(v, jobs=()):
    rows, width = v.shape
    job_ins, job_outs, job_alias = _jobs_io(jobs)
    n_ji, n_jo = len(job_ins), len(job_outs)

    def body(v_ref, *rest):
        job_in_refs, o_ref = rest[:n_ji], rest[n_ji]
        job_out_refs = rest[n_ji + 1:n_ji + 1 + n_jo]
        buf, send_sems, recv_sems = rest[n_ji + 1 + n_jo:n_ji + 4 + n_jo]
        job_sems = rest[n_ji + 4 + n_jo:]
        if jobs:
            _jobs_start(jobs, job_in_refs, job_out_refs, job_sems)
        x, y, c = _place()
        me = 4 * x + 2 * y + c
        buf[me] = v_ref[...]
        sends = []
        for d in range(1, N_DEV):
            px, py, pc = x ^ (d >> 2), y ^ ((d >> 1) & 1), c ^ (d & 1)
            cp = pltpu.make_async_remote_copy(
                src_ref=v_ref, dst_ref=buf.at[me], send_sem=send_sems.at[d - 1], recv_sem=recv_sems.at[d - 1],
                device_id=(px, py, pc), device_id_type=MESH)
            cp.start()
            sends.append((cp, 4 * px + 2 * py + pc))
        for cp, _ in sends:
            cp.wait_send()
        for d, (cp, peer) in enumerate(sends):
            pltpu.make_async_remote_copy(
                src_ref=v_ref, dst_ref=buf.at[peer], send_sem=send_sems.at[d], recv_sem=recv_sems.at[d],
                device_id=(x, y, c), device_id_type=MESH).wait_recv()
        total = buf[0]
        for d in range(1, N_DEV):
            total = total + buf[d]
        o_ref[...] = total
        if jobs:
            _jobs_wait(jobs, job_in_refs, job_out_refs, job_sems)

    vmem = pl.BlockSpec(memory_space=pltpu.VMEM)
    return pl.pallas_call(
        body,
        name="sum_over_devices",
        in_specs=[vmem] + [HBM_SPEC] * n_ji,
        out_specs=[vmem] + [HBM_SPEC] * n_jo,
        out_shape=[jax.ShapeDtypeStruct((rows, width), F32)] + job_outs,
        scratch_shapes=[pltpu.VMEM((N_DEV, rows, width), F32), pltpu.SemaphoreType.DMA((N_DEV - 1,)), pltpu.SemaphoreType.DMA((N_DEV - 1,))]
        + (_jobs_sems(jobs) if jobs else []),
        input_output_aliases={1 + i: 1 + o for i, o in job_alias.items()},
        compiler_params=pltpu.CompilerParams(vmem_limit_bytes=V7X_VMEM_LIMIT),
    )(v, *job_ins)


def _adamw_math(w, g, m, v):
    m = ADAM_B1 * m + (1.0 - ADAM_B1) * g
    v = ADAM_B2 * v + (1.0 - ADAM_B2) * (g * g)
    m_hat = m / (1.0 - ADAM_B1 ** ADAM_STEP)
    v_hat = v / (1.0 - ADAM_B2 ** ADAM_STEP)
    delta = -ADAM_LR * (m_hat / (jnp.sqrt(v_hat) + ADAM_EPS) + ADAM_WD * w)
    return delta, m, v


def _adamw_sharded(mine, theirs, w, m, v, name):
    width = _pick(w.shape[1], 1152)
    ncol = w.shape[1] // width
    shard_rows = w.shape[0]
    tr = min(256, shard_rows)

    def slots(arr):
        return [_row(arr, width, 0, s * (shard_rows // tr)) for s in range(4)]

    def total(a, b, c, d):
        return ((a.astype(F32) + b.astype(F32)) + c.astype(F32)) + d.astype(F32)

    def fn(a0, a1, a2, a3, b0, b1, b2, b3, w, m, v):
        g = total(a0, a1, a2, a3) + total(b0, b1, b2, b3)
        delta, m, v = _adamw_math(w, g, m, v)
        return g, delta, m, v

    return _rowwise(fn, slots(mine) + slots(theirs) + [_row(t, width) for t in (w, m, v)], [], [(width, F32)] * 4, [], name,
                    tr=tr, ncol=ncol, n_rows=shard_rows)


def _adamw_small(g, w, m, v):
    def fn(g, w, m, v):
        return _adamw_math(w, g, m, v)

    return _rowwise(fn, [_row(t) for t in (g, w, m, v)], [], [(g.shape[1], F32)] * 3, [], "adamw_small", tr=g.shape[0])


def _flat(a):
    return a.reshape(-1, a.shape[-1])


def kernel(x, g_mix, w_in, g_q, g_k, sgu_ln_g, sgu_ln_b, w_spatial, b_spatial, w_oa, w_ob, w_out, g_ff, w_ff1, w_ff2, loss_target, m_g_mix, m_w_in, m_g_q, m_g_k, m_sgu_ln_g, m_sgu_ln_b, m_w_spatial, m_b_spatial, m_w_oa, m_w_ob, m_w_out, m_g_ff, m_w_ff1, m_w_ff2, v_g_mix, v_w_in, v_g_q, v_g_k, v_sgu_ln_g, v_sgu_ln_b, v_w_spatial, v_b_spatial, v_w_oa, v_w_ob, v_w_out, v_g_ff, v_w_ff1, v_w_ff2):
    depth = g_mix.shape[0]
    seq, d_model = x.shape[1], x.shape[2]
    d_ff = w_ff1.shape[2] * N_CHIPS
    in_cols = w_in.shape[2] * N_CHIPS
    col_gate_a = 3 * SB_WIDTH + 2 * SGU_WIDTH
    assert in_cols == col_gate_a + 2 * d_model

    big = [w_in, w_oa, w_ob, w_out, w_ff1, w_ff2]
    big_m = [m_w_in, m_w_oa, m_w_ob, m_w_out, m_w_ff1, m_w_ff2]
    big_v = [v_w_in, v_w_oa, v_w_ob, v_w_out, v_w_ff1, v_w_ff2]
    axes = [2, 2, 2, 1, 2, 1]
    W_IN, W_OA, W_OB, W_OUT, W_FF1, W_FF2 = range(6)
    shards = [w.astype(BF16) for w in big]
    full = [[None] * depth for _ in big]

    def gather(t, l):
        return _gather_job(shards[t], l, axes[t])

    def forward(t, l):
        return _forward_job(full[t][l], axes[t])

    def add_residual(total, res):
        return total + res

    (full[W_IN][0],) = _run_jobs([gather(W_IN, 0)], "gather_first")
    (full[W_IN][0],) = _run_jobs([forward(W_IN, 0)], "forward_first")

    gate_w = _pick(d_model, 1024)
    bias_col = b_spatial[..., None]

    def rms_fwd(xin, g, name):
        return _rowwise(lambda xb, gb: _rms(xb, gb), [_row(xin)], [g], [(d_model, BF16)], [], name)[0]

    saved = []
    cur = x.reshape(seq, d_model)
    for l in range(depth):
        hb = rms_fwd(cur, g_mix[l:l + 1], "rms_mix")
        early = [W_FF1, W_OA, W_OB, W_OUT]
        proj, *got = _matmul(hb, full[W_IN][l], "nn", F32, "proj", jobs=[gather(t, l) for t in early])
        for t, w in zip(early, got):
            full[t][l] = w

        def qk_fn(q, k, v, gq, gk):
            qn = jnp.concatenate(_per_head(_rms, q, gq), axis=1)
            kn = jnp.concatenate(_per_head(_rms, k, gk), axis=1)
            return qn, kn, v

        gq, gk = g_q[l].reshape(1, SB_WIDTH), g_k[l].reshape(1, SB_WIDTH)
        qn, kn, vb = _rowwise(qk_fn, [_row(proj, SB_WIDTH, 0), _row(proj, SB_WIDTH, 1), _row(proj, SB_WIDTH, 2)],
                              [gq, gk], [(SB_WIDTH, BF16)] * 3, [], "qk_norm")
        o, o32, *got = _attention_fwd(qn, kn, vb, jobs=[forward(t, l) for t in early] + [gather(W_FF2, l)])
        for t, w in zip(early + [W_FF2], got):
            full[t][l] = w
        ya = _matmul(o, full[W_OA][l], "nn", F32, "proj_a")

        def sgu_fn(u_pre, v_pre, ln_g, ln_b, w_s, b_t):
            w_bf = jnp.where(_sgu_mask()[None], w_s, 0.0).astype(BF16)
            vln = _layer_norm(_gelu(v_pre), ln_g, ln_b)
            return _gelu(u_pre) * _sgu_mix(vln.astype(BF16), w_bf, b_t)

        sgu_consts = [sgu_ln_g[l:l + 1], sgu_ln_b[l:l + 1], w_spatial[l], bias_col[l]]
        s = _rowwise(sgu_fn, [_row(proj, SGU_WIDTH, 3), _row(proj, SGU_WIDTH, 4)], sgu_consts, [(SGU_WIDTH, BF16)], [], "sgu_fwd")[0]
        def merge_fn(b, ga, gb, a):
            return b, jax.nn.sigmoid(ga) * a + jax.nn.sigmoid(gb) * b

        gates = [(proj, col_gate_a // gate_w), (proj, (col_gate_a + d_model) // gate_w)]
        yb, merged = _matmul(s, full[W_OB][l], "nn", (F32, BF16), "proj_b", extras=gates + [ya], post=merge_fn, tm=512, tn=gate_w)
        x1 = _matmul(merged, full[W_OUT][l], "nn", F32, "proj_out", extras=[cur], post=add_residual)
        h2 = rms_fwd(x1, g_ff[l:l + 1], "rms_ff")
        more = l + 1 < depth
        a1, r, full[W_FF2][l], *got = _matmul(h2, full[W_FF1][l], "nn", (F32, BF16), "ff1",
                                              jobs=[forward(W_FF2, l)] + ([gather(W_IN, l + 1)] if more else []),
                                              post=lambda total: (total, jnp.square(jnp.maximum(total, 0.0))))
        if more:
            full[W_IN][l + 1] = got[0]
            x2, full[W_IN][l + 1] = _matmul(r, full[W_FF2][l], "nn", F32, "ff2", extras=[x1], post=add_residual,
                                            jobs=[forward(W_IN, l + 1)])
        else:
            x2 = _matmul(r, full[W_FF2][l], "nn", F32, "ff2", extras=[x1], post=add_residual)
        saved.append(dict(x=cur, hb=hb, proj=proj, qn=qn, kn=kn, vb=vb, o=o, o32=o32, ya=ya, yb=yb, s=s, merged=merged,
                          x1=x1, h2=h2, a1=a1, r=r, gq=gq, gk=gk, sgu_consts=sgu_consts, gates=gates))
        cur = x2

    def loss_fn(y, target):
        err = y - target
        per_row = jnp.sum(err * err, axis=-1, keepdims=True) * (1.0 / d_model)
        part = 0.5 * jnp.sum(per_row, axis=0, keepdims=True)
        dy = err * (1.0 / d_model)
        return dy, dy, jnp.broadcast_to(part, (8, LANE))

    dx, dxb, loss_part = _rowwise(loss_fn, [_row(cur), _row(loss_target.reshape(seq, d_model))], [],
                                  [(d_model, F32), (d_model, BF16)], [(8, LANE)], "loss")
    loss = lax.psum(loss_part[0, 0], ("x", "y", "c"))

    landed = [None] * len(big)

    theirs = [None] * len(big)

    def scatter(t, grad, l):
        return _scatter_job(grad, landed[t], (4,) + shards[t].shape, l, axes[t])

    def mirror(t, l):
        return _mirror_job(landed[t], theirs[t], l)

    small = {n: [None] * depth for n in ("g_mix", "g_q", "g_k", "ln_g", "ln_b", "w_s", "b_s", "g_ff")}

    def rms_bwd(dh, xin, dres, g, name):
        def fn(dh, xin, dres, g):
            _, vjp = jax.vjp(_rms, xin, g)
            dxin, dg = vjp(dh)
            total = dres + dxin
            return total, total, dg

        return _rowwise(fn, [_row(dh), _row(xin), _row(dres)], [g], [(d_model, F32), (d_model, BF16)], [(1, d_model)], name)

    for l in reversed(range(depth)):
        sv = saved[l]
        da1 = _matmul(dxb, full[W_FF2][l], "nt", BF16, "ff2_dx", extras=[sv["a1"]],
                      post=lambda total, a: total * (2.0 * jnp.maximum(a, 0.0)),
                      jobs=[mirror(t, l + 1) for t in (W_IN, W_FF1)] if l + 1 < depth else [])
        grad_ff2 = _matmul(sv["r"], dxb, "tn", BF16, "ff2_dw", tk=WGRAD_TK,
                           jobs=[mirror(t, l + 1) for t in (W_FF2, W_OUT, W_OA, W_OB)] if l + 1 < depth else [])
        if l + 1 < depth:
            da1, theirs[W_IN], theirs[W_FF1] = da1
            grad_ff2, theirs[W_FF2], theirs[W_OUT], theirs[W_OA], theirs[W_OB] = grad_ff2
        dh2 = _matmul(da1, full[W_FF1][l], "nt", F32, "ff1_dx")
        grad_ff1 = _matmul(sv["h2"], da1, "tn", BF16, "ff1_dw", tk=WGRAD_TK)
        dx1, dx1b, small["g_ff"][l] = rms_bwd(dh2, sv["x1"], dx, g_ff[l:l + 1], "rms_ff_bwd")

        def merge_bwd_fn(dm, ga, gb, a, b):
            sa, sb = jax.nn.sigmoid(ga), jax.nn.sigmoid(gb)
            return dm * a * sa * (1.0 - sa), dm * b * sb * (1.0 - sb), dm * sa, dm * sb

        dga, dgb, dya, dyb = _matmul(dx1b, full[W_OUT][l], "nt", (BF16,) * 4, "out_dx", tm=512, tn=gate_w,
                                     extras=sv["gates"] + [sv["ya"], sv["yb"]], post=merge_bwd_fn)
        grad_out = _matmul(sv["merged"], dx1b, "tn", BF16, "out_dw", tk=WGRAD_TK)

        ds = _matmul(dyb, full[W_OB][l], "nt", F32, "ob_dx")
        grad_ob = _matmul(sv["s"], dyb, "tn", BF16, "ob_dw", tk=WGRAD_TK)

        def sgu_bwd_fn(ds, u_pre, v_pre, ln_g, ln_b, w_s, b_t):
            mask = _sgu_mask()
            w_bf = jnp.where(mask[None], w_s, 0.0).astype(BF16)
            u, u_grad = _gelu_and_grad(u_pre)
            vg, vg_grad = _gelu_and_grad(v_pre)
            vln, ln_vjp = jax.vjp(_layer_norm, vg, ln_g, ln_b)
            vln_bf = vln.astype(BF16)
            mixed = _sgu_mix(vln_bf, w_bf, b_t)
            du_pre = ds * mixed * u_grad
            dmixed = ds * u
            dm_bf = dmixed.astype(BF16)
            dw = [jnp.zeros((SGU_LEN, SGU_LEN), F32) for _ in range(GROUPS)]
            db = [jnp.zeros((SGU_LEN, 1), F32) for _ in range(GROUPS)]
            dvln_rows = []
            for c in range(ds.shape[0] // SGU_LEN):
                rows = slice(c * SGU_LEN, (c + 1) * SGU_LEN)
                cols = []
                for g in range(GROUPS):
                    sl = slice(g * GROUP_DIM, (g + 1) * GROUP_DIM)
                    cols.append(lax.dot_general(w_bf[g], dm_bf[rows, sl], _DIMS["tn"], preferred_element_type=F32))
                    dw[g] = dw[g] + lax.dot_general(dm_bf[rows, sl], vln_bf[rows, sl], _DIMS["nt"], preferred_element_type=F32)
                    db[g] = db[g] + jnp.sum(dmixed[rows, sl], axis=1, keepdims=True)
                dvln_rows.append(jnp.concatenate(cols, axis=1))
            dvln = dvln_rows[0] if len(dvln_rows) == 1 else jnp.concatenate(dvln_rows, axis=0)
            dvg, dln_g, dln_b = ln_vjp(dvln)
            dv_pre = dvg * vg_grad
            dw_s = jnp.stack([jnp.where(mask, d, 0.0) for d in dw])
            return du_pre, dv_pre, dln_g, dln_b, dw_s, jnp.stack(db)

        du, dvs, small["ln_g"][l], small["ln_b"][l], small["w_s"][l], db_col = _rowwise(
            sgu_bwd_fn, [_row(ds), _row(sv["proj"], SGU_WIDTH, 3), _row(sv["proj"], SGU_WIDTH, 4)], sv["sgu_consts"],
            [(SGU_WIDTH, BF16)] * 2, [(1, SGU_WIDTH), (1, SGU_WIDTH), (GROUPS, SGU_LEN, SGU_LEN), (GROUPS, SGU_LEN, 1)], "sgu_bwd", tr=256)
        small["b_s"][l] = db_col[..., 0]

        do = _matmul(dya, full[W_OA][l], "nt", BF16, "oa_dx")
        grad_oa = _matmul(sv["o"], dya, "tn", BF16, "oa_dw", tk=WGRAD_TK)
        dqn, dkn, dv, landed[W_FF2], landed[W_OUT], landed[W_OB], landed[W_OA] = _attention_bwd(
            sv["qn"], sv["kn"], sv["vb"], do, sv["o32"],
            jobs=[scatter(W_FF2, grad_ff2, l), scatter(W_OUT, grad_out, l), scatter(W_OB, grad_ob, l), scatter(W_OA, grad_oa, l)])

        def qk_bwd_fn(dqn, dkn, dv, q, k, gq, gk):
            outs = []
            for d, xin, g in ((dqn, q, gq), (dkn, k, gk)):
                dxs, dgs = [], []
                for h in range(HEADS):
                    sl = slice(h * HEAD_DIM, (h + 1) * HEAD_DIM)
                    _, vjp = jax.vjp(_rms, xin[:, sl], g[:, sl])
                    dxh, dgh = vjp(d[:, sl])
                    dxs.append(dxh)
                    dgs.append(dgh)
                outs.append((jnp.concatenate(dxs, axis=1), jnp.concatenate(dgs, axis=1)))
            return outs[0][0], outs[1][0], dv, outs[0][1], outs[1][1]

        dq, dk, dvb, small["g_q"][l], small["g_k"][l] = _rowwise(
            qk_bwd_fn, [_row(dqn), _row(dkn), _row(dv), _row(sv["proj"], SB_WIDTH, 0), _row(sv["proj"], SB_WIDTH, 1)],
            [sv["gq"], sv["gk"]], [(SB_WIDTH, BF16)] * 3, [(1, SB_WIDTH), (1, SB_WIDTH)], "qk_norm_bwd")

        dproj = jnp.concatenate([dq, dk, dvb, du, dvs, dga, dgb], axis=1)
        grad_in, landed[W_FF1] = _matmul(sv["hb"], dproj, "tn", BF16, "in_dw", tk=WGRAD_TK, jobs=[scatter(W_FF1, grad_ff1, l)])
        dh, landed[W_IN] = _matmul(dproj, full[W_IN][l], "nt", F32, "in_dx", tk=3072, jobs=[scatter(W_IN, grad_in, l)])
        dx, dxb, small["g_mix"][l] = rms_bwd(dh, sv["x"], dx1, g_mix[l:l + 1], "rms_mix_bwd")

    names = ["g_mix", "g_q", "g_k", "ln_g", "ln_b", "w_s", "b_s", "g_ff"]
    small_w = [g_mix, g_q, g_k, sgu_ln_g, sgu_ln_b, w_spatial, b_spatial, g_ff]
    small_m = [m_g_mix, m_g_q, m_g_k, m_sgu_ln_g, m_sgu_ln_b, m_w_spatial, m_b_spatial, m_g_ff]
    small_v = [v_g_mix, v_g_q, v_g_k, v_sgu_ln_g, v_sgu_ln_b, v_w_spatial, v_b_spatial, v_g_ff]

    def pack(parts):
        return jnp.concatenate([p.reshape(-1, LANE) for p in parts], axis=0)

    local_small = pack([jnp.stack(small[n]) for n in names])
    g_small, *theirs = _sum_over_devices(local_small, [mirror(t, 0) for t in range(len(big))])
    d_small, m_small, v_small = _adamw_small(g_small, pack(small_w), pack(small_m), pack(small_v))

    big_out = []
    for t in range(len(big)):
        res = _adamw_sharded(_flat(landed[t]), _flat(theirs[t]), _flat(big[t]), _flat(big_m[t]), _flat(big_v[t]), "adamw")
        big_out.append([r.reshape(big[t].shape) for r in res])

    def unpack(packed):
        outs, row = [], 0
        for w in small_w:
            n = w.size // LANE
            outs.append(packed[row:row + n].reshape(w.shape))
            row += n
        return outs

    small_out = [unpack(p) for p in (g_small, d_small, m_small, v_small)]

    order = [("s", 0), ("b", W_IN), ("s", 1), ("s", 2), ("s", 3), ("s", 4), ("s", 5), ("s", 6),
             ("b", W_OA), ("b", W_OB), ("b", W_OUT), ("s", 7), ("b", W_FF1), ("b", W_FF2)]
    result = [loss, dx.reshape(x.shape)]
    for kind in range(4):
        for which, idx in order:
            result.append(small_out[kind][idx] if which == "s" else big_out[idx][kind])
    return tuple(result)
```

```python
import functools

import jax
import jax.numpy as jnp
from jax import lax
from jax.experimental import pallas as pl
from jax.experimental.pallas import tpu as pltpu

F32 = jnp.float32
BF16 = jnp.bfloat16
MESH = pl.DeviceIdType.MESH

EPS = 1e-6
HEADS = 8
HEAD_DIM = 128
SB_WIDTH = HEADS * HEAD_DIM
GROUPS = 8
GROUP_DIM = 128
SGU_WIDTH = GROUPS * GROUP_DIM
SGU_LEN = 128
CHUNK = 64
ATT_BLOCK = 128
LOG_ZERO = -104.0

ADAM_LR = 0.001
ADAM_B1 = 0.9
ADAM_B2 = 0.999
ADAM_EPS = 1e-08
ADAM_WD = 0.01
ADAM_STEP = 10

N_CHIPS = 4
N_DEV = 8
V7X_VMEM_LIMIT = 48 * 1024 * 1024
ATT_BWD_VMEM_LIMIT = 56 * 1024 * 1024
WGRAD_TK = 2048
LANE = 128


def _params(sem, vmem_limit=V7X_VMEM_LIMIT):
    return pltpu.CompilerParams(dimension_semantics=sem, vmem_limit_bytes=vmem_limit)


def _pick(dim, pref):
    if dim <= pref:
        return dim
    for t in range(pref - pref % LANE, 0, -LANE):
        if dim % t == 0:
            return t
    raise ValueError(f"no tile for {dim}")


HBM_SPEC = pl.BlockSpec(memory_space=pl.ANY)


def _place():
    return lax.axis_index("x"), lax.axis_index("y"), lax.axis_index("c")


def _other_chips(x, y):
    return [(1 - x, y), (x, 1 - y), (1 - x, 1 - y)]


def _shard_view(ref, axis, index):
    size = ref.shape[axis] // N_CHIPS
    start = pl.multiple_of(index * size, size)
    if axis == 0:
        return ref.at[pl.ds(start, size), :]
    return ref.at[:, pl.ds(start, size)]


def _half_view(view, half):
    rows = view.shape[0] // 2
    return view.at[pl.ds(pl.multiple_of(half * rows, 8), rows), :]


def _gather_job(shards, layer, axis):
    return dict(kind="gather", src=shards, layer=layer, axis=axis - 1)


def _forward_job(weight, axis):
    return dict(kind="forward", src=weight, axis=axis - 1)


def _scatter_job(grad, landed, shape, layer, axis):
    return dict(kind="scatter", src=grad, buf=landed, shape=shape, layer=layer, axis=axis - 1)


def _mirror_job(landed, theirs, layer):
    return dict(kind="mirror", src=landed, buf=theirs, layer=layer)


def _jobs_io(jobs):
    ins, outs, alias = [], [], {}
    for job in jobs:
        src = job["src"]
        ins.append(src)
        if job["kind"] == "gather":
            shape = list(src.shape[1:])
            shape[job["axis"]] *= N_CHIPS
            outs.append(jax.ShapeDtypeStruct(tuple(shape), src.dtype))
        elif job["kind"] == "forward":
            alias[len(ins) - 1] = len(outs)
            outs.append(jax.ShapeDtypeStruct(src.shape, src.dtype))
        elif job["kind"] == "scatter":
            if job["buf"] is not None:
                ins.append(job["buf"])
                alias[len(ins) - 1] = len(outs)
            outs.append(jax.ShapeDtypeStruct(job["shape"], src.dtype))
        else:
            if job["buf"] is not None:
                ins.append(job["buf"])
                alias[len(ins) - 1] = len(outs)
            outs.append(jax.ShapeDtypeStruct(src.shape, src.dtype))
    return ins, outs, alias


def _jobs_sems(jobs):
    n = len(jobs)
    return [pltpu.SemaphoreType.DMA((3 * n,)), pltpu.SemaphoreType.DMA((3 * n,)), pltpu.SemaphoreType.DMA((n,))]


def _jobs_copies(jobs, in_refs, out_refs, sems):
    send_sems, recv_sems, local_sems = sems
    x, y, c = _place()
    me = 2 * x + y
    sibling = (x, y, 1 - c)
    triples, ip = [], 0
    for n, (job, out) in enumerate(zip(jobs, out_refs)):
        kind = job["kind"]
        src = in_refs[ip]
        ip += 2 if (kind in ("scatter", "mirror") and job["buf"] is not None) else 1

        def remote(k, src_ref, dst_ref, to):
            return pltpu.make_async_remote_copy(
                src_ref=src_ref, dst_ref=dst_ref, send_sem=send_sems.at[3 * n + k], recv_sem=recv_sems.at[3 * n + k],
                device_id=to, device_id_type=MESH)

        local, sends, recvs = [], [], []
        if kind == "gather":
            axis = job["axis"]
            mine = src.at[job["layer"]]
            local.append(pltpu.make_async_copy(mine, _shard_view(out, axis, me), local_sems.at[n]))
            for k, (px, py) in enumerate(_other_chips(x, y)):
                sends.append(remote(k, _half_view(mine, c), _half_view(_shard_view(out, axis, me), c), (px, py, c)))
                recvs.append(remote(k, _half_view(mine, c), _half_view(_shard_view(out, axis, 2 * px + py), c), (px, py, c)))
        elif kind == "forward":
            axis = job["axis"]
            for k, (px, py) in enumerate(_other_chips(x, y)):
                got = _shard_view(src, axis, 2 * px + py)
                lands = _shard_view(out, axis, 2 * px + py)
                sends.append(remote(k, _half_view(got, c), _half_view(lands, c), sibling))
                recvs.append(remote(k, _half_view(got, c), _half_view(lands, 1 - c), sibling))
        elif kind == "scatter":
            axis, layer = job["axis"], job["layer"]
            local.append(pltpu.make_async_copy(_shard_view(src, axis, me), out.at[3, layer], local_sems.at[n]))
            for k, (px, py) in enumerate(_other_chips(x, y)):
                cp = remote(k, _shard_view(src, axis, 2 * px + py), out.at[k, layer], (px, py, c))
                sends.append(cp)
                recvs.append(cp)
        else:
            cp = remote(0, src.at[:, job["layer"]], out.at[:, job["layer"]], sibling)
            sends.append(cp)
            recvs.append(cp)
        triples.append((local, sends, recvs))
    return triples


def _jobs_start(jobs, in_refs, out_refs, sems):
    for local, sends, _ in _jobs_copies(jobs, in_refs, out_refs, sems):
        for cp in local + sends:
            cp.start()


def _jobs_wait(jobs, in_refs, out_refs, sems):
    triples = _jobs_copies(jobs, in_refs, out_refs, sems)
    for local, sends, _ in triples:
        for cp in local:
            cp.wait()
        for cp in sends:
            cp.wait_send()
    for _, _, recvs in triples:
        for cp in recvs:
            cp.wait_recv()


def _run_jobs(jobs, name):
    ins, outs, alias = _jobs_io(jobs)

    def body(*refs):
        in_refs, out_refs, sems = refs[:len(ins)], refs[len(ins):len(ins) + len(outs)], refs[len(ins) + len(outs):]
        _jobs_start(jobs, in_refs, out_refs, sems)
        _jobs_wait(jobs, in_refs, out_refs, sems)

    return pl.pallas_call(
        body, name=name, in_specs=[HBM_SPEC] * len(ins), out_specs=[HBM_SPEC] * len(outs), out_shape=outs,
        scratch_shapes=_jobs_sems(jobs), input_output_aliases=alias,
    )(*ins)


_DIMS = {
    "nn": (((1,), (0,)), ((), ())),
    "nt": (((1,), (1,)), ((), ())),
    "tn": (((0,), (0,)), ((), ())),
}


def _matmul(a, b, mode, out_dtype, name, *, extras=(), post=None, jobs=(), tm=1024, tn=1024, tk=2048):
    assert a.dtype == BF16 and b.dtype == BF16
    if mode == "nn":
        (m, k), (k2, n) = a.shape, b.shape
    elif mode == "nt":
        (m, k), (n, k2) = a.shape, b.shape
    else:
        (k, m), (k2, n) = a.shape, b.shape
    assert k == k2
    tm, tn, tk = _pick(m, tm), _pick(n, tn), _pick(k, tk)
    nk = k // tk
    grid = (m // tm, n // tn, nk)
    dims = _DIMS[mode]
    out_dtypes = out_dtype if isinstance(out_dtype, (tuple, list)) else (out_dtype,)
    n_main, n_out = 2 + len(extras), len(out_dtypes)

    a_spec = pl.BlockSpec((tk, tm), lambda i, j, kk: (kk, i)) if mode == "tn" else pl.BlockSpec((tm, tk), lambda i, j, kk: (i, kk))
    b_spec = pl.BlockSpec((tn, tk), lambda i, j, kk: (j, kk)) if mode == "nt" else pl.BlockSpec((tk, tn), lambda i, j, kk: (kk, j))
    tile_spec = pl.BlockSpec((tm, tn), lambda i, j, kk: (i, j))
    extras = [e if isinstance(e, tuple) else (e, 0) for e in extras]
    extra_specs = [pl.BlockSpec((tm, tn), functools.partial(lambda i, j, kk, c: (i, j + c), c=c)) for _, c in extras]
    job_ins, job_outs, job_alias = _jobs_io(jobs)
    operands = [a, b, *[e for e, _ in extras], *job_ins]
    in_specs = [a_spec, b_spec] + extra_specs + [HBM_SPEC] * len(job_ins)
    out_shape = [jax.ShapeDtypeStruct((m, n), dt) for dt in out_dtypes] + job_outs
    out_specs = [tile_spec] * n_out + [HBM_SPEC] * len(job_outs)
    aliases = {n_main + i: n_out + o for i, o in job_alias.items()}
    scratch = ([pltpu.VMEM((tm, tn), F32)] if nk > 1 else []) + (_jobs_sems(jobs) if jobs else [])

    def body(*refs):
        a_ref, b_ref = refs[0], refs[1]
        extra_refs = refs[2:n_main]
        job_in_refs = refs[n_main:n_main + len(job_ins)]
        outs = refs[n_main + len(job_ins):]
        o_refs, job_out_refs = outs[:n_out], outs[n_out:n_out + len(job_outs)]
        rest = outs[n_out + len(job_outs):]
        acc_ref = rest[0] if nk > 1 else None
        sems = rest[1:] if nk > 1 else rest
        step = (pl.program_id(0) * grid[1] + pl.program_id(1)) * grid[2] + pl.program_id(2)
        if jobs:
            pl.when(step == 0)(lambda: _jobs_start(jobs, job_in_refs, job_out_refs, sems))

        p = lax.dot_general(a_ref[...], b_ref[...], dims, preferred_element_type=F32)

        def finish(total):
            vals = post(total, *[r[...] for r in extra_refs]) if post is not None else total
            vals = vals if isinstance(vals, (tuple, list)) else (vals,)
            for o_ref, v in zip(o_refs, vals, strict=True):
                o_ref[...] = v.astype(o_ref.dtype)

        if nk == 1:
            finish(p)
        else:
            kk = pl.program_id(2)

            @pl.when(kk == 0)
            def _():
                acc_ref[...] = p

            @pl.when(kk > 0)
            def _():
                acc_ref[...] += p

            @pl.when(kk == nk - 1)
            def _():
                finish(acc_ref[...])

        if jobs:
            pl.when(step == grid[0] * grid[1] * grid[2] - 1)(lambda: _jobs_wait(jobs, job_in_refs, job_out_refs, sems))

    res = pl.pallas_call(
        body,
        name=name,
        grid=grid,
        in_specs=in_specs,
        out_specs=out_specs,
        out_shape=out_shape,
        scratch_shapes=scratch,
        input_output_aliases=aliases,
        compiler_params=_params(("arbitrary", "arbitrary", "arbitrary") if jobs else ("parallel", "parallel", "arbitrary")),
    )(*operands)
    return res[0] if len(res) == 1 else res


def _rowwise(fn, rows, consts, outs, reds, name, *, tr=512, ncol=1, n_rows=None, jobs=()):
    if n_rows is None:
        n_rows = rows[0][0].shape[0]
    tr = min(tr, n_rows)
    assert n_rows % tr == 0
    in_specs, operands = [], []
    for arr, width, coloff, rowoff in rows:
        in_specs.append(pl.BlockSpec((tr, width), functools.partial(lambda j, i, c, r: (i + r, j + c), c=coloff, r=rowoff)))
        operands.append(arr)
    for arr in consts:
        if arr.ndim == 3:
            in_specs.append(pl.BlockSpec(arr.shape, lambda j, i: (0, 0, 0)))
        else:
            in_specs.append(pl.BlockSpec(arr.shape, lambda j, i: (0, 0)))
        operands.append(arr)
    out_specs, out_shape = [], []
    for width, dtype in outs:
        out_specs.append(pl.BlockSpec((tr, width), lambda j, i: (i, j)))
        out_shape.append(jax.ShapeDtypeStruct((n_rows, ncol * width), dtype))
    for shape in reds:
        if len(shape) == 3:
            out_specs.append(pl.BlockSpec(shape, lambda j, i: (0, 0, 0)))
            out_shape.append(jax.ShapeDtypeStruct(shape, F32))
        else:
            out_specs.append(pl.BlockSpec(shape, lambda j, i: (0, j)))
            out_shape.append(jax.ShapeDtypeStruct((shape[0], ncol * shape[1]), F32))
    n_in, n_out, n_red = len(operands), len(outs), len(reds)
    job_ins, job_outs, job_alias = _jobs_io(jobs)
    n_ji, n_jo = len(job_ins), len(job_outs)
    n_steps = ncol * (n_rows // tr)

    def body(*refs):
        job_in_refs = refs[n_in:n_in + n_ji]
        o_refs = refs[n_in + n_ji:n_in + n_ji + n_out]
        red_refs = refs[n_in + n_ji + n_out:n_in + n_ji + n_out + n_red]
        job_out_refs = refs[n_in + n_ji + n_out + n_red:n_in + n_ji + n_out + n_red + n_jo]
        sems = refs[n_in + n_ji + n_out + n_red + n_jo:]
        step = pl.program_id(0) * (n_rows // tr) + pl.program_id(1)
        if jobs:
            pl.when(step == 0)(lambda: _jobs_start(jobs, job_in_refs, job_out_refs, sems))
        vals = fn(*[r[...] for r in refs[:n_in]])
        if not isinstance(vals, (tuple, list)):
            vals = (vals,)
        assert len(vals) == n_out + n_red
        for o_ref, v in zip(o_refs, vals[:n_out]):
            o_ref[...] = v.astype(o_ref.dtype)
        if n_red:
            first = pl.program_id(1) == 0
            for o_ref, v in zip(red_refs, vals[n_out:]):
                @pl.when(first)
                def _(o_ref=o_ref, v=v):
                    o_ref[...] = v

                @pl.when(jnp.logical_not(first))
                def _(o_ref=o_ref, v=v):
                    o_ref[...] += v
        if jobs:
            pl.when(step == n_steps - 1)(lambda: _jobs_wait(jobs, job_in_refs, job_out_refs, sems))

    res = pl.pallas_call(
        body,
        name=name,
        grid=(ncol, n_rows // tr),
        in_specs=in_specs + [HBM_SPEC] * n_ji,
        out_specs=out_specs + [HBM_SPEC] * n_jo,
        out_shape=out_shape + job_outs,
        scratch_shapes=_jobs_sems(jobs) if jobs else [],
        input_output_aliases={n_in + i: n_out + n_red + o for i, o in job_alias.items()},
        compiler_params=_params(("arbitrary", "arbitrary") if jobs else ("parallel", "arbitrary")),
    )(*(operands + job_ins))
    return res


def _row(arr, width=None, coloff=0, rowoff=0):
    return (arr, arr.shape[1] if width is None else width, coloff, rowoff)


def _rms(x, g):
    return x * lax.rsqrt(jnp.mean(x * x, axis=-1, keepdims=True) + EPS) * g


def _erf(x):
    x = jnp.clip(x, -4.0, 4.0)
    x2 = x * x
    alpha = x2 * -2.72614225801306e-10 + 2.77068142495902e-08
    alpha = alpha * x2 - 2.10102402082508e-06
    alpha = alpha * x2 - 5.69250639462346e-05
    alpha = alpha * x2 - 7.34990630326855e-04
    alpha = alpha * x2 - 2.95459980854025e-03
    alpha = alpha * x2 - 1.60960333262415e-02
    beta = x2 * -1.45660718464996e-05 - 2.13374055278905e-04
    beta = beta * x2 - 1.68282697438203e-03
    beta = beta * x2 - 7.37332916720468e-03
    beta = beta * x2 - 1.42647390514189e-02
    return x * alpha / beta


def _gelu(x):
    return 0.5 * x * (1.0 + _erf(x * (2.0 ** -0.5)))


def _gelu_and_grad(x):
    cdf = 0.5 * (1.0 + _erf(x * (2.0 ** -0.5)))
    pdf = jnp.exp(-0.5 * x * x) * (2.0 * jnp.pi) ** -0.5
    return x * cdf, cdf + x * pdf


def _layer_norm(x, g, b):
    mu = jnp.mean(x, axis=-1, keepdims=True)
    xc = x - mu
    return xc * lax.rsqrt(jnp.mean(xc * xc, axis=-1, keepdims=True) + EPS) * g + b


def _per_head(fn, x, g):
    parts = []
    for h in range(HEADS):
        sl = slice(h * HEAD_DIM, (h + 1) * HEAD_DIM)
        parts.append(fn(x[:, sl], g[:, sl]))
    return parts


def _sgu_mask():
    i = lax.broadcasted_iota(jnp.int32, (SGU_LEN, SGU_LEN), 0)
    j = lax.broadcasted_iota(jnp.int32, (SGU_LEN, SGU_LEN), 1)
    return (j // CHUNK) <= (i // CHUNK)


def _sgu_mix(vln_bf, w_bf, bias):
    rows = vln_bf.shape[0]
    out_rows = []
    for c in range(rows // SGU_LEN):
        cols = []
        for g in range(GROUPS):
            blk = vln_bf[c * SGU_LEN:(c + 1) * SGU_LEN, g * GROUP_DIM:(g + 1) * GROUP_DIM]
            mixed = jnp.dot(w_bf[g], blk, preferred_element_type=F32) + bias[g]
            cols.append(mixed)
        out_rows.append(jnp.concatenate(cols, axis=1))
    return out_rows[0] if len(out_rows) == 1 else jnp.concatenate(out_rows, axis=0)


def _split_dots(xs, m_bf):
    his = [x.astype(BF16) for x in xs]
    los = [(x - hi.astype(F32)).astype(BF16) for x, hi in zip(xs, his)]
    res = jnp.dot(jnp.concatenate(his + los, axis=0), m_bf, preferred_element_type=F32)
    n, rows = len(xs), xs[0].shape[0]
    return [res[h * rows:(h + 1) * rows] + res[(n + h) * rows:(n + h + 1) * rows] for h in range(n)]


def _att_tiles(qs, kbs, run_keeps, diag, want_sig=False):
    blk = ATT_BLOCK
    row = lax.broadcasted_iota(jnp.int32, (blk, blk), 0)
    col = lax.broadcasted_iota(jnp.int32, (blk, blk), 1)
    later = (row > col).astype(BF16)
    past = col < row
    zs = [lax.dot_general(q, kb, _DIMS["nt"], preferred_element_type=F32) * (HEAD_DIM ** -0.5) for q, kb in zip(qs, kbs)]
    es = [jnp.exp(-jnp.abs(z)) for z in zs]
    softplus = [jnp.maximum(z, 0.0) + jnp.log(1.0 + e) for z, e in zip(zs, es)]
    log_keeps = [jnp.where(past, -sp, 0.0) if diag else -sp for sp in softplus]
    tails = _split_dots(log_keeps, later)
    weights = [jnp.exp((z - sp) + (tail + keep)) for z, sp, tail, keep in zip(zs, softplus, tails, run_keeps)]
    if diag:
        weights = [jnp.where(past, a, 0.0) for a in weights]
    sigs = None
    if want_sig:
        invs = [1.0 / (1.0 + e) for e in es]
        sigs = [jnp.where(z >= 0.0, inv, e * inv) for z, e, inv in zip(zs, es, invs)]
    return weights, sigs, [jnp.sum(lk, axis=1, keepdims=True) for lk in log_keeps]


def _any_alive(keeps):
    return jnp.max(functools.reduce(jnp.maximum, keeps)) >= LOG_ZERO


def _attention_fwd(qn, kn, vb, *, heads_per_step=8, jobs=()):
    t = qn.shape[0]
    blk = ATT_BLOCK
    nq = t // blk
    hp = heads_per_step
    groups, width = HEADS // hp, hp * HEAD_DIM
    heads = [slice(h * HEAD_DIM, (h + 1) * HEAD_DIM) for h in range(hp)]
    job_ins, job_outs, job_alias = _jobs_io(jobs)
    n_ji, n_jo = len(job_ins), len(job_outs)

    def body(q_ref, k_ref, v_ref, *rest):
        job_in_refs, o_ref, o32_ref = rest[:n_ji], rest[n_ji], rest[n_ji + 1]
        job_out_refs, sems = rest[n_ji + 2:n_ji + 2 + n_jo], rest[n_ji + 2 + n_jo:]
        i = pl.program_id(1)
        step = pl.program_id(0) * nq + i
        if jobs:
            pl.when(step == 0)(lambda: _jobs_start(jobs, job_in_refs, job_out_refs, sems))
        qs = [q_ref[:, sl] for sl in heads]

        def tile(j, keeps, accs, diag):
            rows = pl.ds(pl.multiple_of(j * blk, blk), blk)
            weights, _, keep_sums = _att_tiles(qs, [k_ref[rows, sl] for sl in heads], keeps, diag)
            accs = [acc + jnp.dot(a.astype(BF16), v_ref[rows, sl], preferred_element_type=F32)
                    for acc, a, sl in zip(accs, weights, heads)]
            return tuple(k + s for k, s in zip(keeps, keep_sums)), tuple(accs)

        zero = tuple(jnp.zeros((blk, 1), F32) for _ in heads)
        keeps, accs = tile(i, zero, tuple(jnp.zeros((blk, HEAD_DIM), F32) for _ in heads), True)

        def cond(c):
            return jnp.logical_and(c[0] >= 0, _any_alive(c[1]))

        def step_fn(c):
            j, keeps, accs = c
            keeps, accs = tile(j, keeps, accs, False)
            return j - 1, keeps, accs

        _, _, accs = lax.while_loop(cond, step_fn, (i - 1, keeps, accs))
        for h, sl in enumerate(heads):
            o_ref[:, sl] = accs[h].astype(o_ref.dtype)
            o32_ref[:, sl] = accs[h]
        if jobs:
            pl.when(step == groups * nq - 1)(lambda: _jobs_wait(jobs, job_in_refs, job_out_refs, sems))

    blk_spec = pl.BlockSpec((blk, width), lambda g, i: (i, g))
    head_spec = pl.BlockSpec((t, width), lambda g, i: (0, g), pipeline_mode=pl.Buffered(1))
    return pl.pallas_call(
        body,
        name="attention_fwd",
        grid=(groups, nq),
        in_specs=[blk_spec, head_spec, head_spec] + [HBM_SPEC] * n_ji,
        out_specs=[blk_spec, blk_spec] + [HBM_SPEC] * n_jo,
        out_shape=[jax.ShapeDtypeStruct((t, SB_WIDTH), BF16), jax.ShapeDtypeStruct((t, SB_WIDTH), F32)] + job_outs,
        scratch_shapes=_jobs_sems(jobs) if jobs else [],
        input_output_aliases={3 + i: 2 + o for i, o in job_alias.items()},
        compiler_params=_params(("arbitrary", "arbitrary")),
    )(qn, kn, vb, *job_ins)


def _attention_bwd(qn, kn, vb, do, o32, *, heads_per_step=4, jobs=()):
    t = qn.shape[0]
    blk = ATT_BLOCK
    nq = t // blk
    hp = heads_per_step
    groups, width = HEADS // hp, hp * HEAD_DIM
    heads = [slice(h * HEAD_DIM, (h + 1) * HEAD_DIM) for h in range(hp)]
    job_ins, job_outs, job_alias = _jobs_io(jobs)
    n_ji, n_jo = len(job_ins), len(job_outs)

    def body(q_ref, k_ref, v_ref, do_ref, o32_ref, *rest):
        job_in_refs = rest[:n_ji]
        dq_ref, dk_ref, dv_ref = rest[n_ji:n_ji + 3]
        job_out_refs, sems = rest[n_ji + 3:n_ji + 3 + n_jo], rest[n_ji + 3 + n_jo:]
        i = pl.program_id(1)
        step = pl.program_id(0) * nq + i
        if jobs:
            pl.when(step == 0)(lambda: _jobs_start(jobs, job_in_refs, job_out_refs, sems))

        @pl.when(i == 0)
        def _():
            dk_ref[...] = jnp.zeros_like(dk_ref)
            dv_ref[...] = jnp.zeros_like(dv_ref)

        qs = [q_ref[:, sl] for sl in heads]
        dos = [do_ref[:, sl] for sl in heads]
        totals = [jnp.sum(d.astype(F32) * o32_ref[:, sl], axis=1, keepdims=True) for d, sl in zip(dos, heads)]
        row = lax.broadcasted_iota(jnp.int32, (blk, blk), 0)
        col = lax.broadcasted_iota(jnp.int32, (blk, blk), 1)
        from_here = (row >= col).astype(BF16)

        def tile(j, keeps, run_dlas, dqs, diag):
            rows = pl.ds(pl.multiple_of(j * blk, blk), blk)
            kbs = [k_ref[rows, sl] for sl in heads]
            weights, sigs, keep_sums = _att_tiles(qs, kbs, keeps, diag, want_sig=True)
            wbs = [a.astype(BF16) for a in weights]
            dps = [lax.dot_general(d, v_ref[rows, sl], _DIMS["nt"], preferred_element_type=F32) for d, sl in zip(dos, heads)]
            dlas = [wb.astype(F32) * dp for wb, dp in zip(wbs, dps)]
            later_sums = _split_dots(dlas, from_here)
            dzbs = []
            for dla, sig, total, later, run in zip(dlas, sigs, totals, later_sums, run_dlas):
                dz = dla * (1.0 - sig) - sig * (total - (later + run))
                if diag:
                    dz = jnp.where(col < row, dz, 0.0)
                dzbs.append((dz * (HEAD_DIM ** -0.5)).astype(BF16))
            dqs = [dq + jnp.dot(dzb, kb, preferred_element_type=F32) for dq, dzb, kb in zip(dqs, dzbs, kbs)]
            for sl, dzb, wb, q, d in zip(heads, dzbs, wbs, qs, dos):
                dk_ref[rows, sl] += lax.dot_general(dzb, q, _DIMS["tn"], preferred_element_type=F32)
                dv_ref[rows, sl] += lax.dot_general(wb, d, _DIMS["tn"], preferred_element_type=F32)
            keeps = tuple(k + s for k, s in zip(keeps, keep_sums))
            run_dlas = tuple(r + jnp.sum(dla, axis=1, keepdims=True) for r, dla in zip(run_dlas, dlas))
            return keeps, run_dlas, tuple(dqs)

        zero = tuple(jnp.zeros((blk, 1), F32) for _ in heads)
        keeps, run_dlas, dqs = tile(i, zero, zero, tuple(jnp.zeros((blk, HEAD_DIM), F32) for _ in heads), True)

        def cond(c):
            return jnp.logical_and(c[0] >= 0, _any_alive(c[1]))

        def step_fn(c):
            j, keeps, run_dlas, dqs = c
            keeps, run_dlas, dqs = tile(j, keeps, run_dlas, dqs, False)
            return j - 1, keeps, run_dlas, dqs

        _, _, _, dqs = lax.while_loop(cond, step_fn, (i - 1, keeps, run_dlas, dqs))
        for h, sl in enumerate(heads):
            dq_ref[:, sl] = dqs[h]
        if jobs:
            pl.when(step == groups * nq - 1)(lambda: _jobs_wait(jobs, job_in_refs, job_out_refs, sems))

    blk_spec = pl.BlockSpec((blk, width), lambda g, i: (i, g))
    head_spec = pl.BlockSpec((t, width), lambda g, i: (0, g), pipeline_mode=pl.Buffered(1))
    full = jax.ShapeDtypeStruct((t, SB_WIDTH), F32)
    return pl.pallas_call(
        body,
        name="attention_bwd",
        grid=(groups, nq),
        in_specs=[blk_spec, head_spec, head_spec, blk_spec, blk_spec] + [HBM_SPEC] * n_ji,
        out_specs=[blk_spec, head_spec, head_spec] + [HBM_SPEC] * n_jo,
        out_shape=[full, full, full] + job_outs,
        scratch_shapes=_jobs_sems(jobs) if jobs else [],
        input_output_aliases={5 + i: 3 + o for i, o in job_alias.items()},
        compiler_params=_params(("arbitrary", "arbitrary"), ATT_BWD_VMEM_LIMIT),
    )(qn, kn, vb, do, o32, *job_ins)


def _sum_over_devices(v, jobs=()):
    rows, width = v.shape
    job_ins, job_outs, job_alias = _jobs_io(jobs)
    n_ji, n_jo = len(job_ins), len(job_outs)

    def body(v_ref, *rest):
        job_in_refs, o_ref = rest[:n_ji], rest[n_ji]
        job_out_refs = rest[n_ji + 1:n_ji + 1 + n_jo]
        buf, send_sems, recv_sems = rest[n_ji + 1 + n_jo:n_ji + 4 + n_jo]
        job_sems = rest[n_ji + 4 + n_jo:]
        if jobs:
            _jobs_start(jobs, job_in_refs, job_out_refs, job_sems)
        x, y, c = _place()
        me = 4 * x + 2 * y + c
        buf[me] = v_ref[...]
        sibling = (x, y, 1 - c)
        chips = _other_chips(x, y)

        def copy(k, block, to, src=None):
            slot = buf.at[4 * block[0] + 2 * block[1] + block[2]]
            return pltpu.make_async_remote_copy(
                src_ref=slot if src is None else src, dst_ref=slot, send_sem=send_sems.at[k], recv_sem=recv_sems.at[k],
                device_id=to, device_id_type=MESH)

        first = [copy(0, (x, y, c), sibling, src=v_ref)]
        first += [copy(1 + j, (x, y, c), (px, py, c), src=v_ref) for j, (px, py) in enumerate(chips)]
        for cp in first:
            cp.start()
        passed = [copy(4 + j, (px, py, c), sibling) for j, (px, py) in enumerate(chips)]
        for j, (px, py) in enumerate(chips):
            copy(1 + j, (px, py, c), (x, y, c)).wait_recv()
            passed[j].start()
        copy(0, (x, y, 1 - c), (x, y, c)).wait_recv()
        for j, (px, py) in enumerate(chips):
            copy(4 + j, (px, py, 1 - c), (x, y, c)).wait_recv()
        for cp in first + passed:
            cp.wait_send()
        total = buf[0]
        for d in range(1, N_DEV):
            total = total + buf[d]
        o_ref[...] = total
        if jobs:
            _jobs_wait(jobs, job_in_refs, job_out_refs, job_sems)

    vmem = pl.BlockSpec(memory_space=pltpu.VMEM)
    return pl.pallas_call(
        body,
        name="sum_over_devices",
        in_specs=[vmem] + [HBM_SPEC] * n_ji,
        out_specs=[vmem] + [HBM_SPEC] * n_jo,
        out_shape=[jax.ShapeDtypeStruct((rows, width), F32)] + job_outs,
        scratch_shapes=[pltpu.VMEM((N_DEV, rows, width), F32), pltpu.SemaphoreType.DMA((N_DEV - 1,)), pltpu.SemaphoreType.DMA((N_DEV - 1,))]
        + (_jobs_sems(jobs) if jobs else []),
        input_output_aliases={1 + i: 1 + o for i, o in job_alias.items()},
        compiler_params=pltpu.CompilerParams(vmem_limit_bytes=V7X_VMEM_LIMIT),
    )(v, *job_ins)


def _adamw_math(w, g, m, v):
    m = ADAM_B1 * m + (1.0 - ADAM_B1) * g
    v = ADAM_B2 * v + (1.0 - ADAM_B2) * (g * g)
    m_hat = m / (1.0 - ADAM_B1 ** ADAM_STEP)
    v_hat = v / (1.0 - ADAM_B2 ** ADAM_STEP)
    delta = -ADAM_LR * (m_hat / (jnp.sqrt(v_hat) + ADAM_EPS) + ADAM_WD * w)
    return delta, m, v


def _adamw_sharded(mine, theirs, w, m, v, name):
    width = _pick(w.shape[1], 1152)
    ncol = w.shape[1] // width
    shard_rows = w.shape[0]
    tr = min(256, shard_rows)

    def slots(arr):
        return [_row(arr, width, 0, s * (shard_rows // tr)) for s in range(4)]

    def total(a, b, c, d):
        return ((a.astype(F32) + b.astype(F32)) + c.astype(F32)) + d.astype(F32)

    def fn(a0, a1, a2, a3, b0, b1, b2, b3, w, m, v):
        g = total(a0, a1, a2, a3) + total(b0, b1, b2, b3)
        delta, m, v = _adamw_math(w, g, m, v)
        return g, delta, m, v

    return _rowwise(fn, slots(mine) + slots(theirs) + [_row(t, width) for t in (w, m, v)], [], [(width, F32)] * 4, [], name,
                    tr=tr, ncol=ncol, n_rows=shard_rows)


def _adamw_small(g, w, m, v):
    def fn(g, w, m, v):
        return _adamw_math(w, g, m, v)

    return _rowwise(fn, [_row(t) for t in (g, w, m, v)], [], [(g.shape[1], F32)] * 3, [], "adamw_small", tr=g.shape[0])


def _flat(a):
    return a.reshape(-1, a.shape[-1])


def kernel(x, g_mix, w_in, g_q, g_k, sgu_ln_g, sgu_ln_b, w_spatial, b_spatial, w_oa, w_ob, w_out, g_ff, w_ff1, w_ff2, loss_target, m_g_mix, m_w_in, m_g_q, m_g_k, m_sgu_ln_g, m_sgu_ln_b, m_w_spatial, m_b_spatial, m_w_oa, m_w_ob, m_w_out, m_g_ff, m_w_ff1, m_w_ff2, v_g_mix, v_w_in, v_g_q, v_g_k, v_sgu_ln_g, v_sgu_ln_b, v_w_spatial, v_b_spatial, v_w_oa, v_w_ob, v_w_out, v_g_ff, v_w_ff1, v_w_ff2):
    depth = g_mix.shape[0]
    seq, d_model = x.shape[1], x.shape[2]
    d_ff = w_ff1.shape[2] * N_CHIPS
    in_cols = w_in.shape[2] * N_CHIPS
    col_gate_a = 3 * SB_WIDTH + 2 * SGU_WIDTH
    assert in_cols == col_gate_a + 2 * d_model

    big = [w_in, w_oa, w_ob, w_out, w_ff1, w_ff2]
    big_m = [m_w_in, m_w_oa, m_w_ob, m_w_out, m_w_ff1, m_w_ff2]
    big_v = [v_w_in, v_w_oa, v_w_ob, v_w_out, v_w_ff1, v_w_ff2]
    axes = [2, 2, 2, 1, 2, 1]
    W_IN, W_OA, W_OB, W_OUT, W_FF1, W_FF2 = range(6)
    shards = [w.astype(BF16) for w in big]
    full = [[None] * depth for _ in big]

    def gather(t, l):
        return _gather_job(shards[t], l, axes[t])

    def forward(t, l):
        return _forward_job(full[t][l], axes[t])

    def add_residual(total, res):
        return total + res

    (full[W_IN][0],) = _run_jobs([gather(W_IN, 0)], "gather_first")
    (full[W_IN][0],) = _run_jobs([forward(W_IN, 0)], "forward_first")

    gate_w = _pick(d_model, 1024)
    bias_col = b_spatial[..., None]

    def rms_fwd(xin, g, name):
        return _rowwise(lambda xb, gb: _rms(xb, gb), [_row(xin)], [g], [(d_model, BF16)], [], name)[0]

    saved = []
    cur = x.reshape(seq, d_model)
    for l in range(depth):
        hb = rms_fwd(cur, g_mix[l:l + 1], "rms_mix")
        early = [W_FF1, W_OA, W_OB, W_OUT]
        proj, *got = _matmul(hb, full[W_IN][l], "nn", F32, "proj", jobs=[gather(t, l) for t in early])
        for t, w in zip(early, got):
            full[t][l] = w

        def qk_fn(q, k, v, gq, gk):
            qn = jnp.concatenate(_per_head(_rms, q, gq), axis=1)
            kn = jnp.concatenate(_per_head(_rms, k, gk), axis=1)
            return qn, kn, v

        gq, gk = g_q[l].reshape(1, SB_WIDTH), g_k[l].reshape(1, SB_WIDTH)
        qn, kn, vb = _rowwise(qk_fn, [_row(proj, SB_WIDTH, 0), _row(proj, SB_WIDTH, 1), _row(proj, SB_WIDTH, 2)],
                              [gq, gk], [(SB_WIDTH, BF16)] * 3, [], "qk_norm")
        o, o32, *got = _attention_fwd(qn, kn, vb, jobs=[forward(t, l) for t in early] + [gather(W_FF2, l)])
        for t, w in zip(early + [W_FF2], got):
            full[t][l] = w
        ya = _matmul(o, full[W_OA][l], "nn", F32, "proj_a")

        def sgu_fn(u_pre, v_pre, ln_g, ln_b, w_s, b_t):
            w_bf = jnp.where(_sgu_mask()[None], w_s, 0.0).astype(BF16)
            vln = _layer_norm(_gelu(v_pre), ln_g, ln_b)
            return _gelu(u_pre) * _sgu_mix(vln.astype(BF16), w_bf, b_t)

        sgu_consts = [sgu_ln_g[l:l + 1], sgu_ln_b[l:l + 1], w_spatial[l], bias_col[l]]
        s = _rowwise(sgu_fn, [_row(proj, SGU_WIDTH, 3), _row(proj, SGU_WIDTH, 4)], sgu_consts, [(SGU_WIDTH, BF16)], [], "sgu_fwd")[0]
        def merge_fn(b, ga, gb, a):
            return b, jax.nn.sigmoid(ga) * a + jax.nn.sigmoid(gb) * b

        gates = [(proj, col_gate_a // gate_w), (proj, (col_gate_a + d_model) // gate_w)]
        yb, merged = _matmul(s, full[W_OB][l], "nn", (F32, BF16), "proj_b", extras=gates + [ya], post=merge_fn, tm=512, tn=gate_w)
        x1 = _matmul(merged, full[W_OUT][l], "nn", F32, "proj_out", extras=[cur], post=add_residual)
        h2 = rms_fwd(x1, g_ff[l:l + 1], "rms_ff")
        more = l + 1 < depth
        a1, r, full[W_FF2][l], *got = _matmul(h2, full[W_FF1][l], "nn", (F32, BF16), "ff1",
                                              jobs=[forward(W_FF2, l)] + ([gather(W_IN, l + 1)] if more else []),
                                              post=lambda total: (total, jnp.square(jnp.maximum(total, 0.0))))
        if more:
            full[W_IN][l + 1] = got[0]
            x2, full[W_IN][l + 1] = _matmul(r, full[W_FF2][l], "nn", F32, "ff2", extras=[x1], post=add_residual,
                                            jobs=[forward(W_IN, l + 1)])
        else:
            x2 = _matmul(r, full[W_FF2][l], "nn", F32, "ff2", extras=[x1], post=add_residual)
        saved.append(dict(x=cur, hb=hb, proj=proj, qn=qn, kn=kn, vb=vb, o=o, o32=o32, ya=ya, yb=yb, s=s, merged=merged,
                          x1=x1, h2=h2, a1=a1, r=r, gq=gq, gk=gk, sgu_consts=sgu_consts, gates=gates))
        cur = x2

    def loss_fn(y, target):
        err = y - target
        per_row = jnp.sum(err * err, axis=-1, keepdims=True) * (1.0 / d_model)
        part = 0.5 * jnp.sum(per_row, axis=0, keepdims=True)
        dy = err * (1.0 / d_model)
        return dy, dy, jnp.broadcast_to(part, (8, LANE))

    dx, dxb, loss_part = _rowwise(loss_fn, [_row(cur), _row(loss_target.reshape(seq, d_model))], [],
                                  [(d_model, F32), (d_model, BF16)], [(8, LANE)], "loss")
    loss = lax.psum(loss_part[0, 0], ("x", "y", "c"))

    landed = [None] * len(big)

    theirs = [None] * len(big)

    def scatter(t, grad, l):
        return _scatter_job(grad, landed[t], (4,) + shards[t].shape, l, axes[t])

    def mirror(t, l):
        return _mirror_job(landed[t], theirs[t], l)

    small = {n: [None] * depth for n in ("g_mix", "g_q", "g_k", "ln_g", "ln_b", "w_s", "b_s", "g_ff")}

    def rms_bwd(dh, xin, dres, g, name, jobs=()):
        def fn(dh, xin, dres, g):
            _, vjp = jax.vjp(_rms, xin, g)
            dxin, dg = vjp(dh)
            total = dres + dxin
            return total, total, dg

        return _rowwise(fn, [_row(dh), _row(xin), _row(dres)], [g], [(d_model, F32), (d_model, BF16)], [(1, d_model)], name, jobs=jobs)

    for l in reversed(range(depth)):
        sv = saved[l]
        da1 = _matmul(dxb, full[W_FF2][l], "nt", BF16, "ff2_dx", extras=[sv["a1"]],
                      post=lambda total, a: total * (2.0 * jnp.maximum(a, 0.0)),
                      jobs=[mirror(t, l + 1) for t in (W_IN, W_FF1)] if l + 1 < depth else [])
        grad_ff2 = _matmul(sv["r"], dxb, "tn", BF16, "ff2_dw", tk=WGRAD_TK,
                           jobs=[mirror(t, l + 1) for t in (W_FF2, W_OUT, W_OA, W_OB)] if l + 1 < depth else [])
        if l + 1 < depth:
            da1, theirs[W_IN], theirs[W_FF1] = da1
            grad_ff2, theirs[W_FF2], theirs[W_OUT], theirs[W_OA], theirs[W_OB] = grad_ff2
        dh2 = _matmul(da1, full[W_FF1][l], "nt", F32, "ff1_dx")
        grad_ff1 = _matmul(sv["h2"], da1, "tn", BF16, "ff1_dw", tk=WGRAD_TK)
        dx1, dx1b, small["g_ff"][l] = rms_bwd(dh2, sv["x1"], dx, g_ff[l:l + 1], "rms_ff_bwd")

        def merge_bwd_fn(dm, ga, gb, a, b):
            sa, sb = jax.nn.sigmoid(ga), jax.nn.sigmoid(gb)
            return dm * a * sa * (1.0 - sa), dm * b * sb * (1.0 - sb), dm * sa, dm * sb

        dga, dgb, dya, dyb = _matmul(dx1b, full[W_OUT][l], "nt", (BF16,) * 4, "out_dx", tm=512, tn=gate_w,
                                     extras=sv["gates"] + [sv["ya"], sv["yb"]], post=merge_bwd_fn)
        grad_out = _matmul(sv["merged"], dx1b, "tn", BF16, "out_dw", tk=WGRAD_TK)

        ds = _matmul(dyb, full[W_OB][l], "nt", F32, "ob_dx")
        grad_ob = _matmul(sv["s"], dyb, "tn", BF16, "ob_dw", tk=WGRAD_TK)

        def sgu_bwd_fn(ds, u_pre, v_pre, ln_g, ln_b, w_s, b_t):
            mask = _sgu_mask()
            w_bf = jnp.where(mask[None], w_s, 0.0).astype(BF16)
            u, u_grad = _gelu_and_grad(u_pre)
            vg, vg_grad = _gelu_and_grad(v_pre)
            vln, ln_vjp = jax.vjp(_layer_norm, vg, ln_g, ln_b)
            vln_bf = vln.astype(BF16)
            mixed = _sgu_mix(vln_bf, w_bf, b_t)
            du_pre = ds * mixed * u_grad
            dmixed = ds * u
            dm_bf = dmixed.astype(BF16)
            dw = [jnp.zeros((SGU_LEN, SGU_LEN), F32) for _ in range(GROUPS)]
            db = [jnp.zeros((SGU_LEN, 1), F32) for _ in range(GROUPS)]
            dvln_rows = []
            for c in range(ds.shape[0] // SGU_LEN):
                rows = slice(c * SGU_LEN, (c + 1) * SGU_LEN)
                cols = []
                for g in range(GROUPS):
                    sl = slice(g * GROUP_DIM, (g + 1) * GROUP_DIM)
                    cols.append(lax.dot_general(w_bf[g], dm_bf[rows, sl], _DIMS["tn"], preferred_element_type=F32))
                    dw[g] = dw[g] + lax.dot_general(dm_bf[rows, sl], vln_bf[rows, sl], _DIMS["nt"], preferred_element_type=F32)
                    db[g] = db[g] + jnp.sum(dmixed[rows, sl], axis=1, keepdims=True)
                dvln_rows.append(jnp.concatenate(cols, axis=1))
            dvln = dvln_rows[0] if len(dvln_rows) == 1 else jnp.concatenate(dvln_rows, axis=0)
            dvg, dln_g, dln_b = ln_vjp(dvln)
            dv_pre = dvg * vg_grad
            dw_s = jnp.stack([jnp.where(mask, d, 0.0) for d in dw])
            return du_pre, dv_pre, dln_g, dln_b, dw_s, jnp.stack(db)

        du, dvs, small["ln_g"][l], small["ln_b"][l], small["w_s"][l], db_col = _rowwise(
            sgu_bwd_fn, [_row(ds), _row(sv["proj"], SGU_WIDTH, 3), _row(sv["proj"], SGU_WIDTH, 4)], sv["sgu_consts"],
            [(SGU_WIDTH, BF16)] * 2, [(1, SGU_WIDTH), (1, SGU_WIDTH), (GROUPS, SGU_LEN, SGU_LEN), (GROUPS, SGU_LEN, 1)], "sgu_bwd", tr=256)
        small["b_s"][l] = db_col[..., 0]

        do = _matmul(dya, full[W_OA][l], "nt", BF16, "oa_dx")
        grad_oa = _matmul(sv["o"], dya, "tn", BF16, "oa_dw", tk=WGRAD_TK)
        dqn, dkn, dv, landed[W_FF2], landed[W_OUT], landed[W_OB], landed[W_OA] = _attention_bwd(
            sv["qn"], sv["kn"], sv["vb"], do, sv["o32"],
            jobs=[scatter(W_FF2, grad_ff2, l), scatter(W_OUT, grad_out, l), scatter(W_OB, grad_ob, l), scatter(W_OA, grad_oa, l)])

        def qk_bwd_fn(dqn, dkn, dv, q, k, gq, gk):
            outs = []
            for d, xin, g in ((dqn, q, gq), (dkn, k, gk)):
                dxs, dgs = [], []
                for h in range(HEADS):
                    sl = slice(h * HEAD_DIM, (h + 1) * HEAD_DIM)
                    _, vjp = jax.vjp(_rms, xin[:, sl], g[:, sl])
                    dxh, dgh = vjp(d[:, sl])
                    dxs.append(dxh)
                    dgs.append(dgh)
                outs.append((jnp.concatenate(dxs, axis=1), jnp.concatenate(dgs, axis=1)))
            return outs[0][0], outs[1][0], dv, outs[0][1], outs[1][1]

        dq, dk, dvb, small["g_q"][l], small["g_k"][l] = _rowwise(
            qk_bwd_fn, [_row(dqn), _row(dkn), _row(dv), _row(sv["proj"], SB_WIDTH, 0), _row(sv["proj"], SB_WIDTH, 1)],
            [sv["gq"], sv["gk"]], [(SB_WIDTH, BF16)] * 3, [(1, SB_WIDTH), (1, SB_WIDTH)], "qk_norm_bwd")

        dproj = jnp.concatenate([dq, dk, dvb, du, dvs, dga, dgb], axis=1)
        grad_in, landed[W_FF1] = _matmul(sv["hb"], dproj, "tn", BF16, "in_dw", tk=WGRAD_TK, jobs=[scatter(W_FF1, grad_ff1, l)])
        early = (W_FF2, W_OUT, W_OA, W_OB) if l == 0 else ()
        dh, landed[W_IN], *got = _matmul(dproj, full[W_IN][l], "nt", F32, "in_dx", tk=3072,
                                         jobs=[scatter(W_IN, grad_in, l)] + [mirror(t, 0) for t in early])
        for t, b in zip(early, got):
            theirs[t] = b
        dx, dxb, small["g_mix"][l], *got = rms_bwd(dh, sv["x"], dx1, g_mix[l:l + 1], "rms_mix_bwd",
                                                   jobs=[mirror(W_FF1, 0)] if l == 0 else [])
        if got:
            theirs[W_FF1] = got[0]

    names = ["g_mix", "g_q", "g_k", "ln_g", "ln_b", "w_s", "b_s", "g_ff"]
    small_w = [g_mix, g_q, g_k, sgu_ln_g, sgu_ln_b, w_spatial, b_spatial, g_ff]
    small_m = [m_g_mix, m_g_q, m_g_k, m_sgu_ln_g, m_sgu_ln_b, m_w_spatial, m_b_spatial, m_g_ff]
    small_v = [v_g_mix, v_g_q, v_g_k, v_sgu_ln_g, v_sgu_ln_b, v_w_spatial, v_b_spatial, v_g_ff]

    def pack(parts):
        return jnp.concatenate([p.reshape(-1, LANE) for p in parts], axis=0)

    local_small = pack([jnp.stack(small[n]) for n in names])
    g_small, theirs[W_IN] = _sum_over_devices(local_small, [mirror(W_IN, 0)])
    d_small, m_small, v_small = _adamw_small(g_small, pack(small_w), pack(small_m), pack(small_v))

    big_out = []
    for t in range(len(big)):
        res = _adamw_sharded(_flat(landed[t]), _flat(theirs[t]), _flat(big[t]), _flat(big_m[t]), _flat(big_v[t]), "adamw")
        big_out.append([r.reshape(big[t].shape) for r in res])

    def unpack(packed):
        outs, row = [], 0
        for w in small_w:
            n = w.size // LANE
            outs.append(packed[row:row + n].reshape(w.shape))
            row += n
        return outs

    small_out = [unpack(p) for p in (g_small, d_small, m_small, v_small)]

    order = [("s", 0), ("b", W_IN), ("s", 1), ("s", 2), ("s", 3), ("s", 4), ("s", 5), ("s", 6),
             ("b", W_OA), ("b", W_OB), ("b", W_OUT), ("s", 7), ("b", W_FF1), ("b", W_FF2)]
    result = [loss, dx.reshape(x.shape)]
    for kind in range(4):
        for which, idx in order:
            result.append(small_out[kind][idx] if which == "s" else big_out[idx][kind])
    return tuple(result)
```

```python
import functools

import jax
import jax.numpy as jnp
from jax import lax
from jax.experimental import pallas as pl
from jax.experimental.pallas import tpu as pltpu

F32 = jnp.float32
BF16 = jnp.bfloat16
MESH = pl.DeviceIdType.MESH

EPS = 1e-6
HEADS = 8
HEAD_DIM = 128
SB_WIDTH = HEADS * HEAD_DIM
GROUPS = 8
GROUP_DIM = 128
SGU_WIDTH = GROUPS * GROUP_DIM
SGU_LEN = 128
CHUNK = 64
ATT_BLOCK = 128
LOG_ZERO = -104.0

ADAM_LR = 0.001
ADAM_B1 = 0.9
ADAM_B2 = 0.999
ADAM_EPS = 1e-08
ADAM_WD = 0.01
ADAM_STEP = 10

N_CHIPS = 4
N_DEV = 8
V7X_VMEM_LIMIT = 48 * 1024 * 1024
ATT_BWD_VMEM_LIMIT = 56 * 1024 * 1024
WGRAD_TK = 2048
LANE = 128


def _params(sem, vmem_limit=V7X_VMEM_LIMIT):
    return pltpu.CompilerParams(dimension_semantics=sem, vmem_limit_bytes=vmem_limit)


def _pick(dim, pref):
    if dim <= pref:
        return dim
    for t in range(pref - pref % LANE, 0, -LANE):
        if dim % t == 0:
            return t
    raise ValueError(f"no tile for {dim}")


HBM_SPEC = pl.BlockSpec(memory_space=pl.ANY)


def _place():
    return lax.axis_index("x"), lax.axis_index("y"), lax.axis_index("c")


def _other_chips(x, y):
    return [(1 - x, y), (x, 1 - y), (1 - x, 1 - y)]


def _shard_view(ref, axis, index):
    size = ref.shape[axis] // N_CHIPS
    start = pl.multiple_of(index * size, size)
    if axis == 0:
        return ref.at[pl.ds(start, size), :]
    return ref.at[:, pl.ds(start, size)]


def _half_view(view, half):
    rows = view.shape[0] // 2
    return view.at[pl.ds(pl.multiple_of(half * rows, 8), rows), :]


def _gather_job(shards, layer, axis):
    return dict(kind="gather", src=shards, layer=layer, axis=axis - 1)


def _forward_job(weight, axis):
    return dict(kind="forward", src=weight, axis=axis - 1)


def _scatter_job(grad, landed, shape, layer, axis):
    return dict(kind="scatter", src=grad, buf=landed, shape=shape, layer=layer, axis=axis - 1)


def _mirror_job(landed, theirs, layer):
    return dict(kind="mirror", src=landed, buf=theirs, layer=layer)


def _jobs_io(jobs):
    ins, outs, alias = [], [], {}
    for job in jobs:
        src = job["src"]
        ins.append(src)
        if job["kind"] == "gather":
            shape = list(src.shape[1:])
            shape[job["axis"]] *= N_CHIPS
            outs.append(jax.ShapeDtypeStruct(tuple(shape), src.dtype))
        elif job["kind"] == "forward":
            alias[len(ins) - 1] = len(outs)
            outs.append(jax.ShapeDtypeStruct(src.shape, src.dtype))
        elif job["kind"] == "scatter":
            if job["buf"] is not None:
                ins.append(job["buf"])
                alias[len(ins) - 1] = len(outs)
            outs.append(jax.ShapeDtypeStruct(job["shape"], src.dtype))
        else:
            if job["buf"] is not None:
                ins.append(job["buf"])
                alias[len(ins) - 1] = len(outs)
            outs.append(jax.ShapeDtypeStruct(src.shape, src.dtype))
    return ins, outs, alias


def _jobs_sems(jobs):
    n = len(jobs)
    return [pltpu.SemaphoreType.DMA((3 * n,)), pltpu.SemaphoreType.DMA((3 * n,)), pltpu.SemaphoreType.DMA((n,))]


def _jobs_copies(jobs, in_refs, out_refs, sems):
    send_sems, recv_sems, local_sems = sems
    x, y, c = _place()
    me = 2 * x + y
    sibling = (x, y, 1 - c)
    triples, ip = [], 0
    for n, (job, out) in enumerate(zip(jobs, out_refs)):
        kind = job["kind"]
        src = in_refs[ip]
        ip += 2 if (kind in ("scatter", "mirror") and job["buf"] is not None) else 1

        def remote(k, src_ref, dst_ref, to):
            return pltpu.make_async_remote_copy(
                src_ref=src_ref, dst_ref=dst_ref, send_sem=send_sems.at[3 * n + k], recv_sem=recv_sems.at[3 * n + k],
                device_id=to, device_id_type=MESH)

        local, sends, recvs = [], [], []
        if kind == "gather":
            axis = job["axis"]
            mine = src.at[job["layer"]]
            local.append(pltpu.make_async_copy(mine, _shard_view(out, axis, me), local_sems.at[n]))
            for k, (px, py) in enumerate(_other_chips(x, y)):
                sends.append(remote(k, _half_view(mine, c), _half_view(_shard_view(out, axis, me), c), (px, py, c)))
                recvs.append(remote(k, _half_view(mine, c), _half_view(_shard_view(out, axis, 2 * px + py), c), (px, py, c)))
        elif kind == "forward":
            axis = job["axis"]
            for k, (px, py) in enumerate(_other_chips(x, y)):
                got = _shard_view(src, axis, 2 * px + py)
                lands = _shard_view(out, axis, 2 * px + py)
                sends.append(remote(k, _half_view(got, c), _half_view(lands, c), sibling))
                recvs.append(remote(k, _half_view(got, c), _half_view(lands, 1 - c), sibling))
        elif kind == "scatter":
            axis, layer = job["axis"], job["layer"]
            local.append(pltpu.make_async_copy(_shard_view(src, axis, me), out.at[3, layer], local_sems.at[n]))
            for k, (px, py) in enumerate(_other_chips(x, y)):
                cp = remote(k, _shard_view(src, axis, 2 * px + py), out.at[k, layer], (px, py, c))
                sends.append(cp)
                recvs.append(cp)
        else:
            cp = remote(0, src.at[:, job["layer"]], out.at[:, job["layer"]], sibling)
            sends.append(cp)
            recvs.append(cp)
        triples.append((local, sends, recvs))
    return triples


def _jobs_start(jobs, in_refs, out_refs, sems):
    for local, sends, _ in _jobs_copies(jobs, in_refs, out_refs, sems):
        for cp in local + sends:
            cp.start()


def _jobs_wait(jobs, in_refs, out_refs, sems):
    triples = _jobs_copies(jobs, in_refs, out_refs, sems)
    for local, sends, _ in triples:
        for cp in local:
            cp.wait()
        for cp in sends:
            cp.wait_send()
    for _, _, recvs in triples:
        for cp in recvs:
            cp.wait_recv()


def _run_jobs(jobs, name):
    ins, outs, alias = _jobs_io(jobs)

    def body(*refs):
        in_refs, out_refs, sems = refs[:len(ins)], refs[len(ins):len(ins) + len(outs)], refs[len(ins) + len(outs):]
        _jobs_start(jobs, in_refs, out_refs, sems)
        _jobs_wait(jobs, in_refs, out_refs, sems)

    return pl.pallas_call(
        body, name=name, in_specs=[HBM_SPEC] * len(ins), out_specs=[HBM_SPEC] * len(outs), out_shape=outs,
        scratch_shapes=_jobs_sems(jobs), input_output_aliases=alias,
    )(*ins)


_DIMS = {
    "nn": (((1,), (0,)), ((), ())),
    "nt": (((1,), (1,)), ((), ())),
    "tn": (((0,), (0,)), ((), ())),
}


def _matmul(a, b, mode, out_dtype, name, *, extras=(), post=None, jobs=(), tm=1024, tn=1024, tk=2048):
    assert a.dtype == BF16 and b.dtype == BF16
    if mode == "nn":
        (m, k), (k2, n) = a.shape, b.shape
    elif mode == "nt":
        (m, k), (n, k2) = a.shape, b.shape
    else:
        (k, m), (k2, n) = a.shape, b.shape
    assert k == k2
    tm, tn, tk = _pick(m, tm), _pick(n, tn), _pick(k, tk)
    nk = k // tk
    grid = (m // tm, n // tn, nk)
    dims = _DIMS[mode]
    out_dtypes = out_dtype if isinstance(out_dtype, (tuple, list)) else (out_dtype,)
    n_main, n_out = 2 + len(extras), len(out_dtypes)

    a_spec = pl.BlockSpec((tk, tm), lambda i, j, kk: (kk, i)) if mode == "tn" else pl.BlockSpec((tm, tk), lambda i, j, kk: (i, kk))
    b_spec = pl.BlockSpec((tn, tk), lambda i, j, kk: (j, kk)) if mode == "nt" else pl.BlockSpec((tk, tn), lambda i, j, kk: (kk, j))
    tile_spec = pl.BlockSpec((tm, tn), lambda i, j, kk: (i, j))
    extras = [e if isinstance(e, tuple) else (e, 0) for e in extras]
    extra_specs = [pl.BlockSpec((tm, tn), functools.partial(lambda i, j, kk, c: (i, j + c), c=c)) for _, c in extras]
    job_ins, job_outs, job_alias = _jobs_io(jobs)
    operands = [a, b, *[e for e, _ in extras], *job_ins]
    in_specs = [a_spec, b_spec] + extra_specs + [HBM_SPEC] * len(job_ins)
    out_shape = [jax.ShapeDtypeStruct((m, n), dt) for dt in out_dtypes] + job_outs
    out_specs = [tile_spec] * n_out + [HBM_SPEC] * len(job_outs)
    aliases = {n_main + i: n_out + o for i, o in job_alias.items()}
    scratch = ([pltpu.VMEM((tm, tn), F32)] if nk > 1 else []) + (_jobs_sems(jobs) if jobs else [])

    def body(*refs):
        a_ref, b_ref = refs[0], refs[1]
        extra_refs = refs[2:n_main]
        job_in_refs = refs[n_main:n_main + len(job_ins)]
        outs = refs[n_main + len(job_ins):]
        o_refs, job_out_refs = outs[:n_out], outs[n_out:n_out + len(job_outs)]
        rest = outs[n_out + len(job_outs):]
        acc_ref = rest[0] if nk > 1 else None
        sems = rest[1:] if nk > 1 else rest
        step = (pl.program_id(0) * grid[1] + pl.program_id(1)) * grid[2] + pl.program_id(2)
        if jobs:
            pl.when(step == 0)(lambda: _jobs_start(jobs, job_in_refs, job_out_refs, sems))

        p = lax.dot_general(a_ref[...], b_ref[...], dims, preferred_element_type=F32)

        def finish(total):
            vals = post(total, *[r[...] for r in extra_refs]) if post is not None else total
            vals = vals if isinstance(vals, (tuple, list)) else (vals,)
            for o_ref, v in zip(o_refs, vals, strict=True):
                o_ref[...] = v.astype(o_ref.dtype)

        if nk == 1:
            finish(p)
        else:
            kk = pl.program_id(2)

            @pl.when(kk == 0)
            def _():
                acc_ref[...] = p

            @pl.when(kk > 0)
            def _():
                acc_ref[...] += p

            @pl.when(kk == nk - 1)
            def _():
                finish(acc_ref[...])

        if jobs:
            pl.when(step == grid[0] * grid[1] * grid[2] - 1)(lambda: _jobs_wait(jobs, job_in_refs, job_out_refs, sems))

    res = pl.pallas_call(
        body,
        name=name,
        grid=grid,
        in_specs=in_specs,
        out_specs=out_specs,
        out_shape=out_shape,
        scratch_shapes=scratch,
        input_output_aliases=aliases,
        compiler_params=_params(("arbitrary", "arbitrary", "arbitrary") if jobs else ("parallel", "parallel", "arbitrary")),
    )(*operands)
    return res[0] if len(res) == 1 else res


def _rowwise(fn, rows, consts, outs, reds, name, *, tr=512, ncol=1, n_rows=None, jobs=()):
    if n_rows is None:
        n_rows = rows[0][0].shape[0]
    tr = min(tr, n_rows)
    assert n_rows % tr == 0
    in_specs, operands = [], []
    for arr, width, coloff, rowoff in rows:
        in_specs.append(pl.BlockSpec((tr, width), functools.partial(lambda j, i, c, r: (i + r, j + c), c=coloff, r=rowoff)))
        operands.append(arr)
    for arr in consts:
        if arr.ndim == 3:
            in_specs.append(pl.BlockSpec(arr.shape, lambda j, i: (0, 0, 0)))
        else:
            in_specs.append(pl.BlockSpec(arr.shape, lambda j, i: (0, 0)))
        operands.append(arr)
    out_specs, out_shape = [], []
    for width, dtype in outs:
        out_specs.append(pl.BlockSpec((tr, width), lambda j, i: (i, j)))
        out_shape.append(jax.ShapeDtypeStruct((n_rows, ncol * width), dtype))
    for shape in reds:
        if len(shape) == 3:
            out_specs.append(pl.BlockSpec(shape, lambda j, i: (0, 0, 0)))
            out_shape.append(jax.ShapeDtypeStruct(shape, F32))
        else:
            out_specs.append(pl.BlockSpec(shape, lambda j, i: (0, j)))
            out_shape.append(jax.ShapeDtypeStruct((shape[0], ncol * shape[1]), F32))
    n_in, n_out, n_red = len(operands), len(outs), len(reds)
    job_ins, job_outs, job_alias = _jobs_io(jobs)
    n_ji, n_jo = len(job_ins), len(job_outs)
    n_steps = ncol * (n_rows // tr)

    def body(*refs):
        job_in_refs = refs[n_in:n_in + n_ji]
        o_refs = refs[n_in + n_ji:n_in + n_ji + n_out]
        red_refs = refs[n_in + n_ji + n_out:n_in + n_ji + n_out + n_red]
        job_out_refs = refs[n_in + n_ji + n_out + n_red:n_in + n_ji + n_out + n_red + n_jo]
        sems = refs[n_in + n_ji + n_out + n_red + n_jo:]
        step = pl.program_id(0) * (n_rows // tr) + pl.program_id(1)
        if jobs:
            pl.when(step == 0)(lambda: _jobs_start(jobs, job_in_refs, job_out_refs, sems))
        vals = fn(*[r[...] for r in refs[:n_in]])
        if not isinstance(vals, (tuple, list)):
            vals = (vals,)
        assert len(vals) == n_out + n_red
        for o_ref, v in zip(o_refs, vals[:n_out]):
            o_ref[...] = v.astype(o_ref.dtype)
        if n_red:
            first = pl.program_id(1) == 0
            for o_ref, v in zip(red_refs, vals[n_out:]):
                @pl.when(first)
                def _(o_ref=o_ref, v=v):
                    o_ref[...] = v

                @pl.when(jnp.logical_not(first))
                def _(o_ref=o_ref, v=v):
                    o_ref[...] += v
        if jobs:
            pl.when(step == n_steps - 1)(lambda: _jobs_wait(jobs, job_in_refs, job_out_refs, sems))

    res = pl.pallas_call(
        body,
        name=name,
        grid=(ncol, n_rows // tr),
        in_specs=in_specs + [HBM_SPEC] * n_ji,
        out_specs=out_specs + [HBM_SPEC] * n_jo,
        out_shape=out_shape + job_outs,
        scratch_shapes=_jobs_sems(jobs) if jobs else [],
        input_output_aliases={n_in + i: n_out + n_red + o for i, o in job_alias.items()},
        compiler_params=_params(("arbitrary", "arbitrary") if jobs else ("parallel", "arbitrary")),
    )(*(operands + job_ins))
    return res


def _row(arr, width=None, coloff=0, rowoff=0):
    return (arr, arr.shape[1] if width is None else width, coloff, rowoff)


def _rms(x, g):
    return x * lax.rsqrt(jnp.mean(x * x, axis=-1, keepdims=True) + EPS) * g


def _erf(x):
    x = jnp.clip(x, -4.0, 4.0)
    x2 = x * x
    alpha = x2 * -2.72614225801306e-10 + 2.77068142495902e-08
    alpha = alpha * x2 - 2.10102402082508e-06
    alpha = alpha * x2 - 5.69250639462346e-05
    alpha = alpha * x2 - 7.34990630326855e-04
    alpha = alpha * x2 - 2.95459980854025e-03
    alpha = alpha * x2 - 1.60960333262415e-02
    beta = x2 * -1.45660718464996e-05 - 2.13374055278905e-04
    beta = beta * x2 - 1.68282697438203e-03
    beta = beta * x2 - 7.37332916720468e-03
    beta = beta * x2 - 1.42647390514189e-02
    return x * alpha / beta


def _gelu(x):
    return 0.5 * x * (1.0 + _erf(x * (2.0 ** -0.5)))


def _gelu_and_grad(x):
    cdf = 0.5 * (1.0 + _erf(x * (2.0 ** -0.5)))
    pdf = jnp.exp(-0.5 * x * x) * (2.0 * jnp.pi) ** -0.5
    return x * cdf, cdf + x * pdf


def _layer_norm(x, g, b):
    mu = jnp.mean(x, axis=-1, keepdims=True)
    xc = x - mu
    return xc * lax.rsqrt(jnp.mean(xc * xc, axis=-1, keepdims=True) + EPS) * g + b


def _per_head(fn, x, g):
    parts = []
    for h in range(HEADS):
        sl = slice(h * HEAD_DIM, (h + 1) * HEAD_DIM)
        parts.append(fn(x[:, sl], g[:, sl]))
    return parts


def _sgu_mask():
    i = lax.broadcasted_iota(jnp.int32, (SGU_LEN, SGU_LEN), 0)
    j = lax.broadcasted_iota(jnp.int32, (SGU_LEN, SGU_LEN), 1)
    return (j // CHUNK) <= (i // CHUNK)


def _sgu_mix(vln_bf, w_bf, bias):
    rows = vln_bf.shape[0]
    out_rows = []
    for c in range(rows // SGU_LEN):
        cols = []
        for g in range(GROUPS):
            blk = vln_bf[c * SGU_LEN:(c + 1) * SGU_LEN, g * GROUP_DIM:(g + 1) * GROUP_DIM]
            mixed = jnp.dot(w_bf[g], blk, preferred_element_type=F32) + bias[g]
            cols.append(mixed)
        out_rows.append(jnp.concatenate(cols, axis=1))
    return out_rows[0] if len(out_rows) == 1 else jnp.concatenate(out_rows, axis=0)


def _split_dots(xs, m_bf):
    his = [x.astype(BF16) for x in xs]
    los = [(x - hi.astype(F32)).astype(BF16) for x, hi in zip(xs, his)]
    res = jnp.dot(jnp.concatenate(his + los, axis=0), m_bf, preferred_element_type=F32)
    n, rows = len(xs), xs[0].shape[0]
    return [res[h * rows:(h + 1) * rows] + res[(n + h) * rows:(n + h + 1) * rows] for h in range(n)]


def _att_tiles(qs, kbs, run_keeps, diag, want_sig=False):
    blk = ATT_BLOCK
    row = lax.broadcasted_iota(jnp.int32, (blk, blk), 0)
    col = lax.broadcasted_iota(jnp.int32, (blk, blk), 1)
    later = (row > col).astype(BF16)
    past = col < row
    zs = [lax.dot_general(q, kb, _DIMS["nt"], preferred_element_type=F32) * (HEAD_DIM ** -0.5) for q, kb in zip(qs, kbs)]
    es = [jnp.exp(-jnp.abs(z)) for z in zs]
    softplus = [jnp.maximum(z, 0.0) + jnp.log(1.0 + e) for z, e in zip(zs, es)]
    log_keeps = [jnp.where(past, -sp, 0.0) if diag else -sp for sp in softplus]
    tails = _split_dots(log_keeps, later)
    weights = [jnp.exp((z - sp) + (tail + keep)) for z, sp, tail, keep in zip(zs, softplus, tails, run_keeps)]
    if diag:
        weights = [jnp.where(past, a, 0.0) for a in weights]
    sigs = None
    if want_sig:
        invs = [1.0 / (1.0 + e) for e in es]
        sigs = [jnp.where(z >= 0.0, inv, e * inv) for z, e, inv in zip(zs, es, invs)]
    return weights, sigs, [jnp.sum(lk, axis=1, keepdims=True) for lk in log_keeps]


def _any_alive(keeps):
    return jnp.max(functools.reduce(jnp.maximum, keeps)) >= LOG_ZERO


def _attention_fwd(qn, kn, vb, *, heads_per_step=8, jobs=()):
    t = qn.shape[0]
    blk = ATT_BLOCK
    nq = t // blk
    hp = heads_per_step
    groups, width = HEADS // hp, hp * HEAD_DIM
    heads = [slice(h * HEAD_DIM, (h + 1) * HEAD_DIM) for h in range(hp)]
    job_ins, job_outs, job_alias = _jobs_io(jobs)
    n_ji, n_jo = len(job_ins), len(job_outs)

    def body(q_ref, k_ref, v_ref, *rest):
        job_in_refs, o_ref, o32_ref = rest[:n_ji], rest[n_ji], rest[n_ji + 1]
        job_out_refs, sems = rest[n_ji + 2:n_ji + 2 + n_jo], rest[n_ji + 2 + n_jo:]
        i = pl.program_id(1)
        step = pl.program_id(0) * nq + i
        if jobs:
            pl.when(step == 0)(lambda: _jobs_start(jobs, job_in_refs, job_out_refs, sems))
        qs = [q_ref[:, sl] for sl in heads]

        def tile(j, keeps, accs, diag):
            rows = pl.ds(pl.multiple_of(j * blk, blk), blk)
            weights, _, keep_sums = _att_tiles(qs, [k_ref[rows, sl] for sl in heads], keeps, diag)
            accs = [acc + jnp.dot(a.astype(BF16), v_ref[rows, sl], preferred_element_type=F32)
                    for acc, a, sl in zip(accs, weights, heads)]
            return tuple(k + s for k, s in zip(keeps, keep_sums)), tuple(accs)

        zero = tuple(jnp.zeros((blk, 1), F32) for _ in heads)
        keeps, accs = tile(i, zero, tuple(jnp.zeros((blk, HEAD_DIM), F32) for _ in heads), True)

        def cond(c):
            return jnp.logical_and(c[0] >= 0, _any_alive(c[1]))

        def step_fn(c):
            j, keeps, accs = c
            keeps, accs = tile(j, keeps, accs, False)
            return j - 1, keeps, accs

        _, _, accs = lax.while_loop(cond, step_fn, (i - 1, keeps, accs))
        for h, sl in enumerate(heads):
            o_ref[:, sl] = accs[h].astype(o_ref.dtype)
            o32_ref[:, sl] = accs[h]
        if jobs:
            pl.when(step == groups * nq - 1)(lambda: _jobs_wait(jobs, job_in_refs, job_out_refs, sems))

    blk_spec = pl.BlockSpec((blk, width), lambda g, i: (i, g))
    head_spec = pl.BlockSpec((t, width), lambda g, i: (0, g), pipeline_mode=pl.Buffered(1))
    return pl.pallas_call(
        body,
        name="attention_fwd",
        grid=(groups, nq),
        in_specs=[blk_spec, head_spec, head_spec] + [HBM_SPEC] * n_ji,
        out_specs=[blk_spec, blk_spec] + [HBM_SPEC] * n_jo,
        out_shape=[jax.ShapeDtypeStruct((t, SB_WIDTH), BF16), jax.ShapeDtypeStruct((t, SB_WIDTH), F32)] + job_outs,
        scratch_shapes=_jobs_sems(jobs) if jobs else [],
        input_output_aliases={3 + i: 2 + o for i, o in job_alias.items()},
        compiler_params=_params(("arbitrary", "arbitrary")),
    )(qn, kn, vb, *job_ins)


def _attention_bwd(qn, kn, vb, do, o32, *, heads_per_step=4, jobs=()):
    t = qn.shape[0]
    blk = ATT_BLOCK
    nq = t // blk
    hp = heads_per_step
    groups, width = HEADS // hp, hp * HEAD_DIM
    heads = [slice(h * HEAD_DIM, (h + 1) * HEAD_DIM) for h in range(hp)]
    job_ins, job_outs, job_alias = _jobs_io(jobs)
    n_ji, n_jo = len(job_ins), len(job_outs)

    def body(q_ref, k_ref, v_ref, do_ref, o32_ref, *rest):
        job_in_refs = rest[:n_ji]
        dq_ref, dk_ref, dv_ref = rest[n_ji:n_ji + 3]
        job_out_refs, sems = rest[n_ji + 3:n_ji + 3 + n_jo], rest[n_ji + 3 + n_jo:]
        i = pl.program_id(1)
        step = pl.program_id(0) * nq + i
        if jobs:
            pl.when(step == 0)(lambda: _jobs_start(jobs, job_in_refs, job_out_refs, sems))

        @pl.when(i == 0)
        def _():
            dk_ref[...] = jnp.zeros_like(dk_ref)
            dv_ref[...] = jnp.zeros_like(dv_ref)

        qs = [q_ref[:, sl] for sl in heads]
        dos = [do_ref[:, sl] for sl in heads]
        totals = [jnp.sum(d.astype(F32) * o32_ref[:, sl], axis=1, keepdims=True) for d, sl in zip(dos, heads)]
        row = lax.broadcasted_iota(jnp.int32, (blk, blk), 0)
        col = lax.broadcasted_iota(jnp.int32, (blk, blk), 1)
        from_here = (row >= col).astype(BF16)

        def tile(j, keeps, run_dlas, dqs, diag):
            rows = pl.ds(pl.multiple_of(j * blk, blk), blk)
            kbs = [k_ref[rows, sl] for sl in heads]
            weights, sigs, keep_sums = _att_tiles(qs, kbs, keeps, diag, want_sig=True)
            wbs = [a.astype(BF16) for a in weights]
            dps = [lax.dot_general(d, v_ref[rows, sl], _DIMS["nt"], preferred_element_type=F32) for d, sl in zip(dos, heads)]
            dlas = [wb.astype(F32) * dp for wb, dp in zip(wbs, dps)]
            later_sums = _split_dots(dlas, from_here)
            dzbs = []
            for dla, sig, total, later, run in zip(dlas, sigs, totals, later_sums, run_dlas):
                dz = dla * (1.0 - sig) - sig * (total - (later + run))
                if diag:
                    dz = jnp.where(col < row, dz, 0.0)
                dzbs.append((dz * (HEAD_DIM ** -0.5)).astype(BF16))
            dqs = [dq + jnp.dot(dzb, kb, preferred_element_type=F32) for dq, dzb, kb in zip(dqs, dzbs, kbs)]
            for sl, dzb, wb, q, d in zip(heads, dzbs, wbs, qs, dos):
                dk_ref[rows, sl] += lax.dot_general(dzb, q, _DIMS["tn"], preferred_element_type=F32)
                dv_ref[rows, sl] += lax.dot_general(wb, d, _DIMS["tn"], preferred_element_type=F32)
            keeps = tuple(k + s for k, s in zip(keeps, keep_sums))
            run_dlas = tuple(r + jnp.sum(dla, axis=1, keepdims=True) for r, dla in zip(run_dlas, dlas))
            return keeps, run_dlas, tuple(dqs)

        zero = tuple(jnp.zeros((blk, 1), F32) for _ in heads)
        keeps, run_dlas, dqs = tile(i, zero, zero, tuple(jnp.zeros((blk, HEAD_DIM), F32) for _ in heads), True)

        def cond(c):
            return jnp.logical_and(c[0] >= 0, _any_alive(c[1]))

        def step_fn(c):
            j, keeps, run_dlas, dqs = c
            keeps, run_dlas, dqs = tile(j, keeps, run_dlas, dqs, False)
            return j - 1, keeps, run_dlas, dqs

        _, _, _, dqs = lax.while_loop(cond, step_fn, (i - 1, keeps, run_dlas, dqs))
        for h, sl in enumerate(heads):
            dq_ref[:, sl] = dqs[h]
        if jobs:
            pl.when(step == groups * nq - 1)(lambda: _jobs_wait(jobs, job_in_refs, job_out_refs, sems))

    blk_spec = pl.BlockSpec((blk, width), lambda g, i: (i, g))
    head_spec = pl.BlockSpec((t, width), lambda g, i: (0, g), pipeline_mode=pl.Buffered(1))
    full = jax.ShapeDtypeStruct((t, SB_WIDTH), F32)
    return pl.pallas_call(
        body,
        name="attention_bwd",
        grid=(groups, nq),
        in_specs=[blk_spec, head_spec, head_spec, blk_spec, blk_spec] + [HBM_SPEC] * n_ji,
        out_specs=[blk_spec, head_spec, head_spec] + [HBM_SPEC] * n_jo,
        out_shape=[full, full, full] + job_outs,
        scratch_shapes=_jobs_sems(jobs) if jobs else [],
        input_output_aliases={5 + i: 3 + o for i, o in job_alias.items()},
        compiler_params=_params(("arbitrary", "arbitrary"), ATT_BWD_VMEM_LIMIT),
    )(qn, kn, vb, do, o32, *job_ins)


def _sum_over_devices(v, jobs=()):
    rows, width = v.shape
    job_ins, job_outs, job_alias = _jobs_io(jobs)
    n_ji, n_jo = len(job_ins), len(job_outs)

    def body(v_ref, *rest):
        job_in_refs, o_ref = rest[:n_ji], rest[n_ji]
        job_out_refs = rest[n_ji + 1:n_ji + 1 + n_jo]
        buf, send_sems, recv_sems = rest[n_ji + 1 + n_jo:n_ji + 4 + n_jo]
        job_sems = rest[n_ji + 4 + n_jo:]
        if jobs:
            _jobs_start(jobs, job_in_refs, job_out_refs, job_sems)
        x, y, c = _place()
        me = 4 * x + 2 * y + c
        buf[me] = v_ref[...]
        sibling = (x, y, 1 - c)
        chips = _other_chips(x, y)

        def copy(k, block, to, src=None):
            slot = buf.at[4 * block[0] + 2 * block[1] + block[2]]
            return pltpu.make_async_remote_copy(
                src_ref=slot if src is None else src, dst_ref=slot, send_sem=send_sems.at[k], recv_sem=recv_sems.at[k],
                device_id=to, device_id_type=MESH)

        first = [copy(0, (x, y, c), sibling, src=v_ref)]
        first += [copy(1 + j, (x, y, c), (px, py, c), src=v_ref) for j, (px, py) in enumerate(chips)]
        for cp in first:
            cp.start()
        passed = [copy(4 + j, (px, py, c), sibling) for j, (px, py) in enumerate(chips)]
        for j, (px, py) in enumerate(chips):
            copy(1 + j, (px, py, c), (x, y, c)).wait_recv()
            passed[j].start()
        copy(0, (x, y, 1 - c), (x, y, c)).wait_recv()
        for j, (px, py) in enumerate(chips):
            copy(4 + j, (px, py, 1 - c), (x, y, c)).wait_recv()
        for cp in first + passed:
            cp.wait_send()
        total = buf[0]
        for d in range(1, N_DEV):
            total = total + buf[d]
        o_ref[...] = total
        if jobs:
            _jobs_wait(jobs, job_in_refs, job_out_refs, job_sems)

    vmem = pl.BlockSpec(memory_space=pltpu.VMEM)
    return pl.pallas_call(
        body,
        name="sum_over_devices",
        in_specs=[vmem] + [HBM_SPEC] * n_ji,
        out_specs=[vmem] + [HBM_SPEC] * n_jo,
        out_shape=[jax.ShapeDtypeStruct((rows, width), F32)] + job_outs,
        scratch_shapes=[pltpu.VMEM((N_DEV, rows, width), F32), pltpu.SemaphoreType.DMA((N_DEV - 1,)), pltpu.SemaphoreType.DMA((N_DEV - 1,))]
        + (_jobs_sems(jobs) if jobs else []),
        input_output_aliases={1 + i: 1 + o for i, o in job_alias.items()},
        compiler_params=pltpu.CompilerParams(vmem_limit_bytes=V7X_VMEM_LIMIT),
    )(v, *job_ins)


def _adamw_math(w, g, m, v):
    m = ADAM_B1 * m + (1.0 - ADAM_B1) * g
    v = ADAM_B2 * v + (1.0 - ADAM_B2) * (g * g)
    m_hat = m / (1.0 - ADAM_B1 ** ADAM_STEP)
    v_hat = v / (1.0 - ADAM_B2 ** ADAM_STEP)
    delta = -ADAM_LR * (m_hat / (jnp.sqrt(v_hat) + ADAM_EPS) + ADAM_WD * w)
    return delta, m, v


def _adamw_sharded(mine, theirs, w, m, v, name):
    width = _pick(w.shape[1], 1152)
    ncol = w.shape[1] // width
    shard_rows = w.shape[0]
    tr = min(256, shard_rows)

    def slots(arr):
        return [_row(arr, width, 0, s * (shard_rows // tr)) for s in range(4)]

    def total(a, b, c, d):
        return ((a.astype(F32) + b.astype(F32)) + c.astype(F32)) + d.astype(F32)

    def fn(a0, a1, a2, a3, b0, b1, b2, b3, w, m, v):
        g = total(a0, a1, a2, a3) + total(b0, b1, b2, b3)
        delta, m, v = _adamw_math(w, g, m, v)
        return g, delta, m, v

    return _rowwise(fn, slots(mine) + slots(theirs) + [_row(t, width) for t in (w, m, v)], [], [(width, F32)] * 4, [], name,
                    tr=tr, ncol=ncol, n_rows=shard_rows)


def _adamw_small(g, w, m, v):
    def fn(g, w, m, v):
        return _adamw_math(w, g, m, v)

    return _rowwise(fn, [_row(t) for t in (g, w, m, v)], [], [(g.shape[1], F32)] * 3, [], "adamw_small", tr=g.shape[0])


def _flat(a):
    return a.reshape(-1, a.shape[-1])


def kernel(x, g_mix, w_in, g_q, g_k, sgu_ln_g, sgu_ln_b, w_spatial, b_spatial, w_oa, w_ob, w_out, g_ff, w_ff1, w_ff2, loss_target, m_g_mix, m_w_in, m_g_q, m_g_k, m_sgu_ln_g, m_sgu_ln_b, m_w_spatial, m_b_spatial, m_w_oa, m_w_ob, m_w_out, m_g_ff, m_w_ff1, m_w_ff2, v_g_mix, v_w_in, v_g_q, v_g_k, v_sgu_ln_g, v_sgu_ln_b, v_w_spatial, v_b_spatial, v_w_oa, v_w_ob, v_w_out, v_g_ff, v_w_ff1, v_w_ff2):
    depth = g_mix.shape[0]
    seq, d_model = x.shape[1], x.shape[2]
    d_ff = w_ff1.shape[2] * N_CHIPS
    in_cols = w_in.shape[2] * N_CHIPS
    col_gate_a = 3 * SB_WIDTH + 2 * SGU_WIDTH
    assert in_cols == col_gate_a + 2 * d_model

    big = [w_in, w_oa, w_ob, w_out, w_ff1, w_ff2]
    big_m = [m_w_in, m_w_oa, m_w_ob, m_w_out, m_w_ff1, m_w_ff2]
    big_v = [v_w_in, v_w_oa, v_w_ob, v_w_out, v_w_ff1, v_w_ff2]
    axes = [2, 2, 2, 1, 2, 1]
    W_IN, W_OA, W_OB, W_OUT, W_FF1, W_FF2 = range(6)
    shards = [w.astype(BF16) for w in big]
    full = [[None] * depth for _ in big]

    def gather(t, l):
        return _gather_job(shards[t], l, axes[t])

    def forward(t, l):
        return _forward_job(full[t][l], axes[t])

    def add_residual(total, res):
        return total + res

    (full[W_IN][0],) = _run_jobs([gather(W_IN, 0)], "gather_first")
    (full[W_IN][0],) = _run_jobs([forward(W_IN, 0)], "forward_first")

    gate_w = _pick(d_model, 1024)
    bias_col = b_spatial[..., None]

    def rms_fwd(xin, g, name):
        return _rowwise(lambda xb, gb: _rms(xb, gb), [_row(xin)], [g], [(d_model, BF16)], [], name)[0]

    saved = []
    cur = x.reshape(seq, d_model)
    for l in range(depth):
        hb = rms_fwd(cur, g_mix[l:l + 1], "rms_mix")
        early = [W_FF1, W_OA, W_OB, W_OUT]
        proj, *got = _matmul(hb, full[W_IN][l], "nn", F32, "proj", jobs=[gather(t, l) for t in early])
        for t, w in zip(early, got):
            full[t][l] = w

        def qk_fn(q, k, v, gq, gk):
            qn = jnp.concatenate(_per_head(_rms, q, gq), axis=1)
            kn = jnp.concatenate(_per_head(_rms, k, gk), axis=1)
            return qn, kn, v

        gq, gk = g_q[l].reshape(1, SB_WIDTH), g_k[l].reshape(1, SB_WIDTH)
        qn, kn, vb = _rowwise(qk_fn, [_row(proj, SB_WIDTH, 0), _row(proj, SB_WIDTH, 1), _row(proj, SB_WIDTH, 2)],
                              [gq, gk], [(SB_WIDTH, BF16)] * 3, [], "qk_norm")
        o, o32, *got = _attention_fwd(qn, kn, vb, jobs=[forward(t, l) for t in early] + [gather(W_FF2, l)])
        for t, w in zip(early + [W_FF2], got):
            full[t][l] = w
        ya = _matmul(o, full[W_OA][l], "nn", F32, "proj_a")

        def sgu_fn(u_pre, v_pre, ln_g, ln_b, w_s, b_t):
            w_bf = jnp.where(_sgu_mask()[None], w_s, 0.0).astype(BF16)
            vln = _layer_norm(_gelu(v_pre), ln_g, ln_b)
            return _gelu(u_pre) * _sgu_mix(vln.astype(BF16), w_bf, b_t)

        sgu_consts = [sgu_ln_g[l:l + 1], sgu_ln_b[l:l + 1], w_spatial[l], bias_col[l]]
        s = _rowwise(sgu_fn, [_row(proj, SGU_WIDTH, 3), _row(proj, SGU_WIDTH, 4)], sgu_consts, [(SGU_WIDTH, BF16)], [], "sgu_fwd")[0]
        def merge_fn(b, ga, gb, a):
            return b, jax.nn.sigmoid(ga) * a + jax.nn.sigmoid(gb) * b

        gates = [(proj, col_gate_a // gate_w), (proj, (col_gate_a + d_model) // gate_w)]
        yb, merged = _matmul(s, full[W_OB][l], "nn", (F32, BF16), "proj_b", extras=gates + [ya], post=merge_fn, tm=512, tn=gate_w)
        x1 = _matmul(merged, full[W_OUT][l], "nn", F32, "proj_out", extras=[cur], post=add_residual)
        h2 = rms_fwd(x1, g_ff[l:l + 1], "rms_ff")
        more = l + 1 < depth
        a1, r, full[W_FF2][l], *got = _matmul(h2, full[W_FF1][l], "nn", (F32, BF16), "ff1",
                                              jobs=[forward(W_FF2, l)] + ([gather(W_IN, l + 1)] if more else []),
                                              post=lambda total: (total, jnp.square(jnp.maximum(total, 0.0))))
        if more:
            full[W_IN][l + 1] = got[0]
            x2, full[W_IN][l + 1] = _matmul(r, full[W_FF2][l], "nn", F32, "ff2", extras=[x1], post=add_residual,
                                            jobs=[forward(W_IN, l + 1)])
        else:
            x2 = _matmul(r, full[W_FF2][l], "nn", F32, "ff2", extras=[x1], post=add_residual)
        saved.append(dict(x=cur, hb=hb, proj=proj, qn=qn, kn=kn, vb=vb, o=o, o32=o32, ya=ya, yb=yb, s=s, merged=merged,
                          x1=x1, h2=h2, a1=a1, r=r, gq=gq, gk=gk, sgu_consts=sgu_consts, gates=gates))
        cur = x2

    def loss_fn(y, target):
        err = y - target
        per_row = jnp.sum(err * err, axis=-1, keepdims=True) * (1.0 / d_model)
        part = 0.5 * jnp.sum(per_row, axis=0, keepdims=True)
        dy = err * (1.0 / d_model)
        return dy, dy, jnp.broadcast_to(part, (8, LANE))

    dx, dxb, loss_part = _rowwise(loss_fn, [_row(cur), _row(loss_target.reshape(seq, d_model))], [],
                                  [(d_model, F32), (d_model, BF16)], [(8, LANE)], "loss")
    loss = lax.psum(loss_part[0, 0], ("x", "y", "c"))

    landed = [None] * len(big)

    theirs = [None] * len(big)

    def scatter(t, grad, l):
        return _scatter_job(grad, landed[t], (4,) + shards[t].shape, l, axes[t])

    def mirror(t, l):
        return _mirror_job(landed[t], theirs[t], l)

    small = {n: [None] * depth for n in ("g_mix", "g_q", "g_k", "ln_g", "ln_b", "w_s", "b_s", "g_ff")}

    def rms_bwd(dh, xin, dres, g, name, jobs=()):
        def fn(dh, xin, dres, g):
            _, vjp = jax.vjp(_rms, xin, g)
            dxin, dg = vjp(dh)
            total = dres + dxin
            return total, total, dg

        return _rowwise(fn, [_row(dh), _row(xin), _row(dres)], [g], [(d_model, F32), (d_model, BF16)], [(1, d_model)], name, jobs=jobs)

    for l in reversed(range(depth)):
        sv = saved[l]
        da1 = _matmul(dxb, full[W_FF2][l], "nt", BF16, "ff2_dx", extras=[sv["a1"]],
                      post=lambda total, a: total * (2.0 * jnp.maximum(a, 0.0)),
                      jobs=[mirror(t, l + 1) for t in (W_IN, W_FF1)] if l + 1 < depth else [])
        grad_ff2 = _matmul(sv["r"], dxb, "tn", BF16, "ff2_dw", tk=WGRAD_TK,
                           jobs=[mirror(t, l + 1) for t in (W_FF2, W_OUT, W_OA, W_OB)] if l + 1 < depth else [])
        if l + 1 < depth:
            da1, theirs[W_IN], theirs[W_FF1] = da1
            grad_ff2, theirs[W_FF2], theirs[W_OUT], theirs[W_OA], theirs[W_OB] = grad_ff2
        dh2 = _matmul(da1, full[W_FF1][l], "nt", F32, "ff1_dx")
        grad_ff1 = _matmul(sv["h2"], da1, "tn", BF16, "ff1_dw", tk=WGRAD_TK)
        dx1, dx1b, small["g_ff"][l] = rms_bwd(dh2, sv["x1"], dx, g_ff[l:l + 1], "rms_ff_bwd")

        def merge_bwd_fn(dm, ga, gb, a, b):
            sa, sb = jax.nn.sigmoid(ga), jax.nn.sigmoid(gb)
            return dm * a * sa * (1.0 - sa), dm * b * sb * (1.0 - sb), dm * sa, dm * sb

        dga, dgb, dya, dyb = _matmul(dx1b, full[W_OUT][l], "nt", (BF16,) * 4, "out_dx", tm=512, tn=gate_w,
                                     extras=sv["gates"] + [sv["ya"], sv["yb"]], post=merge_bwd_fn)
        grad_out = _matmul(sv["merged"], dx1b, "tn", BF16, "out_dw", tk=WGRAD_TK)

        ds = _matmul(dyb, full[W_OB][l], "nt", F32, "ob_dx")
        grad_ob = _matmul(sv["s"], dyb, "tn", BF16, "ob_dw", tk=WGRAD_TK)

        def sgu_bwd_fn(ds, u_pre, v_pre, ln_g, ln_b, w_s, b_t):
            mask = _sgu_mask()
            w_bf = jnp.where(mask[None], w_s, 0.0).astype(BF16)
            u, u_grad = _gelu_and_grad(u_pre)
            vg, vg_grad = _gelu_and_grad(v_pre)
            vln, ln_vjp = jax.vjp(_layer_norm, vg, ln_g, ln_b)
            vln_bf = vln.astype(BF16)
            mixed = _sgu_mix(vln_bf, w_bf, b_t)
            du_pre = ds * mixed * u_grad
            dmixed = ds * u
            dm_bf = dmixed.astype(BF16)
            dw = [jnp.zeros((SGU_LEN, SGU_LEN), F32) for _ in range(GROUPS)]
            db = [jnp.zeros((SGU_LEN, 1), F32) for _ in range(GROUPS)]
            dvln_rows = []
            for c in range(ds.shape[0] // SGU_LEN):
                rows = slice(c * SGU_LEN, (c + 1) * SGU_LEN)
                cols = []
                for g in range(GROUPS):
                    sl = slice(g * GROUP_DIM, (g + 1) * GROUP_DIM)
                    cols.append(lax.dot_general(w_bf[g], dm_bf[rows, sl], _DIMS["tn"], preferred_element_type=F32))
                    dw[g] = dw[g] + lax.dot_general(dm_bf[rows, sl], vln_bf[rows, sl], _DIMS["nt"], preferred_element_type=F32)
                    db[g] = db[g] + jnp.sum(dmixed[rows, sl], axis=1, keepdims=True)
                dvln_rows.append(jnp.concatenate(cols, axis=1))
            dvln = dvln_rows[0] if len(dvln_rows) == 1 else jnp.concatenate(dvln_rows, axis=0)
            dvg, dln_g, dln_b = ln_vjp(dvln)
            dv_pre = dvg * vg_grad
            dw_s = jnp.stack([jnp.where(mask, d, 0.0) for d in dw])
            return du_pre, dv_pre, dln_g, dln_b, dw_s, jnp.stack(db)

        du, dvs, small["ln_g"][l], small["ln_b"][l], small["w_s"][l], db_col = _rowwise(
            sgu_bwd_fn, [_row(ds), _row(sv["proj"], SGU_WIDTH, 3), _row(sv["proj"], SGU_WIDTH, 4)], sv["sgu_consts"],
            [(SGU_WIDTH, BF16)] * 2, [(1, SGU_WIDTH), (1, SGU_WIDTH), (GROUPS, SGU_LEN, SGU_LEN), (GROUPS, SGU_LEN, 1)], "sgu_bwd", tr=256)
        small["b_s"][l] = db_col[..., 0]

        do = _matmul(dya, full[W_OA][l], "nt", BF16, "oa_dx")
        grad_oa = _matmul(sv["o"], dya, "tn", BF16, "oa_dw", tk=WGRAD_TK)
        dqn, dkn, dv, landed[W_FF2], landed[W_OUT], landed[W_OB], landed[W_OA] = _attention_bwd(
            sv["qn"], sv["kn"], sv["vb"], do, sv["o32"],
            jobs=[scatter(W_FF2, grad_ff2, l), scatter(W_OUT, grad_out, l), scatter(W_OB, grad_ob, l), scatter(W_OA, grad_oa, l)])

        def qk_bwd_fn(dqn, dkn, dv, q, k, gq, gk):
            outs = []
            for d, xin, g in ((dqn, q, gq), (dkn, k, gk)):
                dxs, dgs = [], []
                for h in range(HEADS):
                    sl = slice(h * HEAD_DIM, (h + 1) * HEAD_DIM)
                    _, vjp = jax.vjp(_rms, xin[:, sl], g[:, sl])
                    dxh, dgh = vjp(d[:, sl])
                    dxs.append(dxh)
                    dgs.append(dgh)
                outs.append((jnp.concatenate(dxs, axis=1), jnp.concatenate(dgs, axis=1)))
            return outs[0][0], outs[1][0], dv, outs[0][1], outs[1][1]

        dq, dk, dvb, small["g_q"][l], small["g_k"][l] = _rowwise(
            qk_bwd_fn, [_row(dqn), _row(dkn), _row(dv), _row(sv["proj"], SB_WIDTH, 0), _row(sv["proj"], SB_WIDTH, 1)],
            [sv["gq"], sv["gk"]], [(SB_WIDTH, BF16)] * 3, [(1, SB_WIDTH), (1, SB_WIDTH)], "qk_norm_bwd")

        dproj = jnp.concatenate([dq, dk, dvb, du, dvs, dga, dgb], axis=1)
        grad_in, landed[W_FF1] = _matmul(sv["hb"], dproj, "tn", BF16, "in_dw", tk=WGRAD_TK, jobs=[scatter(W_FF1, grad_ff1, l)])
        early = (W_FF2, W_OUT, W_OA, W_OB) if l == 0 else ()
        dh, landed[W_IN], *got = _matmul(dproj, full[W_IN][l], "nt", F32, "in_dx", tk=3072,
                                         jobs=[scatter(W_IN, grad_in, l)] + [mirror(t, 0) for t in early])
        for t, b in zip(early, got):
            theirs[t] = b
        dx, dxb, small["g_mix"][l], *got = rms_bwd(dh, sv["x"], dx1, g_mix[l:l + 1], "rms_mix_bwd",
                                                   jobs=[])
        if got:
            theirs[W_FF1] = got[0]

    names = ["g_mix", "g_q", "g_k", "ln_g", "ln_b", "w_s", "b_s", "g_ff"]
    small_w = [g_mix, g_q, g_k, sgu_ln_g, sgu_ln_b, w_spatial, b_spatial, g_ff]
    small_m = [m_g_mix, m_g_q, m_g_k, m_sgu_ln_g, m_sgu_ln_b, m_w_spatial, m_b_spatial, m_g_ff]
    small_v = [v_g_mix, v_g_q, v_g_k, v_sgu_ln_g, v_sgu_ln_b, v_w_spatial, v_b_spatial, v_g_ff]

    def pack(parts):
        return jnp.concatenate([p.reshape(-1, LANE) for p in parts], axis=0)

    local_small = pack([jnp.stack(small[n]) for n in names])
    g_small, theirs[W_IN], theirs[W_FF1] = _sum_over_devices(local_small, [mirror(W_IN, 0), mirror(W_FF1, 0)])
    d_small, m_small, v_small = _adamw_small(g_small, pack(small_w), pack(small_m), pack(small_v))

    big_out = []
    for t in range(len(big)):
        res = _adamw_sharded(_flat(landed[t]), _flat(theirs[t]), _flat(big[t]), _flat(big_m[t]), _flat(big_v[t]), "adamw")
        big_out.append([r.reshape(big[t].shape) for r in res])

    def unpack(packed):
        outs, row = [], 0
        for w in small_w:
            n = w.size // LANE
            outs.append(packed[row:row + n].reshape(w.shape))
            row += n
        return outs

    small_out = [unpack(p) for p in (g_small, d_small, m_small, v_small)]

    order = [("s", 0), ("b", W_IN), ("s", 1), ("s", 2), ("s", 3), ("s", 4), ("s", 5), ("s", 6),
             ("b", W_OA), ("b", W_OB), ("b", W_OUT), ("s", 7), ("b", W_FF1), ("b", W_FF2)]
    result = [loss, dx.reshape(x.shape)]
    for kind in range(4):
        for which, idx in order:
            result.append(small_out[kind][idx] if which == "s" else big_out[idx][kind])
    return tuple(result)
```
